```python
import math
import jax, jax.numpy as jnp
from jax import lax
import numpy as np

D_MODEL = 1024
BATCH = 2
SEQ = 8192
DEPTH = 1

SSM_GROUP = 16
SSM_GROUPS = D_MODEL // 32
SSM_WIDTH = SSM_GROUP * SSM_GROUPS
SSM_STATE = 64
DT_MIN = 1e-3
DT_MAX = 1e-1
FOX_HEAD_DIM = 64
FOX_HEADS = D_MODEL // 128
FOX_WIDTH = FOX_HEADS * FOX_HEAD_DIM
Q_BLOCK = 128
MEM_LEN = 256
MEM_HEADS = 4
MEM_HEAD_DIM = 128
MEM_WIDTH = MEM_HEADS * MEM_HEAD_DIM
FFN_HIDDEN = -(-8 * D_MODEL // (3 * 256)) * 256
N_BRANCHES = 2
RMS_EPS = 1e-6
SPLIT_Q = SSM_WIDTH
SPLIT_K = SPLIT_Q + FOX_WIDTH
SPLIT_V = SPLIT_K + FOX_WIDTH
SPLIT_F = SPLIT_V + FOX_WIDTH
SPLIT_G = SPLIT_F + FOX_HEADS
IN_WIDTH = SPLIT_G + N_BRANCHES * D_MODEL

kernel_name = "hybrid_s5_fox_gated_block"


def rms_norm(x, gain):
    xf = x.astype(jnp.float32)
    y = xf * lax.rsqrt(jnp.mean(xf * xf, axis=-1, keepdims=True) + RMS_EPS)
    return (y * gain.astype(jnp.float32)).astype(x.dtype)


def _linear_recurrence_op(left, right):
    a1, b1 = left
    a2, b2 = right
    return a1 * a2, a2 * b1 + b2


def s5_ssm(u, lam_re, lam_im, log_dt, b_re, b_im, c_re, c_im, d_skip):
    bsz, seq, _ = u.shape
    f32 = jnp.float32
    uf = u.astype(f32).reshape(bsz, seq, SSM_GROUPS, SSM_GROUP)
    lam = lax.complex(lam_re.astype(f32), lam_im.astype(f32))
    dt = jnp.exp(log_dt.astype(f32))[:, None]
    lam_bar = jnp.exp(lam * dt)
    b = lax.complex(b_re.astype(f32), b_im.astype(f32))
    b_bar = ((lam_bar - 1.0) / lam)[..., None] * b
    c = lax.complex(c_re.astype(f32), c_im.astype(f32))
    bu = jnp.einsum('gpn,blgn->blgp', b_bar, uf.astype(jnp.complex64))
    a = jnp.broadcast_to(lam_bar, bu.shape)
    _, states = lax.associative_scan(_linear_recurrence_op, (a, bu), axis=1)
    y = jnp.einsum('gnp,blgp->blgn', c, states).real
    y = y + d_skip.astype(f32).reshape(SSM_GROUPS, SSM_GROUP) * uf
    return y.reshape(bsz, seq, SSM_WIDTH).astype(u.dtype)


def forgetting_attention(q, k, v, f_logit):
    bsz, seq, n_heads, head_dim = q.shape
    log_f = jax.nn.log_sigmoid(f_logit.astype(jnp.float32))
    cum = jnp.cumsum(log_f, axis=1).transpose(0, 2, 1)
    kh = k.transpose(0, 2, 1, 3)
    vh = v.transpose(0, 2, 1, 3)
    n_blocks = seq // Q_BLOCK
    q_blocks = q.transpose(0, 2, 1, 3).reshape(bsz, n_heads, n_blocks, Q_BLOCK, head_dim).transpose(2, 0, 1, 3, 4)
    c_blocks = cum.reshape(bsz, n_heads, n_blocks, Q_BLOCK).transpose(2, 0, 1, 3)
    starts = jnp.arange(n_blocks, dtype=jnp.int32) * Q_BLOCK
    key_pos = jnp.arange(seq, dtype=jnp.int32)
    scale = head_dim ** -0.5

    def one_block(args):
        qb, cb, start = args
        s = jnp.einsum('bhqd,bhkd->bhqk', qb, kh).astype(jnp.float32) * scale
        s = s + cb[..., :, None] - cum[..., None, :]
        q_pos = start + jnp.arange(Q_BLOCK, dtype=jnp.int32)
        causal = key_pos[None, :] <= q_pos[:, None]
        s = jnp.where(causal, s, -jnp.inf)
        p = jax.nn.softmax(s, axis=-1)
        return jnp.einsum('bhqk,bhkd->bhqd', p.astype(vh.dtype), vh)

    out = lax.map(one_block, (q_blocks, c_blocks, starts))
    return out.transpose(1, 0, 3, 2, 4).reshape(bsz, seq, n_heads * head_dim)


def memory_cross_attention(n, m, w_q, w_kv, w_o):
    bsz, seq, _ = n.shape
    mem_len = m.shape[1]
    q = (n @ w_q).reshape(bsz, seq, MEM_HEADS, MEM_HEAD_DIM)
    k, v = jnp.split(m @ w_kv, 2, axis=-1)
    k = k.reshape(bsz, mem_len, MEM_HEADS, MEM_HEAD_DIM)
    v = v.reshape(bsz, mem_len, MEM_HEADS, MEM_HEAD_DIM)
    s = jnp.einsum('blhd,bmhd->bhlm', q, k).astype(jnp.float32) * (MEM_HEAD_DIM ** -0.5)
    p = jax.nn.softmax(s, axis=-1)
    o = jnp.einsum('bhlm,bmhd->blhd', p.astype(v.dtype), v).reshape(bsz, seq, MEM_WIDTH)
    return o @ w_o


def setup_inputs(seed: int = 0) -> dict:
    key = jax.random.key(seed)
    ks = jax.random.split(key, 32)
    f32 = jnp.float32
    L = DEPTH

    def nrm(k, shape, fan_in):
        return jax.random.normal(k, shape, f32) * (fan_in ** -0.5)

    def gain(k, shape):
        return 1.0 + 0.01 * jax.random.normal(k, shape, f32)

    n_idx = jnp.arange(SSM_STATE, dtype=f32)
    lam_re = -0.5 + 0.01 * jax.random.normal(ks[4], (L, SSM_GROUPS, SSM_STATE), f32)
    lam_im = math.pi * n_idx + 0.01 * jax.random.normal(ks[5], (L, SSM_GROUPS, SSM_STATE), f32)
    log_dt = jax.random.uniform(ks[6], (L, SSM_GROUPS), f32, math.log(DT_MIN), math.log(DT_MAX))
    return {
        "x": jax.random.normal(ks[0], (BATCH, SEQ, D_MODEL), f32),
        "mem": jax.random.normal(ks[1], (BATCH, MEM_LEN, D_MODEL), f32),
        "norm_mix": gain(ks[2], (L, D_MODEL)),
        "w_in": nrm(ks[3], (L, D_MODEL, IN_WIDTH), D_MODEL),
        "b_forget": jax.random.uniform(ks[7], (L, FOX_HEADS), f32, 1.0, 5.0),
        "lam_re": lam_re,
        "lam_im": lam_im,
        "log_dt": log_dt,
        "b_re": nrm(ks[8], (L, SSM_GROUPS, SSM_STATE, SSM_GROUP), 2 * SSM_GROUP),
        "b_im": nrm(ks[9], (L, SSM_GROUPS, SSM_STATE, SSM_GROUP), 2 * SSM_GROUP),
        "c_re": nrm(ks[10], (L, SSM_GROUPS, SSM_GROUP, SSM_STATE), SSM_STATE),
        "c_im": nrm(ks[11], (L, SSM_GROUPS, SSM_GROUP, SSM_STATE), SSM_STATE),
        "d_skip": jax.random.normal(ks[12], (L, SSM_WIDTH), f32),
        "w_glu": nrm(ks[13], (L, SSM_WIDTH, 2 * D_MODEL), SSM_WIDTH),
        "w_fox_o": nrm(ks[14], (L, FOX_WIDTH, D_MODEL), FOX_WIDTH),
        "w_mix_out": nrm(ks[15], (L, D_MODEL, D_MODEL), D_MODEL),
        "norm_mem_q": gain(ks[16], (L, D_MODEL)),
        "norm_mem_kv": gain(ks[17], (L, D_MODEL)),
        "w_mem_q": nrm(ks[18], (L, D_MODEL, MEM_WIDTH), D_MODEL),
        "w_mem_kv": nrm(ks[19], (L, D_MODEL, 2 * MEM_WIDTH), D_MODEL),
        "w_mem_o": nrm(ks[20], (L, MEM_WIDTH, D_MODEL), MEM_WIDTH),
        "norm_ffn": gain(ks[21], (L, D_MODEL)),
        "w_ffn_in": nrm(ks[22], (L, D_MODEL, 2 * FFN_HIDDEN), D_MODEL),
        "w_ffn_out": nrm(ks[23], (L, FFN_HIDDEN, D_MODEL), FFN_HIDDEN),
        "norm_final": gain(ks[24], (D_MODEL,)),
    }


def reference(x, mem, norm_mix, w_in, b_forget, lam_re, lam_im, log_dt, b_re, b_im, c_re, c_im,
              d_skip, w_glu, w_fox_o, w_mix_out, norm_mem_q, norm_mem_kv, w_mem_q, w_mem_kv,
              w_mem_o, norm_ffn, w_ffn_in, w_ffn_out, norm_final):
    bsz, seq, _ = x.shape
    h = x
    for l in range(DEPTH):
        u = rms_norm(h, norm_mix[l])
        proj = u @ w_in[l]
        u_ssm, q, k, v, f_logit, gate_logits = jnp.split(
            proj, [SPLIT_Q, SPLIT_K, SPLIT_V, SPLIT_F, SPLIT_G], axis=-1)
        y_ssm = jax.nn.gelu(s5_ssm(u_ssm, lam_re[l], lam_im[l], log_dt[l], b_re[l], b_im[l],
                                   c_re[l], c_im[l], d_skip[l]))
        glu_a, glu_b = jnp.split(y_ssm @ w_glu[l], 2, axis=-1)
        out_a = glu_a * jax.nn.sigmoid(glu_b)
        att = forgetting_attention(
            q.reshape(bsz, seq, FOX_HEADS, FOX_HEAD_DIM),
            k.reshape(bsz, seq, FOX_HEADS, FOX_HEAD_DIM),
            v.reshape(bsz, seq, FOX_HEADS, FOX_HEAD_DIM),
            f_logit + b_forget[l])
        out_b = att @ w_fox_o[l]
        gate_a, gate_b = jnp.split(jax.nn.sigmoid(gate_logits), 2, axis=-1)
        h = h + (gate_a * out_a + gate_b * out_b) @ w_mix_out[l]
        h = h + memory_cross_attention(rms_norm(h, norm_mem_q[l]), rms_norm(mem, norm_mem_kv[l]),
                                       w_mem_q[l], w_mem_kv[l], w_mem_o[l])
        f_a, f_b = jnp.split(rms_norm(h, norm_ffn[l]) @ w_ffn_in[l], 2, axis=-1)
        h = h + (jax.nn.silu(f_a) * f_b) @ w_ffn_out[l]
    return rms_norm(h, norm_final)
```

```python
import functools
import math

import jax
import jax.numpy as jnp
from jax import lax
from jax.experimental import pallas as pl
from jax.experimental.pallas import tpu as pltpu

F32 = jnp.float32
BF16 = jnp.bfloat16

RMS_EPS = 1e-6
SSM_GROUP = 16
SSM_STATE = 64
FOX_HEAD_DIM = 64
MEM_HEADS = 4
LANES = 128
SUBLANES = 8
VMEM_LIMIT = 56 * 1024 * 1024

TM_PROJ = 512
TM_MIX = 512
TM_FFN = 256
SSM_CHUNK = 256
SSM_SEG = SSM_CHUNK // SUBLANES
SSM_COLS = 512
FOX_TQ = 256
FOX_TK = 256


def _cparams(sem):
    return pltpu.CompilerParams(dimension_semantics=sem, vmem_limit_bytes=VMEM_LIMIT)


def _rms(x, gain):
    return x * lax.rsqrt(jnp.mean(x * x, axis=-1, keepdims=True) + RMS_EPS) * gain


def _dot(a, b):
    return jnp.dot(a, b, preferred_element_type=F32)


def _dot_nt(a, b):
    return lax.dot_general(a, b, (((1,), (1,)), ((), ())), preferred_element_type=F32)


def _full(shape):
    n = len(shape)
    return pl.BlockSpec(shape, lambda *_: (0,) * n)


def _ssm_prep_kernel(lre_ref, lim_ref, ldt_ref, bre_ref, bim_ref,
                     are_ref, aim_ref, sre_ref, sim_ref, bbre_ref, bbim_ref):
    lre = lre_ref[...]
    lim = lim_ref[...]
    dt = jnp.exp(ldt_ref[...])
    zr = lre * dt
    zi = lim * dt
    mag = jnp.exp(zr)
    are = mag * jnp.cos(zi)
    aim = mag * jnp.sin(zi)
    are_ref[...] = are
    aim_ref[...] = aim
    mag_s = jnp.exp(zr * SSM_SEG)
    sre_ref[...] = mag_s * jnp.cos(zi * SSM_SEG)
    sim_ref[...] = mag_s * jnp.sin(zi * SSM_SEG)
    nr = are - 1.0
    ni = aim
    den = lre * lre + lim * lim
    fr = (nr * lre + ni * lim) / den
    fi = (ni * lre - nr * lim) / den
    bre = bre_ref[...]
    bim = bim_ref[...]
    bbre_ref[...] = fr * bre - fi * bim
    bbim_ref[...] = fr * bim + fi * bre


def _ssm_prep(lam_re, lam_im, log_dt, b_re, b_im):
    g, p = lam_re.shape
    n = b_re.shape[-1]
    c = g * p
    col = lambda a: a.reshape(c, 1)
    ldt = jnp.broadcast_to(log_dt[:, None], (g, p))
    outs = pl.pallas_call(
        _ssm_prep_kernel,
        out_shape=[jax.ShapeDtypeStruct((c, 1), F32)] * 4 + [jax.ShapeDtypeStruct((c, n), F32)] * 2,
        name="ssm_prep",
    )(col(lam_re), col(lam_im), col(ldt), b_re.reshape(c, n), b_im.reshape(c, n))
    return outs


def _mem_kv_kernel(m_ref, g_ref, w_ref, k_ref, v_ref):
    n = _rms(m_ref[0], g_ref[...]).astype(BF16)
    kv = _dot(n, w_ref[...])
    half = kv.shape[-1] // 2
    k_ref[0] = kv[:, :half].astype(BF16)
    v_ref[0] = kv[:, half:].astype(BF16)


def _mem_kv(mem, gain, w_kv):
    b, m, d = mem.shape
    w2 = w_kv.shape[-1]
    return pl.pallas_call(
        _mem_kv_kernel,
        grid=(b,),
        in_specs=[pl.BlockSpec((1, m, d), lambda i: (i, 0, 0)), _full((1, d)), _full((d, w2))],
        out_specs=[pl.BlockSpec((1, m, w2 // 2), lambda i: (i, 0, 0))] * 2,
        out_shape=[jax.ShapeDtypeStruct((b, m, w2 // 2), BF16)] * 2,
        compiler_params=_cparams(("arbitrary",)),
        name="mem_kv",
    )(mem, gain.reshape(1, d), w_kv.astype(BF16))


def _in_proj_kernel(x_ref, g_ref, wu_ref, wq_ref, wk_ref, wv_ref, wg_ref, wf_ref, bf_ref,
                    u_ref, q_ref, k_ref, v_ref, gate_ref, cum_ref, carry_ref):
    @pl.when(pl.program_id(1) == 0)
    def _():
        carry_ref[...] = jnp.zeros_like(carry_ref)

    un = _rms(x_ref[...], g_ref[...]).astype(BF16)
    u_ref[...] = _dot(un, wu_ref[...])
    q_ref[...] = (_dot(un, wq_ref[...]) * (FOX_HEAD_DIM ** -0.5)).astype(BF16)
    k_ref[...] = _dot(un, wk_ref[...]).astype(BF16)
    v_ref[...] = _dot(un, wv_ref[...]).astype(BF16)
    gate_ref[...] = jax.nn.sigmoid(_dot(un, wg_ref[...]))
    c = jax.nn.log_sigmoid(_dot(un, wf_ref[...]) + bf_ref[...])
    rows = c.shape[0]
    row = lax.broadcasted_iota(jnp.int32, c.shape, 0)
    shift = 1
    while shift < rows:
        c = c + jnp.where(row >= shift, pltpu.roll(c, shift, 0), 0.0)
        shift *= 2
    c = c + carry_ref[...]
    cum_ref[...] = c
    carry_ref[...] = c[rows - 1:rows, :]


def _in_proj(x2, gain, wu, wq, wk, wv, wg, wf, bf, bsz, seq):
    t, d = x2.shape
    tm = TM_PROJ
    nb = seq // tm
    row = lambda w: pl.BlockSpec((tm, w), lambda b, i: (b * nb + i, 0))
    ws = [wu, wq, wk, wv, wg, wf]
    return pl.pallas_call(
        _in_proj_kernel,
        grid=(bsz, nb),
        in_specs=[row(d), _full((1, d))] + [_full(w.shape) for w in ws] + [_full((1, LANES))],
        out_specs=[row(wu.shape[1]), row(wq.shape[1]), row(wk.shape[1]), row(wv.shape[1]),
                   row(wg.shape[1]), row(LANES)],
        out_shape=[jax.ShapeDtypeStruct((t, wu.shape[1]), F32),
                   jax.ShapeDtypeStruct((t, wq.shape[1]), BF16),
                   jax.ShapeDtypeStruct((t, wk.shape[1]), BF16),
                   jax.ShapeDtypeStruct((t, wv.shape[1]), BF16),
                   jax.ShapeDtypeStruct((t, wg.shape[1]), F32),
                   jax.ShapeDtypeStruct((t, LANES), F32)],
        scratch_shapes=[pltpu.VMEM((1, LANES), F32)],
        compiler_params=_cparams(("arbitrary", "arbitrary")),
        name="in_proj",
    )(x2, gain.reshape(1, d), *ws, bf)


def _gelu_tanh(x):
    return 0.5 * x * (1.0 + jnp.tanh(math.sqrt(2.0 / math.pi) * (x + 0.044715 * (x * x * x))))


def _ssm_kernel(u_ref, perm_ref, permt_ref, bre_ref, bim_ref, are_ref, aim_ref, sre_ref, sim_ref,
                cre_ref, cim_ref, d_ref, wglu_ref, out_ref, hre, him, car_re, car_im):
    @pl.when(pl.program_id(1) == 0)
    def _():
        car_re[...] = jnp.zeros_like(car_re)
        car_im[...] = jnp.zeros_like(car_im)

    u = u_ref[...]
    up = _dot(perm_ref[...], u.astype(BF16)).astype(BF16)
    nkb = bre_ref.shape[0]
    kw = bre_ref.shape[1]
    nw = bre_ref.shape[2]
    for kb in range(nkb):
        ukb = up[:, kb * kw:(kb + 1) * kw]
        hre[:, kb * nw:(kb + 1) * nw] = _dot(ukb, bre_ref[kb])
        him[:, kb * nw:(kb + 1) * nw] = _dot(ukb, bim_ref[kb])

    nstate = hre.shape[1]
    for cb in range(nstate // SSM_COLS):
        cs = slice(cb * SSM_COLS, (cb + 1) * SSM_COLS)
        a_re = jnp.broadcast_to(are_ref[:, cs], (SUBLANES, SSM_COLS))
        a_im = jnp.broadcast_to(aim_ref[:, cs], (SUBLANES, SSM_COLS))

        def local(k, h):
            h_re, h_im = h
            r = pl.ds(pl.multiple_of(k * SUBLANES, SUBLANES), SUBLANES)
            n_re = a_re * h_re - a_im * h_im + hre[r, cs]
            n_im = a_re * h_im + a_im * h_re + him[r, cs]
            hre[r, cs] = n_re
            him[r, cs] = n_im
            return n_re, n_im

        zero = jnp.zeros((SUBLANES, SSM_COLS), F32)
        e_re, e_im = lax.fori_loop(0, SSM_SEG, local, (zero, zero), unroll=4)

        s_re = sre_ref[:, cs]
        s_im = sim_ref[:, cs]
        c_re = car_re[:, cs]
        c_im = car_im[:, cs]
        rows_re, rows_im = [], []
        for j in range(SUBLANES):
            rows_re.append(c_re)
            rows_im.append(c_im)
            n_re = s_re * c_re - s_im * c_im + e_re[j:j + 1]
            n_im = s_re * c_im + s_im * c_re + e_im[j:j + 1]
            c_re, c_im = n_re, n_im
        car_re[:, cs] = c_re
        car_im[:, cs] = c_im

        def fix(k, dcar):
            d_re, d_im = dcar
            r = pl.ds(pl.multiple_of(k * SUBLANES, SUBLANES), SUBLANES)
            n_re = a_re * d_re - a_im * d_im
            n_im = a_re * d_im + a_im * d_re
            hre[r, cs] = hre[r, cs] + n_re
            him[r, cs] = him[r, cs] + n_im
            return n_re, n_im

        lax.fori_loop(0, SSM_SEG, fix,
                      (jnp.concatenate(rows_re, axis=0), jnp.concatenate(rows_im, axis=0)), unroll=4)

    ncb = cre_ref.shape[0]
    cw = cre_ref.shape[1]
    ys = []
    for kb in range(ncb):
        h_re = hre[:, kb * cw:(kb + 1) * cw].astype(BF16)
        h_im = him[:, kb * cw:(kb + 1) * cw].astype(BF16)
        ys.append(_dot(h_re, cre_ref[kb]) - _dot(h_im, cim_ref[kb]))
    yp = jnp.concatenate(ys, axis=-1)
    hi = yp.astype(BF16)
    lo = (yp - hi.astype(F32)).astype(BF16)
    y = _dot(permt_ref[...], hi) + _dot(permt_ref[...], lo)
    y = y + d_ref[...] * u
    z = _dot(_gelu_tanh(y).astype(BF16), wglu_ref[...])
    half = z.shape[-1] // 2
    out_ref[...] = z[:, :half] * jax.nn.sigmoid(z[:, half:])


def _ssm(u, perm, permt, bre, bim, a_re, a_im, s_re, s_im, cre, cim, d_skip, w_glu, bsz, seq):
    t, w = u.shape
    q = SSM_CHUNK
    nb = seq // q
    nstate = a_re.shape[1]
    dm = w_glu.shape[1] // 2
    row = lambda width: pl.BlockSpec((q, width), lambda b, i: (b * nb + i, 0))
    consts = [perm, permt, bre, bim, a_re, a_im, s_re, s_im, cre, cim, d_skip, w_glu]
    return pl.pallas_call(
        _ssm_kernel,
        grid=(bsz, nb),
        in_specs=[row(w)] + [_full(c.shape) for c in consts],
        out_specs=row(dm),
        out_shape=jax.ShapeDtypeStruct((t, dm), F32),
        scratch_shapes=[pltpu.VMEM((q, nstate), F32), pltpu.VMEM((q, nstate), F32),
                        pltpu.VMEM((1, nstate), F32), pltpu.VMEM((1, nstate), F32)],
        compiler_params=_cparams(("arbitrary", "arbitrary")),
        name="ssm",
    )(u, *consts)


def _fox_kernel(q_ref, k_ref, v_ref, cq_ref, ck_ref, o_ref):
    qi = pl.program_id(2)
    h = pl.program_id(1)
    tq = q_ref.shape[2]
    tk = FOX_TK
    q = q_ref[0, 0]
    lane = lax.broadcasted_iota(jnp.int32, cq_ref.shape, 1)
    cq = jnp.sum(jnp.where(lane == h, cq_ref[...], 0.0), axis=1, keepdims=True)

    def block(j, carry, masked):
        m, l, acc = carry
        ks = pl.ds(pl.multiple_of(j * tk, tk), tk)
        s = _dot_nt(q, k_ref[0, 0, ks, :])
        s = s + cq - ck_ref[0, pl.ds(j, 1), :]
        if masked:
            qpos = qi * tq + lax.broadcasted_iota(jnp.int32, s.shape, 0)
            kpos = j * tk + lax.broadcasted_iota(jnp.int32, s.shape, 1)
            s = jnp.where(kpos <= qpos, s, -jnp.inf)
        m_new = jnp.maximum(m, jnp.max(s, axis=1, keepdims=True))
        alpha = jnp.exp(m - m_new)
        p = jnp.exp(s - m_new)
        l = alpha * l + jnp.sum(p, axis=1, keepdims=True)
        acc = alpha * acc + _dot(p.astype(BF16), v_ref[0, 0, ks, :])
        return m_new, l, acc

    dh = q.shape[-1]
    init = (jnp.full((tq, 1), -jnp.inf, F32), jnp.zeros((tq, 1), F32), jnp.zeros((tq, dh), F32))
    ndiag = tq // tk
    nfull = qi * ndiag
    carry = lax.fori_loop(0, nfull, lambda j, c: block(j, c, False), init)
    for d in range(ndiag):
        carry = block(nfull + d, carry, True)
    m, l, acc = carry
    o_ref[0, 0] = (acc / l).astype(o_ref.dtype)


def _fox(q, k, v, cum_rows, cum_cols):
    b, h, l, dh = q.shape
    tq = FOX_TQ
    nq = l // tq
    return pl.pallas_call(
        _fox_kernel,
        grid=(b, h, nq),
        in_specs=[pl.BlockSpec((1, 1, tq, dh), lambda bi, hi, qi: (bi, hi, qi, 0)),
                  pl.BlockSpec((1, 1, l, dh), lambda bi, hi, qi: (bi, hi, 0, 0)),
                  pl.BlockSpec((1, 1, l, dh), lambda bi, hi, qi: (bi, hi, 0, 0)),
                  pl.BlockSpec((tq, LANES), lambda bi, hi, qi: (bi * nq + qi, 0)),
                  pl.BlockSpec((1, l // FOX_TK, FOX_TK), lambda bi, hi, qi: (bi * h + hi, 0, 0))],
        out_specs=pl.BlockSpec((1, 1, tq, dh), lambda bi, hi, qi: (bi, hi, qi, 0)),
        out_shape=jax.ShapeDtypeStruct((b, h, l, dh), BF16),
        compiler_params=_cparams(("arbitrary", "arbitrary", "arbitrary")),
        name="fox",
    )(q, k, v, cum_rows, cum_cols)


def _mix_mem_kernel(x_ref, oa_ref, att_ref, gate_ref, wfo_ref, wmix_ref, gq_ref, wq_ref,
                    km_ref, vm_ref, wo_ref, h_ref):
    d = x_ref.shape[-1]
    out_b = _dot(att_ref[...], wfo_ref[...])
    mix = gate_ref[:, :d] * oa_ref[...] + gate_ref[:, d:] * out_b
    h1 = x_ref[...] + _dot(mix.astype(BF16), wmix_ref[...])
    n = _rms(h1, gq_ref[...]).astype(BF16)
    qm = _dot(n, wq_ref[...])
    wm = qm.shape[-1]
    hd = wm // MEM_HEADS
    qm = (qm * (hd ** -0.5)).astype(BF16)
    outs = []
    for hh in range(MEM_HEADS):
        hs = slice(hh * hd, (hh + 1) * hd)
        s = _dot_nt(qm[:, hs], km_ref[0, :, hs])
        s = s - jnp.max(s, axis=-1, keepdims=True)
        p = jnp.exp(s)
        p = p / jnp.sum(p, axis=-1, keepdims=True)
        outs.append(_dot(p.astype(BF16), vm_ref[0, :, hs]))
    o = jnp.concatenate(outs, axis=-1).astype(BF16)
    h_ref[...] = h1 + _dot(o, wo_ref[...])


def _mix_mem(x2, out_a, att, gates, w_fox_o, w_mix, gain_q, w_q, k_m, v_m, w_o, bsz, seq):
    t, d = x2.shape
    tm = TM_MIX
    nb = seq // tm
    row = lambda w: pl.BlockSpec((tm, w), lambda b, i: (b * nb + i, 0))
    mem = pl.BlockSpec((1,) + k_m.shape[1:], lambda b, i: (b, 0, 0))
    return pl.pallas_call(
        _mix_mem_kernel,
        grid=(bsz, nb),
        in_specs=[row(d), row(d), row(att.shape[1]), row(gates.shape[1]),
                  _full(w_fox_o.shape), _full(w_mix.shape), _full((1, d)), _full(w_q.shape),
                  mem, mem, _full(w_o.shape)],
        out_specs=row(d),
        out_shape=jax.ShapeDtypeStruct((t, d), F32),
        compiler_params=_cparams(("arbitrary", "arbitrary")),
        name="mix_mem",
    )(x2, out_a, att, gates, w_fox_o, w_mix, gain_q.reshape(1, d), w_q, k_m, v_m, w_o)


def _ffn_kernel(h_ref, gf_ref, wa_ref, wb_ref, wout_ref, gfin_ref, o_ref):
    h2 = h_ref[...]
    f = _rms(h2, gf_ref[...]).astype(BF16)
    acc = jnp.zeros_like(h2)
    for c in range(wa_ref.shape[0]):
        fa = _dot(f, wa_ref[c])
        fb = _dot(f, wb_ref[c])
        g = (fa * jax.nn.sigmoid(fa) * fb).astype(BF16)
        acc = acc + _dot(g, wout_ref[c])
    o_ref[...] = _rms(h2 + acc, gfin_ref[...])


def _ffn(h2, gain_f, wa, wb, wout, gain_fin):
    t, d = h2.shape
    tm = TM_FFN
    row = pl.BlockSpec((tm, d), lambda i: (i, 0))
    return pl.pallas_call(
        _ffn_kernel,
        grid=(t // tm,),
        in_specs=[row, _full((1, d)), _full(wa.shape), _full(wb.shape), _full(wout.shape),
                  _full((1, d))],
        out_specs=row,
        out_shape=jax.ShapeDtypeStruct((t, d), F32),
        compiler_params=_cparams(("arbitrary",)),
        name="ffn",
    )(h2, gain_f.reshape(1, d), wa, wb, wout, gain_fin.reshape(1, d))


def _block_diag(blocks, per):
    g, r, c = blocks.shape
    b = blocks.reshape(g // per, per, r, c)
    eye = jnp.eye(per, dtype=blocks.dtype)
    out = b[:, :, :, None, :] * eye[None, :, None, :, None]
    return out.reshape(g // per, per * r, per * c)


def _seg_perm(q):
    seg = q // SUBLANES
    r = jnp.arange(q)
    src = (r % SUBLANES) * seg + r // SUBLANES
    return (src[:, None] == jnp.arange(q)[None, :]).astype(BF16)


def _layer(x2, mem, bsz, seq, norm_mix, w_in, b_forget, lam_re, lam_im, log_dt, b_re, b_im, c_re,
           c_im, d_skip, w_glu, w_fox_o, w_mix_out, norm_mem_q, norm_mem_kv, w_mem_q, w_mem_kv,
           w_mem_o, norm_ffn, w_ffn_in, w_ffn_out, norm_final):
    d = x2.shape[-1]
    groups, states = lam_re.shape
    ssm_w = groups * SSM_GROUP
    n_heads = b_forget.shape[0]
    fox_w = n_heads * FOX_HEAD_DIM
    o_q = ssm_w
    o_k = o_q + fox_w
    o_v = o_k + fox_w
    o_f = o_v + fox_w
    o_g = o_f + n_heads

    a_re, a_im, s_re, s_im, bb_re, bb_im = _ssm_prep(lam_re, lam_im, log_dt, b_re, b_im)
    nstate = groups * states
    per = LANES // SSM_GROUP
    to_rows = lambda a: a.reshape(1, nstate)
    bd_in = lambda bb: _block_diag(
        bb.reshape(groups, states, SSM_GROUP).transpose(0, 2, 1), per).astype(BF16)
    bd_out = lambda c: _block_diag(c.transpose(0, 2, 1), per).astype(BF16)
    perm = _seg_perm(SSM_CHUNK)

    w = w_in.astype(BF16)
    wf = jnp.zeros((d, LANES), BF16).at[:, :n_heads].set(w[:, o_f:o_g])
    bf = jnp.zeros((1, LANES), F32).at[0, :n_heads].set(b_forget)
    u, q, k, v, gates, cum = _in_proj(
        x2, norm_mix, w[:, :o_q], w[:, o_q:o_k], w[:, o_k:o_v], w[:, o_v:o_f], w[:, o_g:], wf, bf,
        bsz, seq)

    out_a = _ssm(u, perm, perm.T, bd_in(bb_re), bd_in(bb_im), to_rows(a_re), to_rows(a_im),
                 to_rows(s_re), to_rows(s_im), bd_out(c_re), bd_out(c_im),
                 d_skip.reshape(1, ssm_w), w_glu.astype(BF16), bsz, seq)

    heads = lambda a: a.reshape(bsz, seq, n_heads, FOX_HEAD_DIM).transpose(0, 2, 1, 3)
    cum_cols = cum[:, :n_heads].reshape(bsz, seq, n_heads).transpose(0, 2, 1)
    att = _fox(heads(q), heads(k), heads(v), cum, cum_cols.reshape(bsz * n_heads, seq // FOX_TK, FOX_TK))
    att = att.transpose(0, 2, 1, 3).reshape(bsz * seq, fox_w)

    k_m, v_m = _mem_kv(mem, norm_mem_kv, w_mem_kv)
    h2 = _mix_mem(x2, out_a, att, gates, w_fox_o.astype(BF16), w_mix_out.astype(BF16), norm_mem_q,
                  w_mem_q.astype(BF16), k_m, v_m, w_mem_o.astype(BF16), bsz, seq)
    hidden = w_ffn_out.shape[0]
    nchunk = 2
    hc = hidden // nchunk
    wi = w_ffn_in.astype(BF16)
    wa = wi[:, :hidden].reshape(d, nchunk, hc).transpose(1, 0, 2)
    wb = wi[:, hidden:].reshape(d, nchunk, hc).transpose(1, 0, 2)
    wout = w_ffn_out.astype(BF16).reshape(nchunk, hc, d)
    return _ffn(h2, norm_ffn, wa, wb, wout, norm_final)


def kernel(x, mem, norm_mix, w_in, b_forget, lam_re, lam_im, log_dt, b_re, b_im, c_re, c_im, d_skip,
           w_glu, w_fox_o, w_mix_out, norm_mem_q, norm_mem_kv, w_mem_q, w_mem_kv, w_mem_o, norm_ffn,
           w_ffn_in, w_ffn_out, norm_final):
    bsz, seq, d = x.shape
    assert w_in.shape[0] == 1, "single-layer block"
    out = _layer(x.reshape(bsz * seq, d), mem, bsz, seq, norm_mix[0], w_in[0], b_forget[0],
                 lam_re[0], lam_im[0], log_dt[0], b_re[0], b_im[0], c_re[0], c_im[0], d_skip[0],
                 w_glu[0], w_fox_o[0], w_mix_out[0], norm_mem_q[0], norm_mem_kv[0], w_mem_q[0],
                 w_mem_kv[0], w_mem_o[0], norm_ffn[0], w_ffn_in[0], w_ffn_out[0], norm_final)
    return out.reshape(bsz, seq, d)
```

```python
import functools
import math

import jax
import jax.numpy as jnp
from jax import lax
from jax.experimental import pallas as pl
from jax.experimental.pallas import tpu as pltpu

F32 = jnp.float32
BF16 = jnp.bfloat16

RMS_EPS = 1e-6
LOG2E = math.log2(math.e)
SSM_GROUP = 16
SSM_STATE = 64
FOX_HEAD_DIM = 64
MEM_HEADS = 4
LANES = 128
SUBLANES = 8
VMEM_LIMIT = 56 * 1024 * 1024

TM_PROJ = 512
TM_MIX = 512
TM_FFN = 256
SSM_CHUNK = 256
SSM_SEG = SSM_CHUNK // SUBLANES
SSM_COLS = 512
FOX_TQ = 512
FOX_TK = 512


def _cparams(sem):
    return pltpu.CompilerParams(dimension_semantics=sem, vmem_limit_bytes=VMEM_LIMIT)


def _rms(x, gain):
    return x * lax.rsqrt(jnp.mean(x * x, axis=-1, keepdims=True) + RMS_EPS) * gain


def _dot(a, b):
    return jnp.dot(a, b, preferred_element_type=F32)


def _dot_nt(a, b):
    return lax.dot_general(a, b, (((1,), (1,)), ((), ())), preferred_element_type=F32)


def _full(shape):
    n = len(shape)
    return pl.BlockSpec(shape, lambda *_: (0,) * n)


def _ssm_prep_kernel(lre_ref, lim_ref, ldt_ref, bre_ref, bim_ref,
                     are_ref, aim_ref, sre_ref, sim_ref, bbre_ref, bbim_ref):
    lre = lre_ref[...]
    lim = lim_ref[...]
    dt = jnp.exp(ldt_ref[...])
    zr = lre * dt
    zi = lim * dt
    mag = jnp.exp(zr)
    are = mag * jnp.cos(zi)
    aim = mag * jnp.sin(zi)
    are_ref[...] = are
    aim_ref[...] = aim
    mag_s = jnp.exp(zr * SSM_SEG)
    sre_ref[...] = mag_s * jnp.cos(zi * SSM_SEG)
    sim_ref[...] = mag_s * jnp.sin(zi * SSM_SEG)
    nr = are - 1.0
    ni = aim
    den = lre * lre + lim * lim
    fr = (nr * lre + ni * lim) / den
    fi = (ni * lre - nr * lim) / den
    bre = bre_ref[...]
    bim = bim_ref[...]
    bbre_ref[...] = fr * bre - fi * bim
    bbim_ref[...] = fr * bim + fi * bre


def _ssm_prep(lam_re, lam_im, log_dt, b_re, b_im):
    g, p = lam_re.shape
    n = b_re.shape[-1]
    c = g * p
    col = lambda a: a.reshape(c, 1)
    ldt = jnp.broadcast_to(log_dt[:, None], (g, p))
    outs = pl.pallas_call(
        _ssm_prep_kernel,
        out_shape=[jax.ShapeDtypeStruct((c, 1), F32)] * 4 + [jax.ShapeDtypeStruct((c, n), F32)] * 2,
        name="ssm_prep",
    )(col(lam_re), col(lam_im), col(ldt), b_re.reshape(c, n), b_im.reshape(c, n))
    return outs


def _mem_kv_kernel(m_ref, g_ref, w_ref, k_ref, v_ref):
    n = _rms(m_ref[0], g_ref[...]).astype(BF16)
    kv = _dot(n, w_ref[...])
    half = kv.shape[-1] // 2
    k_ref[0] = kv[:, :half].astype(BF16)
    v_ref[0] = kv[:, half:].astype(BF16)


def _mem_kv(mem, gain, w_kv):
    b, m, d = mem.shape
    w2 = w_kv.shape[-1]
    return pl.pallas_call(
        _mem_kv_kernel,
        grid=(b,),
        in_specs=[pl.BlockSpec((1, m, d), lambda i: (i, 0, 0)), _full((1, d)), _full((d, w2))],
        out_specs=[pl.BlockSpec((1, m, w2 // 2), lambda i: (i, 0, 0))] * 2,
        out_shape=[jax.ShapeDtypeStruct((b, m, w2 // 2), BF16)] * 2,
        compiler_params=_cparams(("arbitrary",)),
        name="mem_kv",
    )(mem, gain.reshape(1, d), w_kv.astype(BF16))


def _in_proj_kernel(x_ref, g_ref, wu_ref, wq_ref, wk_ref, wv_ref, wg_ref, wf_ref, bf_ref,
                    u_ref, q_ref, k_ref, v_ref, gate_ref, carry_ref):
    @pl.when(pl.program_id(1) == 0)
    def _():
        carry_ref[...] = jnp.zeros_like(carry_ref)

    un = _rms(x_ref[...], g_ref[...]).astype(BF16)
    u_ref[...] = _dot(un, wu_ref[...])
    gate_ref[...] = jax.nn.sigmoid(_dot(un, wg_ref[...]))
    c = jax.nn.log_sigmoid(_dot(un, wf_ref[...]) + bf_ref[...])
    rows = c.shape[0]
    row = lax.broadcasted_iota(jnp.int32, c.shape, 0)
    shift = 1
    while shift < rows:
        c = c + jnp.where(row >= shift, pltpu.roll(c, shift, 0), 0.0)
        shift *= 2
    c = c + carry_ref[...]
    carry_ref[...] = c[rows - 1:rows, :]

    q = _dot(un, wq_ref[...]) * (FOX_HEAD_DIM ** -0.5 * LOG2E)
    k = _dot(un, wk_ref[...])
    v = _dot(un, wv_ref[...])
    c2 = c * LOG2E
    dh = FOX_HEAD_DIM
    lane = lax.broadcasted_iota(jnp.int32, (rows, LANES - dh), 1)
    v_pad = jnp.where(lane == 0, 1.0, 0.0)
    for h in range(q.shape[1] // dh):
        ch = c2[:, h:h + 1]
        hi = ch.astype(BF16).astype(F32)
        r1 = ch - hi
        mid = r1.astype(BF16).astype(F32)
        lo = r1 - mid
        q_pad = jnp.where(lane == 0, hi, jnp.where(lane == 1, mid, jnp.where(
            lane == 2, lo, jnp.where(lane < 6, 1.0, 0.0))))
        k_pad = jnp.where(lane < 3, 1.0, jnp.where(lane == 3, -hi, jnp.where(
            lane == 4, -mid, jnp.where(lane == 5, -lo, 0.0))))
        hs = slice(h * dh, (h + 1) * dh)
        os = slice(h * LANES, (h + 1) * LANES)
        q_ref[:, os] = jnp.concatenate([q[:, hs], q_pad], axis=1).astype(BF16)
        k_ref[:, os] = jnp.concatenate([k[:, hs], k_pad], axis=1).astype(BF16)
        v_ref[:, os] = jnp.concatenate([v[:, hs], v_pad], axis=1).astype(BF16)


def _in_proj(x2, gain, wu, wq, wk, wv, wg, wf, bf, bsz, seq):
    t, d = x2.shape
    tm = TM_PROJ
    nb = seq // tm
    row = lambda w: pl.BlockSpec((tm, w), lambda b, i: (b * nb + i, 0))
    ws = [wu, wq, wk, wv, wg, wf]
    wh = wq.shape[1] // FOX_HEAD_DIM * LANES
    return pl.pallas_call(
        _in_proj_kernel,
        grid=(bsz, nb),
        in_specs=[row(d), _full((1, d))] + [_full(w.shape) for w in ws] + [_full((1, LANES))],
        out_specs=[row(wu.shape[1]), row(wh), row(wh), row(wh), row(wg.shape[1])],
        out_shape=[jax.ShapeDtypeStruct((t, wu.shape[1]), F32),
                   jax.ShapeDtypeStruct((t, wh), BF16),
                   jax.ShapeDtypeStruct((t, wh), BF16),
                   jax.ShapeDtypeStruct((t, wh), BF16),
                   jax.ShapeDtypeStruct((t, wg.shape[1]), F32)],
        scratch_shapes=[pltpu.VMEM((1, LANES), F32)],
        compiler_params=_cparams(("arbitrary", "arbitrary")),
        name="in_proj",
    )(x2, gain.reshape(1, d), *ws, bf)


def _gelu_tanh(x):
    return 0.5 * x * (1.0 + jnp.tanh(math.sqrt(2.0 / math.pi) * (x + 0.044715 * (x * x * x))))


def _ssm_kernel(u_ref, perm_ref, permt_ref, bre_ref, bim_ref, are_ref, aim_ref, sre_ref, sim_ref,
                cre_ref, cim_ref, d_ref, wglu_ref, out_ref, hre, him, car_re, car_im):
    @pl.when(pl.program_id(1) == 0)
    def _():
        car_re[...] = jnp.zeros_like(car_re)
        car_im[...] = jnp.zeros_like(car_im)

    u = u_ref[...]
    up = _dot(perm_ref[...], u.astype(BF16)).astype(BF16)
    nkb = bre_ref.shape[0]
    kw = bre_ref.shape[1]
    nw = bre_ref.shape[2]
    for kb in range(nkb):
        ukb = up[:, kb * kw:(kb + 1) * kw]
        hre[:, kb * nw:(kb + 1) * nw] = _dot(ukb, bre_ref[kb])
        him[:, kb * nw:(kb + 1) * nw] = _dot(ukb, bim_ref[kb])

    nstate = hre.shape[1]
    for cb in range(nstate // SSM_COLS):
        cs = slice(cb * SSM_COLS, (cb + 1) * SSM_COLS)
        a_re = jnp.broadcast_to(are_ref[:, cs], (SUBLANES, SSM_COLS))
        a_im = jnp.broadcast_to(aim_ref[:, cs], (SUBLANES, SSM_COLS))

        def local(k, h):
            h_re, h_im = h
            r = pl.ds(pl.multiple_of(k * SUBLANES, SUBLANES), SUBLANES)
            n_re = a_re * h_re - a_im * h_im + hre[r, cs]
            n_im = a_re * h_im + a_im * h_re + him[r, cs]
            hre[r, cs] = n_re
            him[r, cs] = n_im
            return n_re, n_im

        zero = jnp.zeros((SUBLANES, SSM_COLS), F32)
        e_re, e_im = lax.fori_loop(0, SSM_SEG, local, (zero, zero), unroll=4)

        s_re = sre_ref[:, cs]
        s_im = sim_ref[:, cs]
        c_re = car_re[:, cs]
        c_im = car_im[:, cs]
        rows_re, rows_im = [], []
        for j in range(SUBLANES):
            rows_re.append(c_re)
            rows_im.append(c_im)
            n_re = s_re * c_re - s_im * c_im + e_re[j:j + 1]
            n_im = s_re * c_im + s_im * c_re + e_im[j:j + 1]
            c_re, c_im = n_re, n_im
        car_re[:, cs] = c_re
        car_im[:, cs] = c_im

        def fix(k, dcar):
            d_re, d_im = dcar
            r = pl.ds(pl.multiple_of(k * SUBLANES, SUBLANES), SUBLANES)
            n_re = a_re * d_re - a_im * d_im
            n_im = a_re * d_im + a_im * d_re
            hre[r, cs] = hre[r, cs] + n_re
            him[r, cs] = him[r, cs] + n_im
            return n_re, n_im

        lax.fori_loop(0, SSM_SEG, fix,
                      (jnp.concatenate(rows_re, axis=0), jnp.concatenate(rows_im, axis=0)), unroll=4)

    ncb = cre_ref.shape[0]
    cw = cre_ref.shape[1]
    ys = []
    for kb in range(ncb):
        h_re = hre[:, kb * cw:(kb + 1) * cw].astype(BF16)
        h_im = him[:, kb * cw:(kb + 1) * cw].astype(BF16)
        ys.append(_dot(h_re, cre_ref[kb]) - _dot(h_im, cim_ref[kb]))
    yp = jnp.concatenate(ys, axis=-1)
    hi = yp.astype(BF16)
    lo = (yp - hi.astype(F32)).astype(BF16)
    y = _dot(permt_ref[...], hi) + _dot(permt_ref[...], lo)
    y = y + d_ref[...] * u
    z = _dot(_gelu_tanh(y).astype(BF16), wglu_ref[...])
    half = z.shape[-1] // 2
    out_ref[...] = z[:, :half] * jax.nn.sigmoid(z[:, half:])


def _ssm(u, perm, permt, bre, bim, a_re, a_im, s_re, s_im, cre, cim, d_skip, w_glu, bsz, seq):
    t, w = u.shape
    q = SSM_CHUNK
    nb = seq // q
    nstate = a_re.shape[1]
    dm = w_glu.shape[1] // 2
    row = lambda width: pl.BlockSpec((q, width), lambda b, i: (b * nb + i, 0))
    consts = [perm, permt, bre, bim, a_re, a_im, s_re, s_im, cre, cim, d_skip, w_glu]
    return pl.pallas_call(
        _ssm_kernel,
        grid=(bsz, nb),
        in_specs=[row(w)] + [_full(c.shape) for c in consts],
        out_specs=row(dm),
        out_shape=jax.ShapeDtypeStruct((t, dm), F32),
        scratch_shapes=[pltpu.VMEM((q, nstate), F32), pltpu.VMEM((q, nstate), F32),
                        pltpu.VMEM((1, nstate), F32), pltpu.VMEM((1, nstate), F32)],
        compiler_params=_cparams(("arbitrary", "arbitrary")),
        name="ssm",
    )(u, *consts)


def _fox_kernel(q_ref, k_ref, v_ref, o_ref):
    seq = q_ref.shape[0]
    tq = FOX_TQ
    tk = FOX_TK
    dh = FOX_HEAD_DIM

    def block(q, j, carry, masked):
        m, acc = carry
        ks = pl.ds(pl.multiple_of(j * tk, tk), tk)
        s = _dot_nt(q, k_ref[ks, :])
        if masked:
            r = lax.broadcasted_iota(jnp.int32, s.shape, 0)
            c = lax.broadcasted_iota(jnp.int32, s.shape, 1)
            s = jnp.where(c <= r, s, -jnp.inf)
        m_new = jnp.maximum(m, jnp.max(s, axis=1, keepdims=True))
        p = jnp.exp2(s - m_new)
        acc = jnp.exp2(m - m_new) * acc + _dot(p.astype(BF16), v_ref[ks, :])
        return m_new, acc

    def qblock(qi, _):
        qs = pl.ds(pl.multiple_of(qi * tq, tq), tq)
        q = q_ref[qs, :]
        init = (jnp.full((tq, 1), -jnp.inf, F32), jnp.zeros((tq, LANES), F32))
        carry = lax.fori_loop(0, qi, lambda j, c: block(q, j, c, False), init)
        m, acc = block(q, qi, carry, True)
        o_ref[qs, :] = (acc / acc[:, dh:dh + 1]).astype(o_ref.dtype)
        return 0

    lax.fori_loop(0, seq // tq, qblock, 0)


def _fox(q, k, v, bsz, seq):
    t, wh = q.shape
    assert FOX_TQ == FOX_TK
    blk = pl.BlockSpec((seq, LANES), lambda bi, hi: (bi, hi))
    return pl.pallas_call(
        _fox_kernel,
        grid=(bsz, wh // LANES),
        in_specs=[blk, blk, blk],
        out_specs=blk,
        out_shape=jax.ShapeDtypeStruct((t, wh), BF16),
        compiler_params=_cparams(("arbitrary", "arbitrary")),
        name="fox",
    )(q, k, v)


def _mix_mem_kernel(x_ref, oa_ref, att_ref, gate_ref, wfo_ref, wmix_ref, gq_ref, wq_ref,
                    km_ref, vm_ref, wo_ref, h_ref):
    d = x_ref.shape[-1]
    out_b = _dot(att_ref[...], wfo_ref[...])
    mix = gate_ref[:, :d] * oa_ref[...] + gate_ref[:, d:] * out_b
    h1 = x_ref[...] + _dot(mix.astype(BF16), wmix_ref[...])
    n = _rms(h1, gq_ref[...]).astype(BF16)
    qm = _dot(n, wq_ref[...])
    wm = qm.shape[-1]
    hd = wm // MEM_HEADS
    qm = (qm * (hd ** -0.5)).astype(BF16)
    outs = []
    for hh in range(MEM_HEADS):
        hs = slice(hh * hd, (hh + 1) * hd)
        s = _dot_nt(qm[:, hs], km_ref[0, :, hs])
        s = s - jnp.max(s, axis=-1, keepdims=True)
        p = jnp.exp(s)
        p = p / jnp.sum(p, axis=-1, keepdims=True)
        outs.append(_dot(p.astype(BF16), vm_ref[0, :, hs]))
    o = jnp.concatenate(outs, axis=-1).astype(BF16)
    h_ref[...] = h1 + _dot(o, wo_ref[...])


def _mix_mem(x2, out_a, att, gates, w_fox_o, w_mix, gain_q, w_q, k_m, v_m, w_o, bsz, seq):
    t, d = x2.shape
    tm = TM_MIX
    nb = seq // tm
    row = lambda w: pl.BlockSpec((tm, w), lambda b, i: (b * nb + i, 0))
    mem = pl.BlockSpec((1,) + k_m.shape[1:], lambda b, i: (b, 0, 0))
    return pl.pallas_call(
        _mix_mem_kernel,
        grid=(bsz, nb),
        in_specs=[row(d), row(d), row(att.shape[1]), row(gates.shape[1]),
                  _full(w_fox_o.shape), _full(w_mix.shape), _full((1, d)), _full(w_q.shape),
                  mem, mem, _full(w_o.shape)],
        out_specs=row(d),
        out_shape=jax.ShapeDtypeStruct((t, d), F32),
        compiler_params=_cparams(("arbitrary", "arbitrary")),
        name="mix_mem",
    )(x2, out_a, att, gates, w_fox_o, w_mix, gain_q.reshape(1, d), w_q, k_m, v_m, w_o)


def _ffn_kernel(h_ref, gf_ref, wa_ref, wb_ref, wout_ref, gfin_ref, o_ref):
    h2 = h_ref[...]
    f = _rms(h2, gf_ref[...]).astype(BF16)
    acc = jnp.zeros_like(h2)
    for c in range(wa_ref.shape[0]):
        fa = _dot(f, wa_ref[c])
        fb = _dot(f, wb_ref[c])
        g = (fa * jax.nn.sigmoid(fa) * fb).astype(BF16)
        acc = acc + _dot(g, wout_ref[c])
    o_ref[...] = _rms(h2 + acc, gfin_ref[...])


def _ffn(h2, gain_f, wa, wb, wout, gain_fin):
    t, d = h2.shape
    tm = TM_FFN
    row = pl.BlockSpec((tm, d), lambda i: (i, 0))
    return pl.pallas_call(
        _ffn_kernel,
        grid=(t // tm,),
        in_specs=[row, _full((1, d)), _full(wa.shape), _full(wb.shape), _full(wout.shape),
                  _full((1, d))],
        out_specs=row,
        out_shape=jax.ShapeDtypeStruct((t, d), F32),
        compiler_params=_cparams(("arbitrary",)),
        name="ffn",
    )(h2, gain_f.reshape(1, d), wa, wb, wout, gain_fin.reshape(1, d))


def _block_diag(blocks, per):
    g, r, c = blocks.shape
    b = blocks.reshape(g // per, per, r, c)
    eye = jnp.eye(per, dtype=blocks.dtype)
    out = b[:, :, :, None, :] * eye[None, :, None, :, None]
    return out.reshape(g // per, per * r, per * c)


def _seg_perm(q):
    seg = q // SUBLANES
    r = jnp.arange(q)
    src = (r % SUBLANES) * seg + r // SUBLANES
    return (src[:, None] == jnp.arange(q)[None, :]).astype(BF16)


def _layer(x2, mem, bsz, seq, norm_mix, w_in, b_forget, lam_re, lam_im, log_dt, b_re, b_im, c_re,
           c_im, d_skip, w_glu, w_fox_o, w_mix_out, norm_mem_q, norm_mem_kv, w_mem_q, w_mem_kv,
           w_mem_o, norm_ffn, w_ffn_in, w_ffn_out, norm_final):
    d = x2.shape[-1]
    groups, states = lam_re.shape
    ssm_w = groups * SSM_GROUP
    n_heads = b_forget.shape[0]
    fox_w = n_heads * FOX_HEAD_DIM
    o_q = ssm_w
    o_k = o_q + fox_w
    o_v = o_k + fox_w
    o_f = o_v + fox_w
    o_g = o_f + n_heads

    a_re, a_im, s_re, s_im, bb_re, bb_im = _ssm_prep(lam_re, lam_im, log_dt, b_re, b_im)
    nstate = groups * states
    per = LANES // SSM_GROUP
    to_rows = lambda a: a.reshape(1, nstate)
    bd_in = lambda bb: _block_diag(
        bb.reshape(groups, states, SSM_GROUP).transpose(0, 2, 1), per).astype(BF16)
    bd_out = lambda c: _block_diag(c.transpose(0, 2, 1), per).astype(BF16)
    perm = _seg_perm(SSM_CHUNK)

    w = w_in.astype(BF16)
    wf = jnp.zeros((d, LANES), BF16).at[:, :n_heads].set(w[:, o_f:o_g])
    bf = jnp.zeros((1, LANES), F32).at[0, :n_heads].set(b_forget)
    u, q, k, v, gates = _in_proj(
        x2, norm_mix, w[:, :o_q], w[:, o_q:o_k], w[:, o_k:o_v], w[:, o_v:o_f], w[:, o_g:], wf, bf,
        bsz, seq)

    out_a = _ssm(u, perm, perm.T, bd_in(bb_re), bd_in(bb_im), to_rows(a_re), to_rows(a_im),
                 to_rows(s_re), to_rows(s_im), bd_out(c_re), bd_out(c_im),
                 d_skip.reshape(1, ssm_w), w_glu.astype(BF16), bsz, seq)

    att = _fox(q, k, v, bsz, seq)
    w_fo = jnp.pad(w_fox_o.reshape(n_heads, FOX_HEAD_DIM, d),
                   ((0, 0), (0, LANES - FOX_HEAD_DIM), (0, 0))).reshape(n_heads * LANES, d)

    k_m, v_m = _mem_kv(mem, norm_mem_kv, w_mem_kv)
    h2 = _mix_mem(x2, out_a, att, gates, w_fo.astype(BF16), w_mix_out.astype(BF16), norm_mem_q,
                  w_mem_q.astype(BF16), k_m, v_m, w_mem_o.astype(BF16), bsz, seq)
    hidden = w_ffn_out.shape[0]
    nchunk = 2
    hc = hidden // nchunk
    wi = w_ffn_in.astype(BF16)
    wa = wi[:, :hidden].reshape(d, nchunk, hc).transpose(1, 0, 2)
    wb = wi[:, hidden:].reshape(d, nchunk, hc).transpose(1, 0, 2)
    wout = w_ffn_out.astype(BF16).reshape(nchunk, hc, d)
    return _ffn(h2, norm_ffn, wa, wb, wout, norm_final)


def kernel(x, mem, norm_mix, w_in, b_forget, lam_re, lam_im, log_dt, b_re, b_im, c_re, c_im, d_skip,
           w_glu, w_fox_o, w_mix_out, norm_mem_q, norm_mem_kv, w_mem_q, w_mem_kv, w_mem_o, norm_ffn,
           w_ffn_in, w_ffn_out, norm_final):
    bsz, seq, d = x.shape
    assert w_in.shape[0] == 1, "single-layer block"
    out = _layer(x.reshape(bsz * seq, d), mem, bsz, seq, norm_mix[0], w_in[0], b_forget[0],
                 lam_re[0], lam_im[0], log_dt[0], b_re[0], b_im[0], c_re[0], c_im[0], d_skip[0],
                 w_glu[0], w_fox_o[0], w_mix_out[0], norm_mem_q[0], norm_mem_kv[0], w_mem_q[0],
                 w_mem_kv[0], w_mem_o[0], norm_ffn[0], w_ffn_in[0], w_ffn_out[0], norm_final)
    return out.reshape(bsz, seq, d)
```

```python
import functools
import math

import jax
import jax.numpy as jnp
from jax import lax
from jax.experimental import pallas as pl
from jax.experimental.pallas import tpu as pltpu

F32 = jnp.float32
BF16 = jnp.bfloat16

RMS_EPS = 1e-6
LOG2E = math.log2(math.e)
SSM_GROUP = 16
SSM_STATE = 64
FOX_HEAD_DIM = 64
MEM_HEADS = 4
LANES = 128
SUBLANES = 8
VMEM_LIMIT = 56 * 1024 * 1024

TM_PROJ = 512
TM_MIX = 512
TM_FFN = 256
SSM_CHUNK = 256
SSM_SEG = SSM_CHUNK // SUBLANES
SSM_COLS = 512
FOX_TQ = 1024
FOX_TK = 512
FOX_HEADS_PER_STEP = 2


def _cparams(sem):
    return pltpu.CompilerParams(dimension_semantics=sem, vmem_limit_bytes=VMEM_LIMIT)


def _rms(x, gain):
    return x * lax.rsqrt(jnp.mean(x * x, axis=-1, keepdims=True) + RMS_EPS) * gain


def _dot(a, b):
    return jnp.dot(a, b, preferred_element_type=F32)


def _dot_nt(a, b):
    return lax.dot_general(a, b, (((1,), (1,)), ((), ())), preferred_element_type=F32)


def _full(shape):
    n = len(shape)
    return pl.BlockSpec(shape, lambda *_: (0,) * n)


def _ssm_prep_kernel(lre_ref, lim_ref, ldt_ref, bre_ref, bim_ref,
                     are_ref, aim_ref, sre_ref, sim_ref, bbre_ref, bbim_ref):
    lre = lre_ref[...]
    lim = lim_ref[...]
    dt = jnp.exp(ldt_ref[...])
    zr = lre * dt
    zi = lim * dt
    mag = jnp.exp(zr)
    are = mag * jnp.cos(zi)
    aim = mag * jnp.sin(zi)
    are_ref[...] = are
    aim_ref[...] = aim
    mag_s = jnp.exp(zr * SSM_SEG)
    sre_ref[...] = mag_s * jnp.cos(zi * SSM_SEG)
    sim_ref[...] = mag_s * jnp.sin(zi * SSM_SEG)
    nr = are - 1.0
    ni = aim
    den = lre * lre + lim * lim
    fr = (nr * lre + ni * lim) / den
    fi = (ni * lre - nr * lim) / den
    bre = bre_ref[...]
    bim = bim_ref[...]
    bbre_ref[...] = fr * bre - fi * bim
    bbim_ref[...] = fr * bim + fi * bre


def _ssm_prep(lam_re, lam_im, log_dt, b_re, b_im):
    g, p = lam_re.shape
    n = b_re.shape[-1]
    c = g * p
    col = lambda a: a.reshape(c, 1)
    ldt = jnp.broadcast_to(log_dt[:, None], (g, p))
    outs = pl.pallas_call(
        _ssm_prep_kernel,
        out_shape=[jax.ShapeDtypeStruct((c, 1), F32)] * 4 + [jax.ShapeDtypeStruct((c, n), F32)] * 2,
        name="ssm_prep",
    )(col(lam_re), col(lam_im), col(ldt), b_re.reshape(c, n), b_im.reshape(c, n))
    return outs


def _mem_kv_kernel(m_ref, g_ref, w_ref, k_ref, v_ref):
    n = _rms(m_ref[0], g_ref[...]).astype(BF16)
    kv = _dot(n, w_ref[...])
    half = kv.shape[-1] // 2
    k_ref[0] = kv[:, :half].astype(BF16)
    v_ref[0] = kv[:, half:].astype(BF16)


def _mem_kv(mem, gain, w_kv):
    b, m, d = mem.shape
    w2 = w_kv.shape[-1]
    return pl.pallas_call(
        _mem_kv_kernel,
        grid=(b,),
        in_specs=[pl.BlockSpec((1, m, d), lambda i: (i, 0, 0)), _full((1, d)), _full((d, w2))],
        out_specs=[pl.BlockSpec((1, m, w2 // 2), lambda i: (i, 0, 0))] * 2,
        out_shape=[jax.ShapeDtypeStruct((b, m, w2 // 2), BF16)] * 2,
        compiler_params=_cparams(("arbitrary",)),
        name="mem_kv",
    )(mem, gain.reshape(1, d), w_kv.astype(BF16))


def _in_proj_kernel(x_ref, g_ref, wu_ref, wq_ref, wk_ref, wv_ref, wg_ref, wf_ref, bf_ref,
                    u_ref, q_ref, k_ref, v_ref, gate_ref, carry_ref):
    @pl.when(pl.program_id(1) == 0)
    def _():
        carry_ref[...] = jnp.zeros_like(carry_ref)

    un = _rms(x_ref[...], g_ref[...]).astype(BF16)
    u_ref[...] = _dot(un, wu_ref[...])
    gate_ref[...] = jax.nn.sigmoid(_dot(un, wg_ref[...]))
    c = jax.nn.log_sigmoid(_dot(un, wf_ref[...]) + bf_ref[...])
    rows = c.shape[0]
    row = lax.broadcasted_iota(jnp.int32, c.shape, 0)
    shift = 1
    while shift < rows:
        c = c + jnp.where(row >= shift, pltpu.roll(c, shift, 0), 0.0)
        shift *= 2
    c = c + carry_ref[...]
    carry_ref[...] = c[rows - 1:rows, :]

    q = _dot(un, wq_ref[...]) * (FOX_HEAD_DIM ** -0.5 * LOG2E)
    k = _dot(un, wk_ref[...])
    v = _dot(un, wv_ref[...])
    c2 = c * LOG2E
    dh = FOX_HEAD_DIM
    lane = lax.broadcasted_iota(jnp.int32, (rows, LANES - dh), 1)
    v_pad = jnp.where(lane == 0, 1.0, 0.0)
    for h in range(q.shape[1] // dh):
        ch = c2[:, h:h + 1]
        hi = ch.astype(BF16).astype(F32)
        r1 = ch - hi
        mid = r1.astype(BF16).astype(F32)
        lo = r1 - mid
        q_pad = jnp.where(lane == 0, hi, jnp.where(lane == 1, mid, jnp.where(
            lane == 2, lo, jnp.where(lane < 6, 1.0, 0.0))))
        k_pad = jnp.where(lane < 3, 1.0, jnp.where(lane == 3, -hi, jnp.where(
            lane == 4, -mid, jnp.where(lane == 5, -lo, 0.0))))
        hs = slice(h * dh, (h + 1) * dh)
        os = slice(h * LANES, (h + 1) * LANES)
        q_ref[:, os] = jnp.concatenate([q[:, hs], q_pad], axis=1).astype(BF16)
        k_ref[:, os] = jnp.concatenate([k[:, hs], k_pad], axis=1).astype(BF16)
        v_ref[:, os] = jnp.concatenate([v[:, hs], v_pad], axis=1).astype(BF16)


def _in_proj(x2, gain, wu, wq, wk, wv, wg, wf, bf, bsz, seq):
    t, d = x2.shape
    tm = TM_PROJ
    nb = seq // tm
    row = lambda w: pl.BlockSpec((tm, w), lambda b, i: (b * nb + i, 0))
    ws = [wu, wq, wk, wv, wg, wf]
    wh = wq.shape[1] // FOX_HEAD_DIM * LANES
    return pl.pallas_call(
        _in_proj_kernel,
        grid=(bsz, nb),
        in_specs=[row(d), _full((1, d))] + [_full(w.shape) for w in ws] + [_full((1, LANES))],
        out_specs=[row(wu.shape[1]), row(wh), row(wh), row(wh), row(wg.shape[1])],
        out_shape=[jax.ShapeDtypeStruct((t, wu.shape[1]), F32),
                   jax.ShapeDtypeStruct((t, wh), BF16),
                   jax.ShapeDtypeStruct((t, wh), BF16),
                   jax.ShapeDtypeStruct((t, wh), BF16),
                   jax.ShapeDtypeStruct((t, wg.shape[1]), F32)],
        scratch_shapes=[pltpu.VMEM((1, LANES), F32)],
        compiler_params=_cparams(("arbitrary", "arbitrary")),
        name="in_proj",
    )(x2, gain.reshape(1, d), *ws, bf)


def _gelu_tanh(x):
    return 0.5 * x * (1.0 + jnp.tanh(math.sqrt(2.0 / math.pi) * (x + 0.044715 * (x * x * x))))


def _ssm_kernel(u_ref, perm_ref, permt_ref, bre_ref, bim_ref, are_ref, aim_ref, sre_ref, sim_ref,
                cre_ref, cim_ref, d_ref, wglu_ref, out_ref, hre, him, car_re, car_im):
    @pl.when(pl.program_id(1) == 0)
    def _():
        car_re[...] = jnp.zeros_like(car_re)
        car_im[...] = jnp.zeros_like(car_im)

    u = u_ref[...]
    up = _dot(perm_ref[...], u.astype(BF16)).astype(BF16)
    nkb = bre_ref.shape[0]
    kw = bre_ref.shape[1]
    nw = bre_ref.shape[2]
    for kb in range(nkb):
        ukb = up[:, kb * kw:(kb + 1) * kw]
        hre[:, kb * nw:(kb + 1) * nw] = _dot(ukb, bre_ref[kb])
        him[:, kb * nw:(kb + 1) * nw] = _dot(ukb, bim_ref[kb])

    nstate = hre.shape[1]
    for cb in range(nstate // SSM_COLS):
        cs = slice(cb * SSM_COLS, (cb + 1) * SSM_COLS)
        a_re = jnp.broadcast_to(are_ref[:, cs], (SUBLANES, SSM_COLS))
        a_im = jnp.broadcast_to(aim_ref[:, cs], (SUBLANES, SSM_COLS))

        def local(k, h):
            h_re, h_im = h
            r = pl.ds(pl.multiple_of(k * SUBLANES, SUBLANES), SUBLANES)
            n_re = a_re * h_re - a_im * h_im + hre[r, cs]
            n_im = a_re * h_im + a_im * h_re + him[r, cs]
            hre[r, cs] = n_re
            him[r, cs] = n_im
            return n_re, n_im

        zero = jnp.zeros((SUBLANES, SSM_COLS), F32)
        e_re, e_im = lax.fori_loop(0, SSM_SEG, local, (zero, zero), unroll=4)

        s_re = sre_ref[:, cs]
        s_im = sim_ref[:, cs]
        c_re = car_re[:, cs]
        c_im = car_im[:, cs]
        rows_re, rows_im = [], []
        for j in range(SUBLANES):
            rows_re.append(c_re)
            rows_im.append(c_im)
            n_re = s_re * c_re - s_im * c_im + e_re[j:j + 1]
            n_im = s_re * c_im + s_im * c_re + e_im[j:j + 1]
            c_re, c_im = n_re, n_im
        car_re[:, cs] = c_re
        car_im[:, cs] = c_im

        def fix(k, dcar):
            d_re, d_im = dcar
            r = pl.ds(pl.multiple_of(k * SUBLANES, SUBLANES), SUBLANES)
            n_re = a_re * d_re - a_im * d_im
            n_im = a_re * d_im + a_im * d_re
            hre[r, cs] = hre[r, cs] + n_re
            him[r, cs] = him[r, cs] + n_im
            return n_re, n_im

        lax.fori_loop(0, SSM_SEG, fix,
                      (jnp.concatenate(rows_re, axis=0), jnp.concatenate(rows_im, axis=0)), unroll=4)

    ncb = cre_ref.shape[0]
    cw = cre_ref.shape[1]
    ys = []
    for kb in range(ncb):
        h_re = hre[:, kb * cw:(kb + 1) * cw].astype(BF16)
        h_im = him[:, kb * cw:(kb + 1) * cw].astype(BF16)
        ys.append(_dot(h_re, cre_ref[kb]) - _dot(h_im, cim_ref[kb]))
    yp = jnp.concatenate(ys, axis=-1)
    hi = yp.astype(BF16)
    lo = (yp - hi.astype(F32)).astype(BF16)
    y = _dot(permt_ref[...], hi) + _dot(permt_ref[...], lo)
    y = y + d_ref[...] * u
    z = _dot(_gelu_tanh(y).astype(BF16), wglu_ref[...])
    half = z.shape[-1] // 2
    out_ref[...] = z[:, :half] * jax.nn.sigmoid(z[:, half:])


def _ssm(u, perm, permt, bre, bim, a_re, a_im, s_re, s_im, cre, cim, d_skip, w_glu, bsz, seq):
    t, w = u.shape
    q = SSM_CHUNK
    nb = seq // q
    nstate = a_re.shape[1]
    dm = w_glu.shape[1] // 2
    row = lambda width: pl.BlockSpec((q, width), lambda b, i: (b * nb + i, 0))
    consts = [perm, permt, bre, bim, a_re, a_im, s_re, s_im, cre, cim, d_skip, w_glu]
    return pl.pallas_call(
        _ssm_kernel,
        grid=(bsz, nb),
        in_specs=[row(w)] + [_full(c.shape) for c in consts],
        out_specs=row(dm),
        out_shape=jax.ShapeDtypeStruct((t, dm), F32),
        scratch_shapes=[pltpu.VMEM((q, nstate), F32), pltpu.VMEM((q, nstate), F32),
                        pltpu.VMEM((1, nstate), F32), pltpu.VMEM((1, nstate), F32)],
        compiler_params=_cparams(("arbitrary", "arbitrary")),
        name="ssm",
    )(u, *consts)


def _fox_kernel(q_ref, k_ref, v_ref, o_ref, m_scr, acc_scr):
    seq = q_ref.shape[0]
    tq = FOX_TQ
    tk = FOX_TK
    dh = FOX_HEAD_DIM
    heads = [slice(h * LANES, (h + 1) * LANES) for h in range(q_ref.shape[1] // LANES)]

    def block(q, j, rows, hs, masked):
        ks = pl.ds(pl.multiple_of(j * tk, tk), tk)
        s = _dot_nt(q, k_ref[ks, hs])
        if masked:
            r = lax.broadcasted_iota(jnp.int32, s.shape, 0)
            c = lax.broadcasted_iota(jnp.int32, s.shape, 1)
            s = jnp.where(c <= r, s, -jnp.inf)
        m_prev = m_scr[rows, hs]
        m_new = jnp.maximum(m_prev, jnp.max(s, axis=1, keepdims=True))
        p = jnp.exp2(s - pltpu.repeat(m_new, tk // LANES, axis=1))
        acc_scr[rows, hs] = (jnp.exp2(m_prev - m_new) * acc_scr[rows, hs]
                             + _dot(p.astype(BF16), v_ref[ks, hs]))
        m_scr[rows, hs] = m_new

    nsub = tq // tk
    all_rows = slice(0, tq)

    def qblock(qi, _):
        qs = pl.ds(pl.multiple_of(qi * tq, tq), tq)
        qh = [q_ref[qs, hs] for hs in heads]
        m_scr[...] = jnp.full(m_scr.shape, -jnp.inf, F32)
        acc_scr[...] = jnp.zeros(acc_scr.shape, F32)

        def full(j, _):
            for q, hs in zip(qh, heads):
                block(q, j, all_rows, hs, False)
            return 0

        lax.fori_loop(0, qi * nsub, full, 0)
        for q, hs in zip(qh, heads):
            for r in range(nsub):
                rows = slice(r * tk, (r + 1) * tk)
                for jj in range(r):
                    block(q[rows], qi * nsub + jj, rows, hs, False)
                block(q[rows], qi * nsub + r, rows, hs, True)
                acc = acc_scr[rows, hs]
                o_ref[pl.ds(pl.multiple_of(qi * tq + r * tk, tk), tk), hs] = (
                    acc / acc[:, dh:dh + 1]).astype(o_ref.dtype)
        return 0

    lax.fori_loop(0, seq // tq, qblock, 0)


def _fox(q, k, v, bsz, seq):
    t, wh = q.shape
    assert FOX_TQ % FOX_TK == 0 and seq % FOX_TQ == 0
    wblk = FOX_HEADS_PER_STEP * LANES
    blk = pl.BlockSpec((seq, wblk), lambda bi, hi: (bi, hi))
    return pl.pallas_call(
        _fox_kernel,
        grid=(bsz, wh // wblk),
        in_specs=[blk, blk, blk],
        out_specs=blk,
        out_shape=jax.ShapeDtypeStruct((t, wh), BF16),
        scratch_shapes=[pltpu.VMEM((FOX_TQ, wblk), F32), pltpu.VMEM((FOX_TQ, wblk), F32)],
        compiler_params=_cparams(("arbitrary", "arbitrary")),
        name="fox",
    )(q, k, v)


def _mix_mem_kernel(x_ref, oa_ref, att_ref, gate_ref, wfo_ref, wmix_ref, gq_ref, wq_ref,
                    km_ref, vm_ref, wo_ref, h_ref):
    d = x_ref.shape[-1]
    out_b = _dot(att_ref[...], wfo_ref[...])
    mix = gate_ref[:, :d] * oa_ref[...] + gate_ref[:, d:] * out_b
    h1 = x_ref[...] + _dot(mix.astype(BF16), wmix_ref[...])
    n = _rms(h1, gq_ref[...]).astype(BF16)
    qm = _dot(n, wq_ref[...])
    wm = qm.shape[-1]
    hd = wm // MEM_HEADS
    qm = (qm * (hd ** -0.5)).astype(BF16)
    outs = []
    for hh in range(MEM_HEADS):
        hs = slice(hh * hd, (hh + 1) * hd)
        s = _dot_nt(qm[:, hs], km_ref[0, :, hs])
        s = s - jnp.max(s, axis=-1, keepdims=True)
        p = jnp.exp(s)
        p = p / jnp.sum(p, axis=-1, keepdims=True)
        outs.append(_dot(p.astype(BF16), vm_ref[0, :, hs]))
    o = jnp.concatenate(outs, axis=-1).astype(BF16)
    h_ref[...] = h1 + _dot(o, wo_ref[...])


def _mix_mem(x2, out_a, att, gates, w_fox_o, w_mix, gain_q, w_q, k_m, v_m, w_o, bsz, seq):
    t, d = x2.shape
    tm = TM_MIX
    nb = seq // tm
    row = lambda w: pl.BlockSpec((tm, w), lambda b, i: (b * nb + i, 0))
    mem = pl.BlockSpec((1,) + k_m.shape[1:], lambda b, i: (b, 0, 0))
    return pl.pallas_call(
        _mix_mem_kernel,
        grid=(bsz, nb),
        in_specs=[row(d), row(d), row(att.shape[1]), row(gates.shape[1]),
                  _full(w_fox_o.shape), _full(w_mix.shape), _full((1, d)), _full(w_q.shape),
                  mem, mem, _full(w_o.shape)],
        out_specs=row(d),
        out_shape=jax.ShapeDtypeStruct((t, d), F32),
        compiler_params=_cparams(("arbitrary", "arbitrary")),
        name="mix_mem",
    )(x2, out_a, att, gates, w_fox_o, w_mix, gain_q.reshape(1, d), w_q, k_m, v_m, w_o)


def _ffn_kernel(h_ref, gf_ref, wa_ref, wb_ref, wout_ref, gfin_ref, o_ref):
    h2 = h_ref[...]
    f = _rms(h2, gf_ref[...]).astype(BF16)
    acc = jnp.zeros_like(h2)
    for c in range(wa_ref.shape[0]):
        fa = _dot(f, wa_ref[c])
        fb = _dot(f, wb_ref[c])
        g = (fa * jax.nn.sigmoid(fa) * fb).astype(BF16)
        acc = acc + _dot(g, wout_ref[c])
    o_ref[...] = _rms(h2 + acc, gfin_ref[...])


def _ffn(h2, gain_f, wa, wb, wout, gain_fin):
    t, d = h2.shape
    tm = TM_FFN
    row = pl.BlockSpec((tm, d), lambda i: (i, 0))
    return pl.pallas_call(
        _ffn_kernel,
        grid=(t // tm,),
        in_specs=[row, _full((1, d)), _full(wa.shape), _full(wb.shape), _full(wout.shape),
                  _full((1, d))],
        out_specs=row,
        out_shape=jax.ShapeDtypeStruct((t, d), F32),
        compiler_params=_cparams(("arbitrary",)),
        name="ffn",
    )(h2, gain_f.reshape(1, d), wa, wb, wout, gain_fin.reshape(1, d))


def _block_diag(blocks, per):
    g, r, c = blocks.shape
    b = blocks.reshape(g // per, per, r, c)
    eye = jnp.eye(per, dtype=blocks.dtype)
    out = b[:, :, :, None, :] * eye[None, :, None, :, None]
    return out.reshape(g // per, per * r, per * c)


def _seg_perm(q):
    seg = q // SUBLANES
    r = jnp.arange(q)
    src = (r % SUBLANES) * seg + r // SUBLANES
    return (src[:, None] == jnp.arange(q)[None, :]).astype(BF16)


def _layer(x2, mem, bsz, seq, norm_mix, w_in, b_forget, lam_re, lam_im, log_dt, b_re, b_im, c_re,
           c_im, d_skip, w_glu, w_fox_o, w_mix_out, norm_mem_q, norm_mem_kv, w_mem_q, w_mem_kv,
           w_mem_o, norm_ffn, w_ffn_in, w_ffn_out, norm_final):
    d = x2.shape[-1]
    groups, states = lam_re.shape
    ssm_w = groups * SSM_GROUP
    n_heads = b_forget.shape[0]
    fox_w = n_heads * FOX_HEAD_DIM
    o_q = ssm_w
    o_k = o_q + fox_w
    o_v = o_k + fox_w
    o_f = o_v + fox_w
    o_g = o_f + n_heads

    a_re, a_im, s_re, s_im, bb_re, bb_im = _ssm_prep(lam_re, lam_im, log_dt, b_re, b_im)
    nstate = groups * states
    per = LANES // SSM_GROUP
    to_rows = lambda a: a.reshape(1, nstate)
    bd_in = lambda bb: _block_diag(
        bb.reshape(groups, states, SSM_GROUP).transpose(0, 2, 1), per).astype(BF16)
    bd_out = lambda c: _block_diag(c.transpose(0, 2, 1), per).astype(BF16)
    perm = _seg_perm(SSM_CHUNK)

    w = w_in.astype(BF16)
    wf = jnp.zeros((d, LANES), BF16).at[:, :n_heads].set(w[:, o_f:o_g])
    bf = jnp.zeros((1, LANES), F32).at[0, :n_heads].set(b_forget)
    u, q, k, v, gates = _in_proj(
        x2, norm_mix, w[:, :o_q], w[:, o_q:o_k], w[:, o_k:o_v], w[:, o_v:o_f], w[:, o_g:], wf, bf,
        bsz, seq)

    out_a = _ssm(u, perm, perm.T, bd_in(bb_re), bd_in(bb_im), to_rows(a_re), to_rows(a_im),
                 to_rows(s_re), to_rows(s_im), bd_out(c_re), bd_out(c_im),
                 d_skip.reshape(1, ssm_w), w_glu.astype(BF16), bsz, seq)

    att = _fox(q, k, v, bsz, seq)
    w_fo = jnp.pad(w_fox_o.reshape(n_heads, FOX_HEAD_DIM, d),
                   ((0, 0), (0, LANES - FOX_HEAD_DIM), (0, 0))).reshape(n_heads * LANES, d)

    k_m, v_m = _mem_kv(mem, norm_mem_kv, w_mem_kv)
    h2 = _mix_mem(x2, out_a, att, gates, w_fo.astype(BF16), w_mix_out.astype(BF16), norm_mem_q,
                  w_mem_q.astype(BF16), k_m, v_m, w_mem_o.astype(BF16), bsz, seq)
    hidden = w_ffn_out.shape[0]
    nchunk = 2
    hc = hidden // nchunk
    wi = w_ffn_in.astype(BF16)
    wa = wi[:, :hidden].reshape(d, nchunk, hc).transpose(1, 0, 2)
    wb = wi[:, hidden:].reshape(d, nchunk, hc).transpose(1, 0, 2)
    wout = w_ffn_out.astype(BF16).reshape(nchunk, hc, d)
    return _ffn(h2, norm_ffn, wa, wb, wout, norm_final)


def kernel(x, mem, norm_mix, w_in, b_forget, lam_re, lam_im, log_dt, b_re, b_im, c_re, c_im, d_skip,
           w_glu, w_fox_o, w_mix_out, norm_mem_q, norm_mem_kv, w_mem_q, w_mem_kv, w_mem_o, norm_ffn,
           w_ffn_in, w_ffn_out, norm_final):
    bsz, seq, d = x.shape
    assert w_in.shape[0] == 1, "single-layer block"
    out = _layer(x.reshape(bsz * seq, d), mem, bsz, seq, norm_mix[0], w_in[0], b_forget[0],
                 lam_re[0], lam_im[0], log_dt[0], b_re[0], b_im[0], c_re[0], c_im[0], d_skip[0],
                 w_glu[0], w_fox_o[0], w_mix_out[0], norm_mem_q[0], norm_mem_kv[0], w_mem_q[0],
                 w_mem_kv[0], w_mem_o[0], norm_ffn[0], w_ffn_in[0], w_ffn_out[0], norm_final)
    return out.reshape(bsz, seq, d)
```

```python
import functools
import math

import jax
import jax.numpy as jnp
from jax import lax
from jax.experimental import pallas as pl
from jax.experimental.pallas import tpu as pltpu

F32 = jnp.float32
BF16 = jnp.bfloat16

RMS_EPS = 1e-6
LOG2E = math.log2(math.e)
SSM_GROUP = 16
SSM_STATE = 64
FOX_HEAD_DIM = 64
MEM_HEADS = 4
LANES = 128
SUBLANES = 8
VMEM_LIMIT = 56 * 1024 * 1024

TM_PROJ = 512
TM_MIX = 512
TM_FFN = 256
SSM_CHUNK = 256
SSM_SEG = SSM_CHUNK // SUBLANES
SSM_COLS = 512
FOX_TQ = 1024
FOX_TK = 512
FOX_HEADS_PER_STEP = 4


def _cparams(sem):
    return pltpu.CompilerParams(dimension_semantics=sem, vmem_limit_bytes=VMEM_LIMIT)


def _rms(x, gain):
    return x * lax.rsqrt(jnp.mean(x * x, axis=-1, keepdims=True) + RMS_EPS) * gain


def _dot(a, b):
    return jnp.dot(a, b, preferred_element_type=F32)


def _dot_nt(a, b):
    return lax.dot_general(a, b, (((1,), (1,)), ((), ())), preferred_element_type=F32)


def _full(shape):
    n = len(shape)
    return pl.BlockSpec(shape, lambda *_: (0,) * n)


def _ssm_prep_kernel(lre_ref, lim_ref, ldt_ref, bre_ref, bim_ref,
                     are_ref, aim_ref, sre_ref, sim_ref, bbre_ref, bbim_ref):
    lre = lre_ref[...]
    lim = lim_ref[...]
    dt = jnp.exp(ldt_ref[...])
    zr = lre * dt
    zi = lim * dt
    mag = jnp.exp(zr)
    are = mag * jnp.cos(zi)
    aim = mag * jnp.sin(zi)
    are_ref[...] = are
    aim_ref[...] = aim
    mag_s = jnp.exp(zr * SSM_SEG)
    sre_ref[...] = mag_s * jnp.cos(zi * SSM_SEG)
    sim_ref[...] = mag_s * jnp.sin(zi * SSM_SEG)
    nr = are - 1.0
    ni = aim
    den = lre * lre + lim * lim
    fr = (nr * lre + ni * lim) / den
    fi = (ni * lre - nr * lim) / den
    bre = bre_ref[...]
    bim = bim_ref[...]
    bbre_ref[...] = fr * bre - fi * bim
    bbim_ref[...] = fr * bim + fi * bre


def _ssm_prep(lam_re, lam_im, log_dt, b_re, b_im):
    g, p = lam_re.shape
    n = b_re.shape[-1]
    c = g * p
    col = lambda a: a.reshape(c, 1)
    ldt = jnp.broadcast_to(log_dt[:, None], (g, p))
    outs = pl.pallas_call(
        _ssm_prep_kernel,
        out_shape=[jax.ShapeDtypeStruct((c, 1), F32)] * 4 + [jax.ShapeDtypeStruct((c, n), F32)] * 2,
        name="ssm_prep",
    )(col(lam_re), col(lam_im), col(ldt), b_re.reshape(c, n), b_im.reshape(c, n))
    return outs


def _mem_kv_kernel(m_ref, g_ref, w_ref, k_ref, v_ref):
    n = _rms(m_ref[0], g_ref[...]).astype(BF16)
    kv = _dot(n, w_ref[...])
    half = kv.shape[-1] // 2
    k_ref[0] = kv[:, :half].astype(BF16)
    v_ref[0] = kv[:, half:].astype(BF16)


def _mem_kv(mem, gain, w_kv):
    b, m, d = mem.shape
    w2 = w_kv.shape[-1]
    return pl.pallas_call(
        _mem_kv_kernel,
        grid=(b,),
        in_specs=[pl.BlockSpec((1, m, d), lambda i: (i, 0, 0)), _full((1, d)), _full((d, w2))],
        out_specs=[pl.BlockSpec((1, m, w2 // 2), lambda i: (i, 0, 0))] * 2,
        out_shape=[jax.ShapeDtypeStruct((b, m, w2 // 2), BF16)] * 2,
        compiler_params=_cparams(("arbitrary",)),
        name="mem_kv",
    )(mem, gain.reshape(1, d), w_kv.astype(BF16))


def _in_proj_kernel(x_ref, g_ref, wu_ref, wq_ref, wk_ref, wv_ref, wg_ref, wf_ref, bf_ref,
                    u_ref, q_ref, k_ref, v_ref, gate_ref, cum_ref, carry_ref):
    @pl.when(pl.program_id(1) == 0)
    def _():
        carry_ref[...] = jnp.zeros_like(carry_ref)

    un = _rms(x_ref[...], g_ref[...]).astype(BF16)
    u_ref[...] = _dot(un, wu_ref[...])
    gate_ref[...] = jax.nn.sigmoid(_dot(un, wg_ref[...]))
    q_ref[...] = (_dot(un, wq_ref[...]) * (FOX_HEAD_DIM ** -0.5 * LOG2E)).astype(BF16)
    k_ref[...] = _dot(un, wk_ref[...]).astype(BF16)
    v_ref[...] = _dot(un, wv_ref[...]).astype(BF16)
    c = jax.nn.log_sigmoid(_dot(un, wf_ref[...]) + bf_ref[...])
    rows = c.shape[0]
    row = lax.broadcasted_iota(jnp.int32, c.shape, 0)
    shift = 1
    while shift < rows:
        c = c + jnp.where(row >= shift, pltpu.roll(c, shift, 0), 0.0)
        shift *= 2
    c = c + carry_ref[...]
    carry_ref[...] = c[rows - 1:rows, :]
    cum_ref[0] = (c * LOG2E).T[:cum_ref.shape[1], :]


def _in_proj(x2, gain, wu, wq, wk, wv, wg, wf, bf, n_heads, bsz, seq):
    t, d = x2.shape
    tm = TM_PROJ
    nb = seq // tm
    row = lambda w: pl.BlockSpec((tm, w), lambda b, i: (b * nb + i, 0))
    ws = [wu, wq, wk, wv, wg, wf]
    return pl.pallas_call(
        _in_proj_kernel,
        grid=(bsz, nb),
        in_specs=[row(d), _full((1, d))] + [_full(w.shape) for w in ws] + [_full((1, LANES))],
        out_specs=[row(wu.shape[1]), row(wq.shape[1]), row(wk.shape[1]), row(wv.shape[1]),
                   row(wg.shape[1]), pl.BlockSpec((1, n_heads, tm), lambda b, i: (b, 0, i))],
        out_shape=[jax.ShapeDtypeStruct((t, wu.shape[1]), F32),
                   jax.ShapeDtypeStruct((t, wq.shape[1]), BF16),
                   jax.ShapeDtypeStruct((t, wk.shape[1]), BF16),
                   jax.ShapeDtypeStruct((t, wv.shape[1]), BF16),
                   jax.ShapeDtypeStruct((t, wg.shape[1]), F32),
                   jax.ShapeDtypeStruct((bsz, n_heads, seq), F32)],
        scratch_shapes=[pltpu.VMEM((1, LANES), F32)],
        compiler_params=_cparams(("arbitrary", "arbitrary")),
        name="in_proj",
    )(x2, gain.reshape(1, d), *ws, bf)


def _gelu_tanh(x):
    return 0.5 * x * (1.0 + jnp.tanh(math.sqrt(2.0 / math.pi) * (x + 0.044715 * (x * x * x))))


def _ssm_kernel(u_ref, perm_ref, permt_ref, bre_ref, bim_ref, are_ref, aim_ref, sre_ref, sim_ref,
                cre_ref, cim_ref, d_ref, wglu_ref, out_ref, hre, him, car_re, car_im):
    @pl.when(pl.program_id(1) == 0)
    def _():
        car_re[...] = jnp.zeros_like(car_re)
        car_im[...] = jnp.zeros_like(car_im)

    u = u_ref[...]
    up = _dot(perm_ref[...], u.astype(BF16)).astype(BF16)
    nkb = bre_ref.shape[0]
    kw = bre_ref.shape[1]
    nw = bre_ref.shape[2]
    for kb in range(nkb):
        ukb = up[:, kb * kw:(kb + 1) * kw]
        hre[:, kb * nw:(kb + 1) * nw] = _dot(ukb, bre_ref[kb])
        him[:, kb * nw:(kb + 1) * nw] = _dot(ukb, bim_ref[kb])

    nstate = hre.shape[1]
    for cb in range(nstate // SSM_COLS):
        cs = slice(cb * SSM_COLS, (cb + 1) * SSM_COLS)
        a_re = jnp.broadcast_to(are_ref[:, cs], (SUBLANES, SSM_COLS))
        a_im = jnp.broadcast_to(aim_ref[:, cs], (SUBLANES, SSM_COLS))

        e_re = hre[0:SUBLANES, cs]
        e_im = him[0:SUBLANES, cs]
        for k in range(1, SSM_SEG):
            r = slice(k * SUBLANES, (k + 1) * SUBLANES)
            e_re, e_im = (a_re * e_re - a_im * e_im + hre[r, cs],
                          a_re * e_im + a_im * e_re + him[r, cs])
            hre[r, cs] = e_re
            him[r, cs] = e_im

        s_re = sre_ref[:, cs]
        s_im = sim_ref[:, cs]
        c_re = car_re[:, cs]
        c_im = car_im[:, cs]
        rows_re, rows_im = [], []
        for j in range(SUBLANES):
            rows_re.append(c_re)
            rows_im.append(c_im)
            n_re = s_re * c_re - s_im * c_im + e_re[j:j + 1]
            n_im = s_re * c_im + s_im * c_re + e_im[j:j + 1]
            c_re, c_im = n_re, n_im
        car_re[:, cs] = c_re
        car_im[:, cs] = c_im

        d_re = jnp.concatenate(rows_re, axis=0)
        d_im = jnp.concatenate(rows_im, axis=0)
        for k in range(SSM_SEG):
            r = slice(k * SUBLANES, (k + 1) * SUBLANES)
            d_re, d_im = a_re * d_re - a_im * d_im, a_re * d_im + a_im * d_re
            hre[r, cs] = hre[r, cs] + d_re
            him[r, cs] = him[r, cs] + d_im

    ncb = cre_ref.shape[0]
    cw = cre_ref.shape[1]
    ys = []
    for kb in range(ncb):
        h_re = hre[:, kb * cw:(kb + 1) * cw].astype(BF16)
        h_im = him[:, kb * cw:(kb + 1) * cw].astype(BF16)
        ys.append(_dot(h_re, cre_ref[kb]) - _dot(h_im, cim_ref[kb]))
    yp = jnp.concatenate(ys, axis=-1)
    hi = yp.astype(BF16)
    lo = (yp - hi.astype(F32)).astype(BF16)
    y = _dot(permt_ref[...], hi) + _dot(permt_ref[...], lo)
    y = y + d_ref[...] * u
    z = _dot(_gelu_tanh(y).astype(BF16), wglu_ref[...])
    half = z.shape[-1] // 2
    out_ref[...] = z[:, :half] * jax.nn.sigmoid(z[:, half:])


def _ssm(u, perm, permt, bre, bim, a_re, a_im, s_re, s_im, cre, cim, d_skip, w_glu, bsz, seq):
    t, w = u.shape
    q = SSM_CHUNK
    nb = seq // q
    nstate = a_re.shape[1]
    dm = w_glu.shape[1] // 2
    row = lambda width: pl.BlockSpec((q, width), lambda b, i: (b * nb + i, 0))
    consts = [perm, permt, bre, bim, a_re, a_im, s_re, s_im, cre, cim, d_skip, w_glu]
    return pl.pallas_call(
        _ssm_kernel,
        grid=(bsz, nb),
        in_specs=[row(w)] + [_full(c.shape) for c in consts],
        out_specs=row(dm),
        out_shape=jax.ShapeDtypeStruct((t, dm), F32),
        scratch_shapes=[pltpu.VMEM((q, nstate), F32), pltpu.VMEM((q, nstate), F32),
                        pltpu.VMEM((1, nstate), F32), pltpu.VMEM((1, nstate), F32)],
        compiler_params=_cparams(("arbitrary", "arbitrary")),
        name="ssm",
    )(u, *consts)


def _fox_kernel(q_ref, k_ref, v_ref, ck_ref, o_ref, m_scr, acc_scr):
    seq = q_ref.shape[0]
    tq = FOX_TQ
    tk = FOX_TK
    dh = FOX_HEAD_DIM
    n_heads = ck_ref.shape[1]
    lane = lax.broadcasted_iota(jnp.int32, (1, LANES), 1)
    keep = [jnp.where(lane < dh, 1.0, 0.0).astype(BF16), jnp.where(lane >= dh, 1.0, 0.0).astype(BF16)]
    sum_lane = [dh, 0]
    ones = [jnp.where(lane == sl, 1.0, 0.0).astype(BF16) for sl in sum_lane]

    def block(q, j, rows, h, masked):
        e = h % 2
        ps = slice((h // 2) * LANES, (h // 2 + 1) * LANES)
        hs = slice(h * LANES, (h + 1) * LANES)
        ks = pl.ds(pl.multiple_of(j * tk, tk), tk)
        s = _dot_nt(q, k_ref[ks, ps] * keep[e]) - ck_ref[0, h, pl.ds(j, 1), :]
        if masked:
            r = lax.broadcasted_iota(jnp.int32, s.shape, 0)
            c = lax.broadcasted_iota(jnp.int32, s.shape, 1)
            s = jnp.where(c <= r, s, -jnp.inf)
        m_prev = m_scr[rows, hs]
        m_new = jnp.maximum(m_prev, jnp.max(s, axis=1, keepdims=True))
        p = jnp.exp2(s - jnp.concatenate([m_new] * (tk // LANES), axis=1))
        acc_scr[rows, hs] = (jnp.exp2(m_prev - m_new) * acc_scr[rows, hs]
                             + _dot(p.astype(BF16), v_ref[ks, ps] * keep[e] + ones[e]))
        m_scr[rows, hs] = m_new

    nsub = tq // tk
    all_rows = slice(0, tq)

    def qblock(qi, _):
        qs = pl.ds(pl.multiple_of(qi * tq, tq), tq)
        qp = [q_ref[qs, p * LANES:(p + 1) * LANES] for p in range(n_heads // 2)]
        m_scr[...] = jnp.full(m_scr.shape, -jnp.inf, F32)
        acc_scr[...] = jnp.zeros(acc_scr.shape, F32)

        def full(j, _):
            for h in range(n_heads):
                block(qp[h // 2], j, all_rows, h, False)
            return 0

        lax.fori_loop(0, qi * nsub, full, 0)
        for r in range(nsub):
            rows = slice(r * tk, (r + 1) * tk)
            for h in range(n_heads):
                for jj in range(r):
                    block(qp[h // 2][rows], qi * nsub + jj, rows, h, False)
                block(qp[h // 2][rows], qi * nsub + r, rows, h, True)
            for p in range(n_heads // 2):
                a0 = acc_scr[rows, (2 * p) * LANES:(2 * p + 1) * LANES]
                a1 = acc_scr[rows, (2 * p + 1) * LANES:(2 * p + 2) * LANES]
                out = jnp.where(lane < dh, a0 / a0[:, sum_lane[0]:sum_lane[0] + 1],
                                a1 / a1[:, sum_lane[1]:sum_lane[1] + 1])
                o_ref[pl.ds(pl.multiple_of(qi * tq + r * tk, tk), tk),
                      p * LANES:(p + 1) * LANES] = out.astype(o_ref.dtype)
        return 0

    lax.fori_loop(0, seq // tq, qblock, 0)


def _fox(q, k, v, cum_t, bsz, seq):
    t, w = q.shape
    assert FOX_TQ % FOX_TK == 0 and seq % FOX_TQ == 0 and FOX_HEADS_PER_STEP % 2 == 0
    hps = FOX_HEADS_PER_STEP
    wblk = hps * FOX_HEAD_DIM
    blk = pl.BlockSpec((seq, wblk), lambda bi, hi: (bi, hi))
    return pl.pallas_call(
        _fox_kernel,
        grid=(bsz, w // wblk),
        in_specs=[blk, blk, blk,
                  pl.BlockSpec((1, hps) + cum_t.shape[2:], lambda bi, hi: (bi, hi, 0, 0))],
        out_specs=blk,
        out_shape=jax.ShapeDtypeStruct((t, w), BF16),
        scratch_shapes=[pltpu.VMEM((FOX_TQ, hps * LANES), F32),
                        pltpu.VMEM((FOX_TQ, hps * LANES), F32)],
        compiler_params=_cparams(("arbitrary", "arbitrary")),
        name="fox",
    )(q, k, v, cum_t)


def _mix_mem_kernel(x_ref, oa_ref, att_ref, gate_ref, wfo_ref, wmix_ref, gq_ref, wq_ref,
                    km_ref, vm_ref, wo_ref, h_ref):
    d = x_ref.shape[-1]
    out_b = _dot(att_ref[...], wfo_ref[...])
    mix = gate_ref[:, :d] * oa_ref[...] + gate_ref[:, d:] * out_b
    h1 = x_ref[...] + _dot(mix.astype(BF16), wmix_ref[...])
    n = _rms(h1, gq_ref[...]).astype(BF16)
    qm = _dot(n, wq_ref[...])
    wm = qm.shape[-1]
    hd = wm // MEM_HEADS
    qm = (qm * (hd ** -0.5)).astype(BF16)
    outs = []
    for hh in range(MEM_HEADS):
        hs = slice(hh * hd, (hh + 1) * hd)
        s = _dot_nt(qm[:, hs], km_ref[0, :, hs])
        s = s - jnp.max(s, axis=-1, keepdims=True)
        p = jnp.exp(s)
        p = p / jnp.sum(p, axis=-1, keepdims=True)
        outs.append(_dot(p.astype(BF16), vm_ref[0, :, hs]))
    o = jnp.concatenate(outs, axis=-1).astype(BF16)
    h_ref[...] = h1 + _dot(o, wo_ref[...])


def _mix_mem(x2, out_a, att, gates, w_fox_o, w_mix, gain_q, w_q, k_m, v_m, w_o, bsz, seq):
    t, d = x2.shape
    tm = TM_MIX
    nb = seq // tm
    row = lambda w: pl.BlockSpec((tm, w), lambda b, i: (b * nb + i, 0))
    mem = pl.BlockSpec((1,) + k_m.shape[1:], lambda b, i: (b, 0, 0))
    return pl.pallas_call(
        _mix_mem_kernel,
        grid=(bsz, nb),
        in_specs=[row(d), row(d), row(att.shape[1]), row(gates.shape[1]),
                  _full(w_fox_o.shape), _full(w_mix.shape), _full((1, d)), _full(w_q.shape),
                  mem, mem, _full(w_o.shape)],
        out_specs=row(d),
        out_shape=jax.ShapeDtypeStruct((t, d), F32),
        compiler_params=_cparams(("arbitrary", "arbitrary")),
        name="mix_mem",
    )(x2, out_a, att, gates, w_fox_o, w_mix, gain_q.reshape(1, d), w_q, k_m, v_m, w_o)


def _ffn_kernel(h_ref, gf_ref, wa_ref, wb_ref, wout_ref, gfin_ref, o_ref):
    h2 = h_ref[...]
    f = _rms(h2, gf_ref[...]).astype(BF16)
    acc = jnp.zeros_like(h2)
    for c in range(wa_ref.shape[0]):
        fa = _dot(f, wa_ref[c])
        fb = _dot(f, wb_ref[c])
        g = (fa * jax.nn.sigmoid(fa) * fb).astype(BF16)
        acc = acc + _dot(g, wout_ref[c])
    o_ref[...] = _rms(h2 + acc, gfin_ref[...])


def _ffn(h2, gain_f, wa, wb, wout, gain_fin):
    t, d = h2.shape
    tm = TM_FFN
    row = pl.BlockSpec((tm, d), lambda i: (i, 0))
    return pl.pallas_call(
        _ffn_kernel,
        grid=(t // tm,),
        in_specs=[row, _full((1, d)), _full(wa.shape), _full(wb.shape), _full(wout.shape),
                  _full((1, d))],
        out_specs=row,
        out_shape=jax.ShapeDtypeStruct((t, d), F32),
        compiler_params=_cparams(("arbitrary",)),
        name="ffn",
    )(h2, gain_f.reshape(1, d), wa, wb, wout, gain_fin.reshape(1, d))


def _block_diag(blocks, per):
    g, r, c = blocks.shape
    b = blocks.reshape(g // per, per, r, c)
    eye = jnp.eye(per, dtype=blocks.dtype)
    out = b[:, :, :, None, :] * eye[None, :, None, :, None]
    return out.reshape(g // per, per * r, per * c)


def _seg_perm(q):
    seg = q // SUBLANES
    r = jnp.arange(q)
    src = (r % SUBLANES) * seg + r // SUBLANES
    return (src[:, None] == jnp.arange(q)[None, :]).astype(BF16)


def _layer(x2, mem, bsz, seq, norm_mix, w_in, b_forget, lam_re, lam_im, log_dt, b_re, b_im, c_re,
           c_im, d_skip, w_glu, w_fox_o, w_mix_out, norm_mem_q, norm_mem_kv, w_mem_q, w_mem_kv,
           w_mem_o, norm_ffn, w_ffn_in, w_ffn_out, norm_final):
    d = x2.shape[-1]
    groups, states = lam_re.shape
    ssm_w = groups * SSM_GROUP
    n_heads = b_forget.shape[0]
    fox_w = n_heads * FOX_HEAD_DIM
    o_q = ssm_w
    o_k = o_q + fox_w
    o_v = o_k + fox_w
    o_f = o_v + fox_w
    o_g = o_f + n_heads

    a_re, a_im, s_re, s_im, bb_re, bb_im = _ssm_prep(lam_re, lam_im, log_dt, b_re, b_im)
    nstate = groups * states
    per = LANES // SSM_GROUP
    to_rows = lambda a: a.reshape(1, nstate)
    bd_in = lambda bb: _block_diag(
        bb.reshape(groups, states, SSM_GROUP).transpose(0, 2, 1), per).astype(BF16)
    bd_out = lambda c: _block_diag(c.transpose(0, 2, 1), per).astype(BF16)
    perm = _seg_perm(SSM_CHUNK)

    w = w_in.astype(BF16)
    wf = jnp.zeros((d, LANES), BF16).at[:, :n_heads].set(w[:, o_f:o_g])
    bf = jnp.zeros((1, LANES), F32).at[0, :n_heads].set(b_forget)
    u, q, k, v, gates, cum_t = _in_proj(
        x2, norm_mix, w[:, :o_q], w[:, o_q:o_k], w[:, o_k:o_v], w[:, o_v:o_f], w[:, o_g:], wf, bf,
        n_heads, bsz, seq)

    out_a = _ssm(u, perm, perm.T, bd_in(bb_re), bd_in(bb_im), to_rows(a_re), to_rows(a_im),
                 to_rows(s_re), to_rows(s_im), bd_out(c_re), bd_out(c_im),
                 d_skip.reshape(1, ssm_w), w_glu.astype(BF16), bsz, seq)

    att = _fox(q, k, v, cum_t.reshape(bsz, n_heads, seq // FOX_TK, FOX_TK), bsz, seq)

    k_m, v_m = _mem_kv(mem, norm_mem_kv, w_mem_kv)
    h2 = _mix_mem(x2, out_a, att, gates, w_fox_o.astype(BF16), w_mix_out.astype(BF16), norm_mem_q,
                  w_mem_q.astype(BF16), k_m, v_m, w_mem_o.astype(BF16), bsz, seq)
    hidden = w_ffn_out.shape[0]
    nchunk = 2
    hc = hidden // nchunk
    wi = w_ffn_in.astype(BF16)
    wa = wi[:, :hidden].reshape(d, nchunk, hc).transpose(1, 0, 2)
    wb = wi[:, hidden:].reshape(d, nchunk, hc).transpose(1, 0, 2)
    wout = w_ffn_out.astype(BF16).reshape(nchunk, hc, d)
    return _ffn(h2, norm_ffn, wa, wb, wout, norm_final)


def kernel(x, mem, norm_mix, w_in, b_forget, lam_re, lam_im, log_dt, b_re, b_im, c_re, c_im, d_skip,
           w_glu, w_fox_o, w_mix_out, norm_mem_q, norm_mem_kv, w_mem_q, w_mem_kv, w_mem_o, norm_ffn,
           w_ffn_in, w_ffn_out, norm_final):
    bsz, seq, d = x.shape
    assert w_in.shape[0] == 1, "single-layer block"
    out = _layer(x.reshape(bsz * seq, d), mem, bsz, seq, norm_mix[0], w_in[0], b_forget[0],
                 lam_re[0], lam_im[0], log_dt[0], b_re[0], b_im[0], c_re[0], c_im[0], d_skip[0],
                 w_glu[0], w_fox_o[0], w_mix_out[0], norm_mem_q[0], norm_mem_kv[0], w_mem_q[0],
                 w_mem_kv[0], w_mem_o[0], norm_ffn[0], w_ffn_in[0], w_ffn_out[0], norm_final)
    return out.reshape(bsz, seq, d)
```

```python
import functools
import math

import jax
import jax.numpy as jnp
from jax import lax
from jax.experimental import pallas as pl
from jax.experimental.pallas import tpu as pltpu

F32 = jnp.float32
BF16 = jnp.bfloat16

RMS_EPS = 1e-6
LOG2E = math.log2(math.e)
SSM_GROUP = 16
SSM_STATE = 64
FOX_HEAD_DIM = 64
MEM_HEADS = 4
LANES = 128
SUBLANES = 8
VMEM_LIMIT = 56 * 1024 * 1024

TM_PROJ = 512
TM_POST = 512
FFN_CHUNKS = 2
SSM_CHUNK = 256
SSM_SEG = SSM_CHUNK // SUBLANES
SSM_COLS = 512
FOX_TQ = 1024
FOX_TK = 512
FOX_HEADS_PER_STEP = 4


def _cparams(sem):
    return pltpu.CompilerParams(dimension_semantics=sem, vmem_limit_bytes=VMEM_LIMIT)


def _rms(x, gain):
    return x * lax.rsqrt(jnp.mean(x * x, axis=-1, keepdims=True) + RMS_EPS) * gain


def _dot(a, b):
    return jnp.dot(a, b, preferred_element_type=F32)


def _dot_nt(a, b):
    return lax.dot_general(a, b, (((1,), (1,)), ((), ())), preferred_element_type=F32)


def _full(shape):
    n = len(shape)
    return pl.BlockSpec(shape, lambda *_: (0,) * n)


def _ssm_prep_kernel(lre_ref, lim_ref, ldt_ref, bre_ref, bim_ref,
                     are_ref, aim_ref, sre_ref, sim_ref, bbre_ref, bbim_ref):
    lre = lre_ref[...]
    lim = lim_ref[...]
    dt = jnp.exp(ldt_ref[...])
    zr = lre * dt
    zi = lim * dt
    mag = jnp.exp(zr)
    are = mag * jnp.cos(zi)
    aim = mag * jnp.sin(zi)
    are_ref[...] = are
    aim_ref[...] = aim
    mag_s = jnp.exp(zr * SSM_SEG)
    sre_ref[...] = mag_s * jnp.cos(zi * SSM_SEG)
    sim_ref[...] = mag_s * jnp.sin(zi * SSM_SEG)
    nr = are - 1.0
    ni = aim
    den = lre * lre + lim * lim
    fr = (nr * lre + ni * lim) / den
    fi = (ni * lre - nr * lim) / den
    bre = bre_ref[...]
    bim = bim_ref[...]
    bbre_ref[...] = fr * bre - fi * bim
    bbim_ref[...] = fr * bim + fi * bre


def _ssm_prep(lam_re, lam_im, log_dt, b_re, b_im):
    g, p = lam_re.shape
    n = b_re.shape[-1]
    c = g * p
    row = lambda a: a.reshape(1, c)
    chan = lambda b: b.transpose(2, 0, 1).reshape(n, c)
    ldt = jnp.broadcast_to(log_dt[:, None], (g, p))
    outs = pl.pallas_call(
        _ssm_prep_kernel,
        out_shape=[jax.ShapeDtypeStruct((1, c), F32)] * 4 + [jax.ShapeDtypeStruct((n, c), F32)] * 2,
        name="ssm_prep",
    )(row(lam_re), row(lam_im), row(ldt), chan(b_re), chan(b_im))
    return outs


def _mem_kv_kernel(m_ref, g_ref, w_ref, k_ref, v_ref):
    n = _rms(m_ref[0], g_ref[...]).astype(BF16)
    kv = _dot(n, w_ref[...])
    half = kv.shape[-1] // 2
    k_ref[0] = kv[:, :half].astype(BF16)
    v_ref[0] = kv[:, half:].astype(BF16)


def _mem_kv(mem, gain, w_kv):
    b, m, d = mem.shape
    w2 = w_kv.shape[-1]
    return pl.pallas_call(
        _mem_kv_kernel,
        grid=(b,),
        in_specs=[pl.BlockSpec((1, m, d), lambda i: (i, 0, 0)), _full((1, d)), _full((d, w2))],
        out_specs=[pl.BlockSpec((1, m, w2 // 2), lambda i: (i, 0, 0))] * 2,
        out_shape=[jax.ShapeDtypeStruct((b, m, w2 // 2), BF16)] * 2,
        compiler_params=_cparams(("arbitrary",)),
        name="mem_kv",
    )(mem, gain.reshape(1, d), w_kv.astype(BF16))


def _in_proj_kernel(x_ref, g_ref, w_ref, wf_ref, bf_ref,
                    u_ref, q_ref, k_ref, v_ref, cum_ref, carry_ref):
    @pl.when(pl.program_id(1) == 0)
    def _():
        carry_ref[...] = jnp.zeros_like(carry_ref)

    un = _rms(x_ref[...], g_ref[...]).astype(BF16)
    wu = u_ref.shape[1]
    wh = q_ref.shape[1]
    col = lambda lo, hi: _dot(un, w_ref[:, lo:hi])
    u_ref[...] = col(0, wu)
    q_ref[...] = (col(wu, wu + wh) * (FOX_HEAD_DIM ** -0.5 * LOG2E)).astype(BF16)
    k_ref[...] = col(wu + wh, wu + 2 * wh).astype(BF16)
    v_ref[...] = col(wu + 2 * wh, wu + 3 * wh).astype(BF16)
    c = jax.nn.log_sigmoid(_dot(un, wf_ref[...]) + bf_ref[...])
    rows = c.shape[0]
    row = lax.broadcasted_iota(jnp.int32, c.shape, 0)
    shift = 1
    while shift < rows:
        c = c + jnp.where(row >= shift, pltpu.roll(c, shift, 0), 0.0)
        shift *= 2
    c = c + carry_ref[...]
    carry_ref[...] = c[rows - 1:rows, :]
    cum_ref[0] = (c * LOG2E).T[:cum_ref.shape[1], :]


def _in_proj(x2, gain, w_uqkv, wf, bf, ssm_w, fox_w, n_heads, bsz, seq):
    t, d = x2.shape
    tm = TM_PROJ
    nb = seq // tm
    row = lambda w: pl.BlockSpec((tm, w), lambda b, i: (b * nb + i, 0))
    return pl.pallas_call(
        _in_proj_kernel,
        grid=(bsz, nb),
        in_specs=[row(d), _full((1, d)), _full(w_uqkv.shape), _full(wf.shape), _full((1, LANES))],
        out_specs=[row(ssm_w), row(fox_w), row(fox_w), row(fox_w),
                   pl.BlockSpec((1, n_heads, tm), lambda b, i: (b, 0, i))],
        out_shape=[jax.ShapeDtypeStruct((t, ssm_w), F32),
                   jax.ShapeDtypeStruct((t, fox_w), BF16),
                   jax.ShapeDtypeStruct((t, fox_w), BF16),
                   jax.ShapeDtypeStruct((t, fox_w), BF16),
                   jax.ShapeDtypeStruct((bsz, n_heads, seq), F32)],
        scratch_shapes=[pltpu.VMEM((1, LANES), F32)],
        compiler_params=_cparams(("arbitrary", "arbitrary")),
        name="in_proj",
    )(x2, gain.reshape(1, d), w_uqkv, wf, bf)


def _gelu_tanh(x):
    return 0.5 * x * (1.0 + jnp.tanh(math.sqrt(2.0 / math.pi) * (x + 0.044715 * (x * x * x))))


def _ssm_kernel(u_ref, perm_ref, permt_ref, bre_ref, bim_ref, are_ref, aim_ref, sre_ref, sim_ref,
                cre_ref, cim_ref, d_ref, wglu_ref, out_ref, hre, him, car_re, car_im):
    @pl.when(pl.program_id(1) == 0)
    def _():
        car_re[...] = jnp.zeros_like(car_re)
        car_im[...] = jnp.zeros_like(car_im)

    u = u_ref[...]
    up = _dot(perm_ref[...], u.astype(BF16)).astype(BF16)
    nkb = bre_ref.shape[0]
    kw = bre_ref.shape[1]
    nw = bre_ref.shape[2]
    for kb in range(nkb):
        ukb = up[:, kb * kw:(kb + 1) * kw]
        hre[:, kb * nw:(kb + 1) * nw] = _dot(ukb, bre_ref[kb])
        him[:, kb * nw:(kb + 1) * nw] = _dot(ukb, bim_ref[kb])

    nstate = hre.shape[1]
    for cb in range(nstate // SSM_COLS):
        cs = slice(cb * SSM_COLS, (cb + 1) * SSM_COLS)
        a_re = jnp.broadcast_to(are_ref[:, cs], (SUBLANES, SSM_COLS))
        a_im = jnp.broadcast_to(aim_ref[:, cs], (SUBLANES, SSM_COLS))

        e_re = hre[0:SUBLANES, cs]
        e_im = him[0:SUBLANES, cs]
        for k in range(1, SSM_SEG):
            r = slice(k * SUBLANES, (k + 1) * SUBLANES)
            e_re, e_im = (a_re * e_re - a_im * e_im + hre[r, cs],
                          a_re * e_im + a_im * e_re + him[r, cs])
            hre[r, cs] = e_re
            him[r, cs] = e_im

        s_re = sre_ref[:, cs]
        s_im = sim_ref[:, cs]
        c_re = car_re[:, cs]
        c_im = car_im[:, cs]
        rows_re, rows_im = [], []
        for j in range(SUBLANES):
            rows_re.append(c_re)
            rows_im.append(c_im)
            n_re = s_re * c_re - s_im * c_im + e_re[j:j + 1]
            n_im = s_re * c_im + s_im * c_re + e_im[j:j + 1]
            c_re, c_im = n_re, n_im
        car_re[:, cs] = c_re
        car_im[:, cs] = c_im

        d_re = jnp.concatenate(rows_re, axis=0)
        d_im = jnp.concatenate(rows_im, axis=0)
        for k in range(SSM_SEG):
            r = slice(k * SUBLANES, (k + 1) * SUBLANES)
            d_re, d_im = a_re * d_re - a_im * d_im, a_re * d_im + a_im * d_re
            hre[r, cs] = hre[r, cs] + d_re
            him[r, cs] = him[r, cs] + d_im

    ncb = cre_ref.shape[0]
    cw = cre_ref.shape[1]
    ys = []
    for kb in range(ncb):
        h_re = hre[:, kb * cw:(kb + 1) * cw].astype(BF16)
        h_im = him[:, kb * cw:(kb + 1) * cw].astype(BF16)
        ys.append(_dot(h_re, cre_ref[kb]) - _dot(h_im, cim_ref[kb]))
    yp = jnp.concatenate(ys, axis=-1)
    hi = yp.astype(BF16)
    lo = (yp - hi.astype(F32)).astype(BF16)
    y = _dot(permt_ref[...], hi) + _dot(permt_ref[...], lo)
    y = y + d_ref[...] * u
    z = _dot(_gelu_tanh(y).astype(BF16), wglu_ref[...])
    half = z.shape[-1] // 2
    out_ref[...] = z[:, :half] * jax.nn.sigmoid(z[:, half:])


def _ssm(u, perm, permt, bre, bim, a_re, a_im, s_re, s_im, cre, cim, d_skip, w_glu, bsz, seq):
    t, w = u.shape
    q = SSM_CHUNK
    nb = seq // q
    nstate = a_re.shape[1]
    dm = w_glu.shape[1] // 2
    row = lambda width: pl.BlockSpec((q, width), lambda b, i: (b * nb + i, 0))
    consts = [perm, permt, bre, bim, a_re, a_im, s_re, s_im, cre, cim, d_skip, w_glu]
    return pl.pallas_call(
        _ssm_kernel,
        grid=(bsz, nb),
        in_specs=[row(w)] + [_full(c.shape) for c in consts],
        out_specs=row(dm),
        out_shape=jax.ShapeDtypeStruct((t, dm), F32),
        scratch_shapes=[pltpu.VMEM((q, nstate), F32), pltpu.VMEM((q, nstate), F32),
                        pltpu.VMEM((1, nstate), F32), pltpu.VMEM((1, nstate), F32)],
        compiler_params=_cparams(("arbitrary", "arbitrary")),
        name="ssm",
    )(u, *consts)


def _fox_kernel(q_ref, k_ref, v_ref, ck_ref, o_ref, m_scr, acc_scr):
    seq = q_ref.shape[0]
    tq = FOX_TQ
    tk = FOX_TK
    dh = FOX_HEAD_DIM
    n_heads = ck_ref.shape[1]
    lane = lax.broadcasted_iota(jnp.int32, (1, LANES), 1)
    keep = [jnp.where(lane < dh, 1.0, 0.0).astype(BF16), jnp.where(lane >= dh, 1.0, 0.0).astype(BF16)]
    sum_lane = [dh, 0]
    ones = [jnp.where(lane == sl, 1.0, 0.0).astype(BF16) for sl in sum_lane]

    def block(q, j, rows, h, masked):
        e = h % 2
        ps = slice((h // 2) * LANES, (h // 2 + 1) * LANES)
        hs = slice(h * LANES, (h + 1) * LANES)
        ks = pl.ds(pl.multiple_of(j * tk, tk), tk)
        s = _dot_nt(q, k_ref[ks, ps] * keep[e]) - ck_ref[0, h, pl.ds(j, 1), :]
        if masked:
            r = lax.broadcasted_iota(jnp.int32, s.shape, 0)
            c = lax.broadcasted_iota(jnp.int32, s.shape, 1)
            s = jnp.where(c <= r, s, -jnp.inf)
        m_prev = m_scr[rows, hs]
        m_new = jnp.maximum(m_prev, jnp.max(s, axis=1, keepdims=True))
        p = jnp.exp2(s - jnp.concatenate([m_new] * (tk // LANES), axis=1))
        acc_scr[rows, hs] = (jnp.exp2(m_prev - m_new) * acc_scr[rows, hs]
                             + _dot(p.astype(BF16), v_ref[ks, ps] * keep[e] + ones[e]))
        m_scr[rows, hs] = m_new

    nsub = tq // tk
    all_rows = slice(0, tq)

    def qblock(qi, _):
        qs = pl.ds(pl.multiple_of(qi * tq, tq), tq)
        qp = [q_ref[qs, p * LANES:(p + 1) * LANES] for p in range(n_heads // 2)]
        m_scr[...] = jnp.full(m_scr.shape, -jnp.inf, F32)
        acc_scr[...] = jnp.zeros(acc_scr.shape, F32)

        def full(j, _):
            for h in range(n_heads):
                block(qp[h // 2], j, all_rows, h, False)
            return 0

        lax.fori_loop(0, qi * nsub, full, 0)
        for r in range(nsub):
            rows = slice(r * tk, (r + 1) * tk)
            for h in range(n_heads):
                for jj in range(r):
                    block(qp[h // 2][rows], qi * nsub + jj, rows, h, False)
                block(qp[h // 2][rows], qi * nsub + r, rows, h, True)
            for p in range(n_heads // 2):
                a0 = acc_scr[rows, (2 * p) * LANES:(2 * p + 1) * LANES]
                a1 = acc_scr[rows, (2 * p + 1) * LANES:(2 * p + 2) * LANES]
                out = jnp.where(lane < dh, a0 / a0[:, sum_lane[0]:sum_lane[0] + 1],
                                a1 / a1[:, sum_lane[1]:sum_lane[1] + 1])
                o_ref[pl.ds(pl.multiple_of(qi * tq + r * tk, tk), tk),
                      p * LANES:(p + 1) * LANES] = out.astype(o_ref.dtype)
        return 0

    lax.fori_loop(0, seq // tq, qblock, 0)


def _fox(q, k, v, cum_t, bsz, seq):
    t, w = q.shape
    assert FOX_TQ % FOX_TK == 0 and seq % FOX_TQ == 0 and FOX_HEADS_PER_STEP % 2 == 0
    hps = FOX_HEADS_PER_STEP
    wblk = hps * FOX_HEAD_DIM
    blk = pl.BlockSpec((seq, wblk), lambda bi, hi: (bi, hi))
    return pl.pallas_call(
        _fox_kernel,
        grid=(bsz, w // wblk),
        in_specs=[blk, blk, blk,
                  pl.BlockSpec((1, hps) + cum_t.shape[2:], lambda bi, hi: (bi, hi, 0, 0))],
        out_specs=blk,
        out_shape=jax.ShapeDtypeStruct((t, w), BF16),
        scratch_shapes=[pltpu.VMEM((FOX_TQ, hps * LANES), F32),
                        pltpu.VMEM((FOX_TQ, hps * LANES), F32)],
        compiler_params=_cparams(("arbitrary", "arbitrary")),
        name="fox",
    )(q, k, v, cum_t)


def _post_kernel(x_ref, oa_ref, att_ref, gmix_ref, wg_ref, wfo_ref, wmix_ref, gq_ref, wq_ref,
                 km_ref, vm_ref, wo_ref, gf_ref, win_ref, wout_ref, gfin_ref, o_ref):
    d = x_ref.shape[-1]
    x = x_ref[...]
    gate = jax.nn.sigmoid(_dot(_rms(x, gmix_ref[...]).astype(BF16), wg_ref[...]))
    out_b = _dot(att_ref[...], wfo_ref[...])
    mix = gate[:, :d] * oa_ref[...] + gate[:, d:] * out_b
    h1 = x + _dot(mix.astype(BF16), wmix_ref[...])

    n = _rms(h1, gq_ref[...]).astype(BF16)
    qm = _dot(n, wq_ref[...])
    hd = qm.shape[-1] // MEM_HEADS
    qm = (qm * (hd ** -0.5)).astype(BF16)
    outs = []
    for hh in range(MEM_HEADS):
        hs = slice(hh * hd, (hh + 1) * hd)
        s = _dot_nt(qm[:, hs], km_ref[0, :, hs])
        s = s - jnp.max(s, axis=-1, keepdims=True)
        p = jnp.exp(s)
        p = p / jnp.sum(p, axis=-1, keepdims=True)
        outs.append(_dot(p.astype(BF16), vm_ref[0, :, hs]))
    o = jnp.concatenate(outs, axis=-1).astype(BF16)
    h2 = h1 + _dot(o, wo_ref[...])

    f = _rms(h2, gf_ref[...]).astype(BF16)
    hidden = wout_ref.shape[0]
    hc = hidden // FFN_CHUNKS
    acc = jnp.zeros_like(h2)
    for c in range(FFN_CHUNKS):
        fa = _dot(f, win_ref[:, c * hc:(c + 1) * hc])
        fb = _dot(f, win_ref[:, hidden + c * hc:hidden + (c + 1) * hc])
        g = (fa * jax.nn.sigmoid(fa) * fb).astype(BF16)
        acc = acc + _dot(g, wout_ref[c * hc:(c + 1) * hc, :])
    o_ref[...] = _rms(h2 + acc, gfin_ref[...])


def _resident(shape):
    n = len(shape)
    return pl.BlockSpec(shape, lambda *_: (0,) * n, pipeline_mode=pl.Buffered(1))


def _post(x2, out_a, att, gain_mix, w_gate, w_fox_o, w_mix, gain_q, w_q, k_m, v_m, w_o,
          gain_f, w_ffn_in, w_ffn_out, gain_fin, bsz, seq):
    t, d = x2.shape
    tm = TM_POST
    nb = seq // tm
    assert (w_ffn_out.shape[0] // FFN_CHUNKS) % LANES == 0
    row = lambda w: pl.BlockSpec((tm, w), lambda b, i: (b * nb + i, 0))
    mem = pl.BlockSpec((1,) + k_m.shape[1:], lambda b, i: (b, 0, 0))
    gain = lambda g: g.reshape(1, d)
    weights = lambda *ws: [_resident(w.shape) for w in ws]
    return pl.pallas_call(
        _post_kernel,
        grid=(bsz, nb),
        in_specs=[row(d), row(d), row(att.shape[1]), _full((1, d))]
                 + weights(w_gate, w_fox_o, w_mix) + [_full((1, d))] + weights(w_q)
                 + [mem, mem] + weights(w_o) + [_full((1, d))] + weights(w_ffn_in, w_ffn_out)
                 + [_full((1, d))],
        out_specs=row(d),
        out_shape=jax.ShapeDtypeStruct((t, d), F32),
        compiler_params=_cparams(("arbitrary", "arbitrary")),
        name="post",
    )(x2, out_a, att, gain(gain_mix), w_gate, w_fox_o, w_mix, gain(gain_q), w_q, k_m, v_m, w_o,
      gain(gain_f), w_ffn_in, w_ffn_out, gain(gain_fin))


def _block_diag(blocks, per):
    g, r, c = blocks.shape
    b = blocks.reshape(g // per, per, r, c)
    eye = jnp.eye(per, dtype=blocks.dtype)
    out = b[:, :, :, None, :] * eye[None, :, None, :, None]
    return out.reshape(g // per, per * r, per * c)


def _seg_perm(q):
    seg = q // SUBLANES
    r = jnp.arange(q)
    src = (r % SUBLANES) * seg + r // SUBLANES
    return (src[:, None] == jnp.arange(q)[None, :]).astype(BF16)


def _layer(x2, mem, bsz, seq, norm_mix, w_in, b_forget, lam_re, lam_im, log_dt, b_re, b_im, c_re,
           c_im, d_skip, w_glu, w_fox_o, w_mix_out, norm_mem_q, norm_mem_kv, w_mem_q, w_mem_kv,
           w_mem_o, norm_ffn, w_ffn_in, w_ffn_out, norm_final):
    d = x2.shape[-1]
    groups, states = lam_re.shape
    ssm_w = groups * SSM_GROUP
    n_heads = b_forget.shape[0]
    fox_w = n_heads * FOX_HEAD_DIM
    o_q = ssm_w
    o_k = o_q + fox_w
    o_v = o_k + fox_w
    o_f = o_v + fox_w
    o_g = o_f + n_heads

    a_re, a_im, s_re, s_im, bb_re, bb_im = _ssm_prep(lam_re, lam_im, log_dt, b_re, b_im)
    nstate = groups * states
    per = LANES // SSM_GROUP
    to_rows = lambda a: a.reshape(1, nstate)
    bd_in = lambda bb: _block_diag(
        bb.reshape(SSM_GROUP, groups, states).transpose(1, 0, 2), per).astype(BF16)
    bd_out = lambda c: _block_diag(c.transpose(0, 2, 1), per).astype(BF16)
    perm = _seg_perm(SSM_CHUNK)

    wf = jnp.zeros((d, LANES), BF16).at[:, :n_heads].set(w_in[:, o_f:o_g].astype(BF16))
    bf = jnp.zeros((1, LANES), F32).at[0, :n_heads].set(b_forget)
    u, q, k, v, cum_t = _in_proj(x2, norm_mix, w_in[:, :o_f].astype(BF16), wf, bf,
                                 ssm_w, fox_w, n_heads, bsz, seq)

    out_a = _ssm(u, perm, perm.T, bd_in(bb_re), bd_in(bb_im), to_rows(a_re), to_rows(a_im),
                 to_rows(s_re), to_rows(s_im), bd_out(c_re), bd_out(c_im),
                 d_skip.reshape(1, ssm_w), w_glu.astype(BF16), bsz, seq)

    att = _fox(q, k, v, cum_t.reshape(bsz, n_heads, seq // FOX_TK, FOX_TK), bsz, seq)

    k_m, v_m = _mem_kv(mem, norm_mem_kv, w_mem_kv)
    bf16 = lambda w: w.astype(BF16)
    return _post(x2, out_a, att, norm_mix, bf16(w_in[:, o_g:]), bf16(w_fox_o), bf16(w_mix_out),
                 norm_mem_q, bf16(w_mem_q), k_m, v_m, bf16(w_mem_o), norm_ffn, bf16(w_ffn_in),
                 bf16(w_ffn_out), norm_final, bsz, seq)


def kernel(x, mem, norm_mix, w_in, b_forget, lam_re, lam_im, log_dt, b_re, b_im, c_re, c_im, d_skip,
           w_glu, w_fox_o, w_mix_out, norm_mem_q, norm_mem_kv, w_mem_q, w_mem_kv, w_mem_o, norm_ffn,
           w_ffn_in, w_ffn_out, norm_final):
    bsz, seq, d = x.shape
    assert w_in.shape[0] == 1, "single-layer block"
    out = _layer(x.reshape(bsz * seq, d), mem, bsz, seq, norm_mix[0], w_in[0], b_forget[0],
                 lam_re[0], lam_im[0], log_dt[0], b_re[0], b_im[0], c_re[0], c_im[0], d_skip[0],
                 w_glu[0], w_fox_o[0], w_mix_out[0], norm_mem_q[0], norm_mem_kv[0], w_mem_q[0],
                 w_mem_kv[0], w_mem_o[0], norm_ffn[0], w_ffn_in[0], w_ffn_out[0], norm_final)
    return out.reshape(bsz, seq, d)
```

```python
import functools
import math

import jax
import jax.numpy as jnp
from jax import lax
from jax.experimental import pallas as pl
from jax.experimental.pallas import tpu as pltpu

F32 = jnp.float32
BF16 = jnp.bfloat16

RMS_EPS = 1e-6
LOG2E = math.log2(math.e)
SSM_GROUP = 16
SSM_STATE = 64
FOX_HEAD_DIM = 64
MEM_HEADS = 4
LANES = 128
SUBLANES = 8
VMEM_LIMIT = 56 * 1024 * 1024

TM_PROJ = 512
TM_POST = 512
FFN_CHUNKS = 2
SSM_CHUNK = 256
SSM_SEG = SSM_CHUNK // SUBLANES
SSM_COLS = 512
FOX_TQ = 1024
FOX_TK = 512
FOX_HEADS_PER_STEP = 4
FOX_SKIP_LOG2 = 160.0
FOX_NORM_SLACK = 1.02


def _cparams(sem):
    return pltpu.CompilerParams(dimension_semantics=sem, vmem_limit_bytes=VMEM_LIMIT)


def _rms(x, gain):
    return x * lax.rsqrt(jnp.mean(x * x, axis=-1, keepdims=True) + RMS_EPS) * gain


def _dot(a, b):
    return jnp.dot(a, b, preferred_element_type=F32)


def _dot_nt(a, b):
    return lax.dot_general(a, b, (((1,), (1,)), ((), ())), preferred_element_type=F32)


def _full(shape):
    n = len(shape)
    return pl.BlockSpec(shape, lambda *_: (0,) * n)


def _ssm_prep_kernel(lre_ref, lim_ref, ldt_ref, bre_ref, bim_ref,
                     are_ref, aim_ref, sre_ref, sim_ref, bbre_ref, bbim_ref):
    lre = lre_ref[...]
    lim = lim_ref[...]
    dt = jnp.exp(ldt_ref[...])
    zr = lre * dt
    zi = lim * dt
    mag = jnp.exp(zr)
    are = mag * jnp.cos(zi)
    aim = mag * jnp.sin(zi)
    are_ref[...] = are
    aim_ref[...] = aim
    mag_s = jnp.exp(zr * SSM_SEG)
    sre_ref[...] = mag_s * jnp.cos(zi * SSM_SEG)
    sim_ref[...] = mag_s * jnp.sin(zi * SSM_SEG)
    nr = are - 1.0
    ni = aim
    den = lre * lre + lim * lim
    fr = (nr * lre + ni * lim) / den
    fi = (ni * lre - nr * lim) / den
    bre = bre_ref[...]
    bim = bim_ref[...]
    bbre_ref[...] = fr * bre - fi * bim
    bbim_ref[...] = fr * bim + fi * bre


def _ssm_prep(lam_re, lam_im, log_dt, b_re, b_im):
    g, p = lam_re.shape
    n = b_re.shape[-1]
    c = g * p
    row = lambda a: a.reshape(1, c)
    chan = lambda b: b.transpose(2, 0, 1).reshape(n, c)
    ldt = jnp.broadcast_to(log_dt[:, None], (g, p))
    outs = pl.pallas_call(
        _ssm_prep_kernel,
        out_shape=[jax.ShapeDtypeStruct((1, c), F32)] * 4 + [jax.ShapeDtypeStruct((n, c), F32)] * 2,
        name="ssm_prep",
    )(row(lam_re), row(lam_im), row(ldt), chan(b_re), chan(b_im))
    return outs


def _mem_kv_kernel(m_ref, g_ref, w_ref, k_ref, v_ref):
    n = _rms(m_ref[0], g_ref[...]).astype(BF16)
    kv = _dot(n, w_ref[...])
    half = kv.shape[-1] // 2
    k_ref[0] = kv[:, :half].astype(BF16)
    v_ref[0] = kv[:, half:].astype(BF16)


def _mem_kv(mem, gain, w_kv):
    b, m, d = mem.shape
    w2 = w_kv.shape[-1]
    return pl.pallas_call(
        _mem_kv_kernel,
        grid=(b,),
        in_specs=[pl.BlockSpec((1, m, d), lambda i: (i, 0, 0)), _full((1, d)), _full((d, w2))],
        out_specs=[pl.BlockSpec((1, m, w2 // 2), lambda i: (i, 0, 0))] * 2,
        out_shape=[jax.ShapeDtypeStruct((b, m, w2 // 2), BF16)] * 2,
        compiler_params=_cparams(("arbitrary",)),
        name="mem_kv",
    )(mem, gain.reshape(1, d), w_kv.astype(BF16))


def _in_proj_kernel(x_ref, g_ref, w_ref, wf_ref, bf_ref,
                    u_ref, q_ref, k_ref, v_ref, cum_ref, carry_ref):
    @pl.when(pl.program_id(1) == 0)
    def _():
        carry_ref[...] = jnp.zeros_like(carry_ref)

    un = _rms(x_ref[...], g_ref[...]).astype(BF16)
    wu = u_ref.shape[1]
    wh = q_ref.shape[1]
    col = lambda lo, hi: _dot(un, w_ref[:, lo:hi])
    u_ref[...] = col(0, wu)
    q_ref[...] = (col(wu, wu + wh) * (FOX_HEAD_DIM ** -0.5 * LOG2E)).astype(BF16)
    k_ref[...] = col(wu + wh, wu + 2 * wh).astype(BF16)
    v_ref[...] = col(wu + 2 * wh, wu + 3 * wh).astype(BF16)
    c = jax.nn.log_sigmoid(_dot(un, wf_ref[...]) + bf_ref[...])
    rows = c.shape[0]
    row = lax.broadcasted_iota(jnp.int32, c.shape, 0)
    shift = 1
    while shift < rows:
        c = c + jnp.where(row >= shift, pltpu.roll(c, shift, 0), 0.0)
        shift *= 2
    c = c + carry_ref[...]
    carry_ref[...] = c[rows - 1:rows, :]
    cum_ref[0] = (c * LOG2E).T[:cum_ref.shape[1], :]


def _in_proj(x2, gain, w_uqkv, wf, bf, ssm_w, fox_w, n_heads, bsz, seq):
    t, d = x2.shape
    tm = TM_PROJ
    nb = seq // tm
    row = lambda w: pl.BlockSpec((tm, w), lambda b, i: (b * nb + i, 0))
    return pl.pallas_call(
        _in_proj_kernel,
        grid=(bsz, nb),
        in_specs=[row(d), _full((1, d)), _full(w_uqkv.shape), _full(wf.shape), _full((1, LANES))],
        out_specs=[row(ssm_w), row(fox_w), row(fox_w), row(fox_w),
                   pl.BlockSpec((1, n_heads, tm), lambda b, i: (b, 0, i))],
        out_shape=[jax.ShapeDtypeStruct((t, ssm_w), F32),
                   jax.ShapeDtypeStruct((t, fox_w), BF16),
                   jax.ShapeDtypeStruct((t, fox_w), BF16),
                   jax.ShapeDtypeStruct((t, fox_w), BF16),
                   jax.ShapeDtypeStruct((bsz, n_heads, seq), F32)],
        scratch_shapes=[pltpu.VMEM((1, LANES), F32)],
        compiler_params=_cparams(("arbitrary", "arbitrary")),
        name="in_proj",
    )(x2, gain.reshape(1, d), w_uqkv, wf, bf)


def _gelu_tanh(x):
    return 0.5 * x * (1.0 + jnp.tanh(math.sqrt(2.0 / math.pi) * (x + 0.044715 * (x * x * x))))


def _ssm_kernel(u_ref, perm_ref, permt_ref, bre_ref, bim_ref, are_ref, aim_ref, sre_ref, sim_ref,
                cre_ref, cim_ref, d_ref, wglu_ref, out_ref, hre, him, car_re, car_im):
    @pl.when(pl.program_id(1) == 0)
    def _():
        car_re[...] = jnp.zeros_like(car_re)
        car_im[...] = jnp.zeros_like(car_im)

    u = u_ref[...]
    up = _dot(perm_ref[...], u.astype(BF16)).astype(BF16)
    nkb = bre_ref.shape[0]
    kw = bre_ref.shape[1]
    nw = bre_ref.shape[2]
    for kb in range(nkb):
        ukb = up[:, kb * kw:(kb + 1) * kw]
        hre[:, kb * nw:(kb + 1) * nw] = _dot(ukb, bre_ref[kb])
        him[:, kb * nw:(kb + 1) * nw] = _dot(ukb, bim_ref[kb])

    nstate = hre.shape[1]
    for cb in range(nstate // SSM_COLS):
        cs = slice(cb * SSM_COLS, (cb + 1) * SSM_COLS)
        a_re = jnp.broadcast_to(are_ref[:, cs], (SUBLANES, SSM_COLS))
        a_im = jnp.broadcast_to(aim_ref[:, cs], (SUBLANES, SSM_COLS))

        e_re = hre[0:SUBLANES, cs]
        e_im = him[0:SUBLANES, cs]
        for k in range(1, SSM_SEG):
            r = slice(k * SUBLANES, (k + 1) * SUBLANES)
            e_re, e_im = (a_re * e_re - a_im * e_im + hre[r, cs],
                          a_re * e_im + a_im * e_re + him[r, cs])
            hre[r, cs] = e_re
            him[r, cs] = e_im

        s_re = sre_ref[:, cs]
        s_im = sim_ref[:, cs]
        c_re = car_re[:, cs]
        c_im = car_im[:, cs]
        rows_re, rows_im = [], []
        for j in range(SUBLANES):
            rows_re.append(c_re)
            rows_im.append(c_im)
            n_re = s_re * c_re - s_im * c_im + e_re[j:j + 1]
            n_im = s_re * c_im + s_im * c_re + e_im[j:j + 1]
            c_re, c_im = n_re, n_im
        car_re[:, cs] = c_re
        car_im[:, cs] = c_im

        d_re = jnp.concatenate(rows_re, axis=0)
        d_im = jnp.concatenate(rows_im, axis=0)
        for k in range(SSM_SEG):
            r = slice(k * SUBLANES, (k + 1) * SUBLANES)
            d_re, d_im = a_re * d_re - a_im * d_im, a_re * d_im + a_im * d_re
            hre[r, cs] = hre[r, cs] + d_re
            him[r, cs] = him[r, cs] + d_im

    ncb = cre_ref.shape[0]
    cw = cre_ref.shape[1]
    ys = []
    for kb in range(ncb):
        h_re = hre[:, kb * cw:(kb + 1) * cw].astype(BF16)
        h_im = him[:, kb * cw:(kb + 1) * cw].astype(BF16)
        ys.append(_dot(h_re, cre_ref[kb]) - _dot(h_im, cim_ref[kb]))
    yp = jnp.concatenate(ys, axis=-1)
    hi = yp.astype(BF16)
    lo = (yp - hi.astype(F32)).astype(BF16)
    y = _dot(permt_ref[...], hi) + _dot(permt_ref[...], lo)
    y = y + d_ref[...] * u
    z = _dot(_gelu_tanh(y).astype(BF16), wglu_ref[...])
    half = z.shape[-1] // 2
    out_ref[...] = z[:, :half] * jax.nn.sigmoid(z[:, half:])


def _ssm(u, perm, permt, bre, bim, a_re, a_im, s_re, s_im, cre, cim, d_skip, w_glu, bsz, seq):
    t, w = u.shape
    q = SSM_CHUNK
    nb = seq // q
    nstate = a_re.shape[1]
    dm = w_glu.shape[1] // 2
    row = lambda width: pl.BlockSpec((q, width), lambda b, i: (b * nb + i, 0))
    consts = [perm, permt, bre, bim, a_re, a_im, s_re, s_im, cre, cim, d_skip, w_glu]
    return pl.pallas_call(
        _ssm_kernel,
        grid=(bsz, nb),
        in_specs=[row(w)] + [_full(c.shape) for c in consts],
        out_specs=row(dm),
        out_shape=jax.ShapeDtypeStruct((t, dm), F32),
        scratch_shapes=[pltpu.VMEM((q, nstate), F32), pltpu.VMEM((q, nstate), F32),
                        pltpu.VMEM((1, nstate), F32), pltpu.VMEM((1, nstate), F32)],
        compiler_params=_cparams(("arbitrary", "arbitrary")),
        name="ssm",
    )(u, *consts)


def _fox_kernel(q_ref, k_ref, v_ref, ck_ref, o_ref, m_scr, acc_scr, first_blk):
    seq = q_ref.shape[0]
    tq = FOX_TQ
    tk = FOX_TK
    dh = FOX_HEAD_DIM
    n_heads = ck_ref.shape[1]
    lane = lax.broadcasted_iota(jnp.int32, (1, LANES), 1)
    keep = [jnp.where(lane < dh, 1.0, 0.0).astype(BF16), jnp.where(lane >= dh, 1.0, 0.0).astype(BF16)]
    sum_lane = [dh, 0]
    ones = [jnp.where(lane == sl, 1.0, 0.0).astype(BF16) for sl in sum_lane]

    def block(q, j, rows, h, masked):
        e = h % 2
        ps = slice((h // 2) * LANES, (h // 2 + 1) * LANES)
        hs = slice(h * LANES, (h + 1) * LANES)
        ks = pl.ds(pl.multiple_of(j * tk, tk), tk)
        s = _dot_nt(q, k_ref[ks, ps] * keep[e]) - ck_ref[0, h, pl.ds(j, 1), :]
        if masked:
            r = lax.broadcasted_iota(jnp.int32, s.shape, 0)
            c = lax.broadcasted_iota(jnp.int32, s.shape, 1)
            s = jnp.where(c <= r, s, -jnp.inf)
        m_prev = m_scr[rows, hs]
        m_new = jnp.maximum(m_prev, jnp.max(s, axis=1, keepdims=True))
        p = jnp.exp2(s - jnp.concatenate([m_new] * (tk // LANES), axis=1))
        acc_scr[rows, hs] = (jnp.exp2(m_prev - m_new) * acc_scr[rows, hs]
                             + _dot(p.astype(BF16), v_ref[ks, ps] * keep[e] + ones[e]))
        m_scr[rows, hs] = m_new

    nsub = tq // tk
    nq = seq // tq
    nk = seq // tk
    all_rows = slice(0, tq)

    col = lax.broadcasted_iota(jnp.int32, (q_ref.shape[1], LANES), 0) // dh
    head_sel = jnp.where(col == lax.broadcasted_iota(jnp.int32, (q_ref.shape[1], LANES), 1),
                         1.0, 0.0).astype(BF16)

    def max_sq_norm(ref, lo, n):
        x = ref[lo:lo + n, :].astype(F32)
        sq = _dot((x * x).astype(BF16), head_sel)
        return jnp.max(sq, axis=0, keepdims=True) * FOX_NORM_SLACK

    k_sq = jnp.concatenate([max_sq_norm(k_ref, j * tk, tk) for j in range(nk)], axis=0)
    q_sq = jnp.concatenate([max_sq_norm(q_ref, i * tq, tq) for i in range(nq)], axis=0)
    blk_id = lax.broadcasted_iota(jnp.int32, (nk, 1), 0).astype(F32)
    for h in range(n_heads):
        k_max = jnp.sqrt(k_sq[:, h:h + 1])
        c_end = ck_ref[0, h, :, tk - 1:tk]
        for i in range(nq):
            first_diag = float(i * nsub)
            q_max = jnp.sqrt(q_sq[i:i + 1, h:h + 1])
            k_own = jnp.max(k_max[i * nsub:(i + 1) * nsub], axis=0, keepdims=True)
            c_start = ck_ref[0, h, i * nsub:i * nsub + 1, 0:1]
            bound = q_max * (k_max + k_own) + (c_start - c_end)
            visit = jnp.logical_and(bound >= -FOX_SKIP_LOG2, blk_id < first_diag)
            first_blk[h * nq + i] = jnp.min(jnp.where(visit, blk_id, first_diag)).astype(jnp.int32)

    def qblock(qi, _):
        qs = pl.ds(pl.multiple_of(qi * tq, tq), tq)
        qp = [q_ref[qs, p * LANES:(p + 1) * LANES] for p in range(n_heads // 2)]
        m_scr[...] = jnp.full(m_scr.shape, -jnp.inf, F32)
        acc_scr[...] = jnp.zeros(acc_scr.shape, F32)

        def full(j, _):
            for h in range(n_heads):
                block(qp[h // 2], j, all_rows, h, False)
            return 0

        starts = [first_blk[h * nq + qi] for h in range(n_heads)]
        common = functools.reduce(jnp.maximum, starts)
        for h in range(n_heads):
            def one(j, _, h=h):
                block(qp[h // 2], j, all_rows, h, False)
                return 0
            lax.fori_loop(starts[h], common, one, 0)
        lax.fori_loop(common, qi * nsub, full, 0)
        for r in range(nsub):
            rows = slice(r * tk, (r + 1) * tk)
            for h in range(n_heads):
                for jj in range(r):
                    block(qp[h // 2][rows], qi * nsub + jj, rows, h, False)
                block(qp[h // 2][rows], qi * nsub + r, rows, h, True)
            for p in range(n_heads // 2):
                a0 = acc_scr[rows, (2 * p) * LANES:(2 * p + 1) * LANES]
                a1 = acc_scr[rows, (2 * p + 1) * LANES:(2 * p + 2) * LANES]
                out = jnp.where(lane < dh, a0 / a0[:, sum_lane[0]:sum_lane[0] + 1],
                                a1 / a1[:, sum_lane[1]:sum_lane[1] + 1])
                o_ref[pl.ds(pl.multiple_of(qi * tq + r * tk, tk), tk),
                      p * LANES:(p + 1) * LANES] = out.astype(o_ref.dtype)
        return 0

    lax.fori_loop(0, seq // tq, qblock, 0)


def _fox(q, k, v, cum_t, bsz, seq):
    t, w = q.shape
    assert FOX_TQ % FOX_TK == 0 and seq % FOX_TQ == 0 and FOX_HEADS_PER_STEP % 2 == 0
    hps = FOX_HEADS_PER_STEP
    wblk = hps * FOX_HEAD_DIM
    blk = pl.BlockSpec((seq, wblk), lambda bi, hi: (bi, hi))
    return pl.pallas_call(
        _fox_kernel,
        grid=(bsz, w // wblk),
        in_specs=[blk, blk, blk,
                  pl.BlockSpec((1, hps) + cum_t.shape[2:], lambda bi, hi: (bi, hi, 0, 0))],
        out_specs=blk,
        out_shape=jax.ShapeDtypeStruct((t, w), BF16),
        scratch_shapes=[pltpu.VMEM((FOX_TQ, hps * LANES), F32),
                        pltpu.VMEM((FOX_TQ, hps * LANES), F32),
                        pltpu.SMEM((hps * (seq // FOX_TQ),), jnp.int32)],
        compiler_params=_cparams(("arbitrary", "arbitrary")),
        name="fox",
    )(q, k, v, cum_t)


def _post_kernel(x_ref, oa_ref, att_ref, gmix_ref, wg_ref, wfo_ref, wmix_ref, gq_ref, wq_ref,
                 km_ref, vm_ref, wo_ref, gf_ref, win_ref, wout_ref, gfin_ref, o_ref):
    d = x_ref.shape[-1]
    x = x_ref[...]
    gate = jax.nn.sigmoid(_dot(_rms(x, gmix_ref[...]).astype(BF16), wg_ref[...]))
    out_b = _dot(att_ref[...], wfo_ref[...])
    mix = gate[:, :d] * oa_ref[...] + gate[:, d:] * out_b
    h1 = x + _dot(mix.astype(BF16), wmix_ref[...])

    n = _rms(h1, gq_ref[...]).astype(BF16)
    qm = _dot(n, wq_ref[...])
    hd = qm.shape[-1] // MEM_HEADS
    qm = (qm * (hd ** -0.5)).astype(BF16)
    outs = []
    for hh in range(MEM_HEADS):
        hs = slice(hh * hd, (hh + 1) * hd)
        s = _dot_nt(qm[:, hs], km_ref[0, :, hs])
        s = s - jnp.max(s, axis=-1, keepdims=True)
        p = jnp.exp(s)
        p = p / jnp.sum(p, axis=-1, keepdims=True)
        outs.append(_dot(p.astype(BF16), vm_ref[0, :, hs]))
    o = jnp.concatenate(outs, axis=-1).astype(BF16)
    h2 = h1 + _dot(o, wo_ref[...])

    f = _rms(h2, gf_ref[...]).astype(BF16)
    hidden = wout_ref.shape[0]
    hc = hidden // FFN_CHUNKS
    acc = jnp.zeros_like(h2)
    for c in range(FFN_CHUNKS):
        fa = _dot(f, win_ref[:, c * hc:(c + 1) * hc])
        fb = _dot(f, win_ref[:, hidden + c * hc:hidden + (c + 1) * hc])
        g = (fa * jax.nn.sigmoid(fa) * fb).astype(BF16)
        acc = acc + _dot(g, wout_ref[c * hc:(c + 1) * hc, :])
    o_ref[...] = _rms(h2 + acc, gfin_ref[...])


def _resident(shape):
    n = len(shape)
    return pl.BlockSpec(shape, lambda *_: (0,) * n, pipeline_mode=pl.Buffered(1))


def _post(x2, out_a, att, gain_mix, w_gate, w_fox_o, w_mix, gain_q, w_q, k_m, v_m, w_o,
          gain_f, w_ffn_in, w_ffn_out, gain_fin, bsz, seq):
    t, d = x2.shape
    tm = TM_POST
    nb = seq // tm
    assert (w_ffn_out.shape[0] // FFN_CHUNKS) % LANES == 0
    row = lambda w: pl.BlockSpec((tm, w), lambda b, i: (b * nb + i, 0))
    mem = pl.BlockSpec((1,) + k_m.shape[1:], lambda b, i: (b, 0, 0))
    gain = lambda g: g.reshape(1, d)
    weights = lambda *ws: [_resident(w.shape) for w in ws]
    return pl.pallas_call(
        _post_kernel,
        grid=(bsz, nb),
        in_specs=[row(d), row(d), row(att.shape[1]), _full((1, d))]
                 + weights(w_gate, w_fox_o, w_mix) + [_full((1, d))] + weights(w_q)
                 + [mem, mem] + weights(w_o) + [_full((1, d))] + weights(w_ffn_in, w_ffn_out)
                 + [_full((1, d))],
        out_specs=row(d),
        out_shape=jax.ShapeDtypeStruct((t, d), F32),
        compiler_params=_cparams(("arbitrary", "arbitrary")),
        name="post",
    )(x2, out_a, att, gain(gain_mix), w_gate, w_fox_o, w_mix, gain(gain_q), w_q, k_m, v_m, w_o,
      gain(gain_f), w_ffn_in, w_ffn_out, gain(gain_fin))


def _block_diag(blocks, per):
    g, r, c = blocks.shape
    b = blocks.reshape(g // per, per, r, c)
    eye = jnp.eye(per, dtype=blocks.dtype)
    out = b[:, :, :, None, :] * eye[None, :, None, :, None]
    return out.reshape(g // per, per * r, per * c)


def _seg_perm(q):
    seg = q // SUBLANES
    r = jnp.arange(q)
    src = (r % SUBLANES) * seg + r // SUBLANES
    return (src[:, None] == jnp.arange(q)[None, :]).astype(BF16)


def _layer(x2, mem, bsz, seq, norm_mix, w_in, b_forget, lam_re, lam_im, log_dt, b_re, b_im, c_re,
           c_im, d_skip, w_glu, w_fox_o, w_mix_out, norm_mem_q, norm_mem_kv, w_mem_q, w_mem_kv,
           w_mem_o, norm_ffn, w_ffn_in, w_ffn_out, norm_final):
    d = x2.shape[-1]
    groups, states = lam_re.shape
    ssm_w = groups * SSM_GROUP
    n_heads = b_forget.shape[0]
    fox_w = n_heads * FOX_HEAD_DIM
    o_q = ssm_w
    o_k = o_q + fox_w
    o_v = o_k + fox_w
    o_f = o_v + fox_w
    o_g = o_f + n_heads

    a_re, a_im, s_re, s_im, bb_re, bb_im = _ssm_prep(lam_re, lam_im, log_dt, b_re, b_im)
    nstate = groups * states
    per = LANES // SSM_GROUP
    to_rows = lambda a: a.reshape(1, nstate)
    bd_in = lambda bb: _block_diag(
        bb.reshape(SSM_GROUP, groups, states).transpose(1, 0, 2), per).astype(BF16)
    bd_out = lambda c: _block_diag(c.transpose(0, 2, 1), per).astype(BF16)
    perm = _seg_perm(SSM_CHUNK)

    wf = jnp.zeros((d, LANES), BF16).at[:, :n_heads].set(w_in[:, o_f:o_g].astype(BF16))
    bf = jnp.zeros((1, LANES), F32).at[0, :n_heads].set(b_forget)
    u, q, k, v, cum_t = _in_proj(x2, norm_mix, w_in[:, :o_f].astype(BF16), wf, bf,
                                 ssm_w, fox_w, n_heads, bsz, seq)

    out_a = _ssm(u, perm, perm.T, bd_in(bb_re), bd_in(bb_im), to_rows(a_re), to_rows(a_im),
                 to_rows(s_re), to_rows(s_im), bd_out(c_re), bd_out(c_im),
                 d_skip.reshape(1, ssm_w), w_glu.astype(BF16), bsz, seq)

    att = _fox(q, k, v, cum_t.reshape(bsz, n_heads, seq // FOX_TK, FOX_TK), bsz, seq)

    k_m, v_m = _mem_kv(mem, norm_mem_kv, w_mem_kv)
    bf16 = lambda w: w.astype(BF16)
    return _post(x2, out_a, att, norm_mix, bf16(w_in[:, o_g:]), bf16(w_fox_o), bf16(w_mix_out),
                 norm_mem_q, bf16(w_mem_q), k_m, v_m, bf16(w_mem_o), norm_ffn, bf16(w_ffn_in),
                 bf16(w_ffn_out), norm_final, bsz, seq)


def kernel(x, mem, norm_mix, w_in, b_forget, lam_re, lam_im, log_dt, b_re, b_im, c_re, c_im, d_skip,
           w_glu, w_fox_o, w_mix_out, norm_mem_q, norm_mem_kv, w_mem_q, w_mem_kv, w_mem_o, norm_ffn,
           w_ffn_in, w_ffn_out, norm_final):
    bsz, seq, d = x.shape
    assert w_in.shape[0] == 1, "single-layer block"
    out = _layer(x.reshape(bsz * seq, d), mem, bsz, seq, norm_mix[0], w_in[0], b_forget[0],
                 lam_re[0], lam_im[0], log_dt[0], b_re[0], b_im[0], c_re[0], c_im[0], d_skip[0],
                 w_glu[0], w_fox_o[0], w_mix_out[0], norm_mem_q[0], norm_mem_kv[0], w_mem_q[0],
                 w_mem_kv[0], w_mem_o[0], norm_ffn[0], w_ffn_in[0], w_ffn_out[0], norm_final)
    return out.reshape(bsz, seq, d)
```

```python
import functools
import math

import jax
import jax.numpy as jnp
from jax import lax
from jax.experimental import pallas as pl
from jax.experimental.pallas import tpu as pltpu

F32 = jnp.float32
BF16 = jnp.bfloat16

RMS_EPS = 1e-6
LOG2E = math.log2(math.e)
SSM_GROUP = 16
SSM_STATE = 64
FOX_HEAD_DIM = 64
MEM_HEADS = 4
LANES = 128
SUBLANES = 8
VMEM_LIMIT = 56 * 1024 * 1024

TM_PROJ = 512
TM_POST = 512
FFN_CHUNKS = 2
SSM_CHUNK = 256
SSM_SEG = SSM_CHUNK // SUBLANES
SSM_COLS = 512
FOX_TQ = 1024
FOX_TK = 512
FOX_HEADS_PER_STEP = 4
FOX_SKIP_LOG2 = 160.0
FOX_NORM_SLACK = 1.02


def _cparams(sem):
    return pltpu.CompilerParams(dimension_semantics=sem, vmem_limit_bytes=VMEM_LIMIT)


def _rms(x, gain):
    return x * lax.rsqrt(jnp.mean(x * x, axis=-1, keepdims=True) + RMS_EPS) * gain


def _dot(a, b):
    return jnp.dot(a, b, preferred_element_type=F32)


def _dot_nt(a, b):
    return lax.dot_general(a, b, (((1,), (1,)), ((), ())), preferred_element_type=F32)


def _full(shape):
    n = len(shape)
    return pl.BlockSpec(shape, lambda *_: (0,) * n)


def _ssm_prep_kernel(lre_ref, lim_ref, ldt_ref, bre_ref, bim_ref,
                     are_ref, aim_ref, sre_ref, sim_ref, bbre_ref, bbim_ref):
    lre = lre_ref[...]
    lim = lim_ref[...]
    dt = jnp.exp(ldt_ref[...])
    zr = lre * dt
    zi = lim * dt
    mag = jnp.exp(zr)
    are = mag * jnp.cos(zi)
    aim = mag * jnp.sin(zi)
    are_ref[...] = are
    aim_ref[...] = aim
    mag_s = jnp.exp(zr * SSM_SEG)
    sre_ref[...] = mag_s * jnp.cos(zi * SSM_SEG)
    sim_ref[...] = mag_s * jnp.sin(zi * SSM_SEG)
    nr = are - 1.0
    ni = aim
    den = lre * lre + lim * lim
    fr = (nr * lre + ni * lim) / den
    fi = (ni * lre - nr * lim) / den
    bre = bre_ref[...]
    bim = bim_ref[...]
    bbre_ref[...] = fr * bre - fi * bim
    bbim_ref[...] = fr * bim + fi * bre


def _ssm_prep(lam_re, lam_im, log_dt, b_re, b_im):
    g, p = lam_re.shape
    n = b_re.shape[-1]
    c = g * p
    row = lambda a: a.reshape(1, c)
    chan = lambda b: b.transpose(2, 0, 1).reshape(n, c)
    ldt = jnp.broadcast_to(log_dt[:, None], (g, p))
    outs = pl.pallas_call(
        _ssm_prep_kernel,
        out_shape=[jax.ShapeDtypeStruct((1, c), F32)] * 4 + [jax.ShapeDtypeStruct((n, c), F32)] * 2,
        name="ssm_prep",
    )(row(lam_re), row(lam_im), row(ldt), chan(b_re), chan(b_im))
    return outs


def _mem_kv_kernel(m_ref, g_ref, w_ref, k_ref, v_ref):
    n = _rms(m_ref[0], g_ref[...]).astype(BF16)
    kv = _dot(n, w_ref[...])
    half = kv.shape[-1] // 2
    k_ref[0] = kv[:, :half].astype(BF16)
    v_ref[0] = kv[:, half:].astype(BF16)


def _mem_kv(mem, gain, w_kv):
    b, m, d = mem.shape
    w2 = w_kv.shape[-1]
    return pl.pallas_call(
        _mem_kv_kernel,
        grid=(b,),
        in_specs=[pl.BlockSpec((1, m, d), lambda i: (i, 0, 0)), _full((1, d)), _full((d, w2))],
        out_specs=[pl.BlockSpec((1, m, w2 // 2), lambda i: (i, 0, 0))] * 2,
        out_shape=[jax.ShapeDtypeStruct((b, m, w2 // 2), BF16)] * 2,
        compiler_params=_cparams(("arbitrary",)),
        name="mem_kv",
    )(mem, gain.reshape(1, d), w_kv.astype(BF16))


def _in_proj_kernel(x_ref, g_ref, w_ref, wf_ref, bf_ref,
                    u_ref, q_ref, k_ref, v_ref, cum_ref, carry_ref):
    @pl.when(pl.program_id(1) == 0)
    def _():
        carry_ref[...] = jnp.zeros_like(carry_ref)

    un = _rms(x_ref[...], g_ref[...]).astype(BF16)
    wu = u_ref.shape[1]
    wh = q_ref.shape[1]
    col = lambda lo, hi: _dot(un, w_ref[:, lo:hi])
    u_ref[...] = col(0, wu)
    q_ref[...] = (col(wu, wu + wh) * (FOX_HEAD_DIM ** -0.5 * LOG2E)).astype(BF16)
    k_ref[...] = col(wu + wh, wu + 2 * wh).astype(BF16)
    v_ref[...] = col(wu + 2 * wh, wu + 3 * wh).astype(BF16)
    c = jax.nn.log_sigmoid(_dot(un, wf_ref[...]) + bf_ref[...])
    rows = c.shape[0]
    row = lax.broadcasted_iota(jnp.int32, c.shape, 0)
    shift = 1
    while shift < rows:
        c = c + jnp.where(row >= shift, pltpu.roll(c, shift, 0), 0.0)
        shift *= 2
    c = c + carry_ref[...]
    carry_ref[...] = c[rows - 1:rows, :]
    cum_ref[0] = (c * LOG2E).T[:cum_ref.shape[1], :]


def _in_proj(x2, gain, w_uqkv, wf, bf, ssm_w, fox_w, n_heads, bsz, seq):
    t, d = x2.shape
    tm = TM_PROJ
    nb = seq // tm
    row = lambda w: pl.BlockSpec((tm, w), lambda b, i: (b * nb + i, 0))
    return pl.pallas_call(
        _in_proj_kernel,
        grid=(bsz, nb),
        in_specs=[row(d), _full((1, d)), _full(w_uqkv.shape), _full(wf.shape), _full((1, LANES))],
        out_specs=[row(ssm_w), row(fox_w), row(fox_w), row(fox_w),
                   pl.BlockSpec((1, n_heads, tm), lambda b, i: (b, 0, i))],
        out_shape=[jax.ShapeDtypeStruct((t, ssm_w), F32),
                   jax.ShapeDtypeStruct((t, fox_w), BF16),
                   jax.ShapeDtypeStruct((t, fox_w), BF16),
                   jax.ShapeDtypeStruct((t, fox_w), BF16),
                   jax.ShapeDtypeStruct((bsz, n_heads, seq), F32)],
        scratch_shapes=[pltpu.VMEM((1, LANES), F32)],
        compiler_params=_cparams(("arbitrary", "arbitrary")),
        name="in_proj",
    )(x2, gain.reshape(1, d), w_uqkv, wf, bf)


def _gelu_tanh(x):
    return 0.5 * x * (1.0 + jnp.tanh(math.sqrt(2.0 / math.pi) * (x + 0.044715 * (x * x * x))))


def _ssm_kernel(u_ref, perm_ref, permt_ref, bre_ref, bim_ref, are_ref, aim_ref, sre_ref, sim_ref,
                cre_ref, cim_ref, d_ref, wglu_ref, out_ref, hre, him, car_re, car_im):
    @pl.when(pl.program_id(1) == 0)
    def _():
        car_re[...] = jnp.zeros_like(car_re)
        car_im[...] = jnp.zeros_like(car_im)

    u = u_ref[...]
    up = _dot(perm_ref[...], u.astype(BF16)).astype(BF16)
    nkb = bre_ref.shape[0]
    kw = bre_ref.shape[1]
    nw = bre_ref.shape[2]
    for kb in range(nkb):
        ukb = up[:, kb * kw:(kb + 1) * kw]
        hre[:, kb * nw:(kb + 1) * nw] = _dot(ukb, bre_ref[kb])
        him[:, kb * nw:(kb + 1) * nw] = _dot(ukb, bim_ref[kb])

    nstate = hre.shape[1]
    for cb in range(nstate // SSM_COLS):
        cs = slice(cb * SSM_COLS, (cb + 1) * SSM_COLS)
        a_re = jnp.broadcast_to(are_ref[:, cs], (SUBLANES, SSM_COLS))
        a_im = jnp.broadcast_to(aim_ref[:, cs], (SUBLANES, SSM_COLS))

        e_re = hre[0:SUBLANES, cs]
        e_im = him[0:SUBLANES, cs]
        for k in range(1, SSM_SEG):
            r = slice(k * SUBLANES, (k + 1) * SUBLANES)
            e_re, e_im = (a_re * e_re - a_im * e_im + hre[r, cs],
                          a_re * e_im + a_im * e_re + him[r, cs])
            hre[r, cs] = e_re
            him[r, cs] = e_im

        s_re = sre_ref[:, cs]
        s_im = sim_ref[:, cs]
        c_re = car_re[:, cs]
        c_im = car_im[:, cs]
        rows_re, rows_im = [], []
        for j in range(SUBLANES):
            rows_re.append(c_re)
            rows_im.append(c_im)
            n_re = s_re * c_re - s_im * c_im + e_re[j:j + 1]
            n_im = s_re * c_im + s_im * c_re + e_im[j:j + 1]
            c_re, c_im = n_re, n_im
        car_re[:, cs] = c_re
        car_im[:, cs] = c_im

        d_re = jnp.concatenate(rows_re, axis=0)
        d_im = jnp.concatenate(rows_im, axis=0)
        for k in range(SSM_SEG):
            r = slice(k * SUBLANES, (k + 1) * SUBLANES)
            d_re, d_im = a_re * d_re - a_im * d_im, a_re * d_im + a_im * d_re
            hre[r, cs] = hre[r, cs] + d_re
            him[r, cs] = him[r, cs] + d_im

    ncb = cre_ref.shape[0]
    cw = cre_ref.shape[1]
    ys = []
    for kb in range(ncb):
        h_re = hre[:, kb * cw:(kb + 1) * cw].astype(BF16)
        h_im = him[:, kb * cw:(kb + 1) * cw].astype(BF16)
        ys.append(_dot(h_re, cre_ref[kb]) - _dot(h_im, cim_ref[kb]))
    yp = jnp.concatenate(ys, axis=-1)
    hi = yp.astype(BF16)
    lo = (yp - hi.astype(F32)).astype(BF16)
    y = _dot(permt_ref[...], hi) + _dot(permt_ref[...], lo)
    y = y + d_ref[...] * u
    z = _dot(_gelu_tanh(y).astype(BF16), wglu_ref[...])
    half = z.shape[-1] // 2
    out_ref[...] = z[:, :half] * jax.nn.sigmoid(z[:, half:])


def _ssm(u, perm, permt, bre, bim, a_re, a_im, s_re, s_im, cre, cim, d_skip, w_glu, bsz, seq):
    t, w = u.shape
    q = SSM_CHUNK
    nb = seq // q
    nstate = a_re.shape[1]
    dm = w_glu.shape[1] // 2
    row = lambda width: pl.BlockSpec((q, width), lambda b, i: (b * nb + i, 0))
    consts = [perm, permt, bre, bim, a_re, a_im, s_re, s_im, cre, cim, d_skip, w_glu]
    return pl.pallas_call(
        _ssm_kernel,
        grid=(bsz, nb),
        in_specs=[row(w)] + [_full(c.shape) for c in consts],
        out_specs=row(dm),
        out_shape=jax.ShapeDtypeStruct((t, dm), F32),
        scratch_shapes=[pltpu.VMEM((q, nstate), F32), pltpu.VMEM((q, nstate), F32),
                        pltpu.VMEM((1, nstate), F32), pltpu.VMEM((1, nstate), F32)],
        compiler_params=_cparams(("arbitrary", "arbitrary")),
        name="ssm",
    )(u, *consts)


def _fox_kernel(q_ref, k_ref, v_ref, ck_ref, o_ref, m_scr, acc_scr, first_blk):
    seq = q_ref.shape[0]
    tq = FOX_TQ
    tk = FOX_TK
    dh = FOX_HEAD_DIM
    n_heads = ck_ref.shape[1]
    lane = lax.broadcasted_iota(jnp.int32, (1, LANES), 1)
    keep = [jnp.where(lane < dh, 1.0, 0.0).astype(BF16), jnp.where(lane >= dh, 1.0, 0.0).astype(BF16)]
    sum_lane = [dh, 0]
    ones = [jnp.where(lane == sl, 1.0, 0.0).astype(BF16) for sl in sum_lane]

    def block(q, j, rows, h, masked):
        e = h % 2
        ps = slice((h // 2) * LANES, (h // 2 + 1) * LANES)
        hs = slice(h * LANES, (h + 1) * LANES)
        ks = pl.ds(pl.multiple_of(j * tk, tk), tk)
        s = _dot_nt(q, k_ref[ks, ps] * keep[e]) - ck_ref[0, h, pl.ds(j, 1), :]
        if masked:
            r = lax.broadcasted_iota(jnp.int32, (tk, tk), 0)
            c = lax.broadcasted_iota(jnp.int32, (tk, tk), 1)
            top = jnp.where(c <= r, s[:tk], -jnp.inf)
            s = top if s.shape[0] == tk else jnp.concatenate([top, s[tk:]], axis=0)
        m_prev = m_scr[rows, hs]
        m_new = jnp.maximum(m_prev, jnp.max(s, axis=1, keepdims=True))
        p = jnp.exp2(s - jnp.concatenate([m_new] * (tk // LANES), axis=1))
        acc_scr[rows, hs] = (jnp.exp2(m_prev - m_new) * acc_scr[rows, hs]
                             + _dot(p.astype(BF16), v_ref[ks, ps] * keep[e] + ones[e]))
        m_scr[rows, hs] = m_new

    nsub = tq // tk
    nq = seq // tq
    nk = seq // tk
    all_rows = slice(0, tq)

    col = lax.broadcasted_iota(jnp.int32, (q_ref.shape[1], LANES), 0) // dh
    head_sel = jnp.where(col == lax.broadcasted_iota(jnp.int32, (q_ref.shape[1], LANES), 1),
                         1.0, 0.0).astype(BF16)

    def max_sq_norm(ref, lo, n):
        x = ref[lo:lo + n, :]
        sq = _dot(x * x, head_sel)
        return jnp.max(sq, axis=0, keepdims=True) * FOX_NORM_SLACK

    k_sq = jnp.concatenate([max_sq_norm(k_ref, j * tk, tk) for j in range(nk)], axis=0)
    q_sq = jnp.concatenate([max_sq_norm(q_ref, i * tq, tq) for i in range(nq)], axis=0)
    blk_id = lax.broadcasted_iota(jnp.int32, (nk, 1), 0).astype(F32)
    for h in range(n_heads):
        k_max = jnp.sqrt(k_sq[:, h:h + 1])
        c_end = ck_ref[0, h, :, tk - 1:tk]
        for i in range(nq):
            first_diag = float(i * nsub)
            q_max = jnp.sqrt(q_sq[i:i + 1, h:h + 1])
            k_own = jnp.max(k_max[i * nsub:(i + 1) * nsub], axis=0, keepdims=True)
            c_start = ck_ref[0, h, i * nsub:i * nsub + 1, 0:1]
            bound = q_max * (k_max + k_own) + (c_start - c_end)
            visit = jnp.logical_and(bound >= -FOX_SKIP_LOG2, blk_id < first_diag)
            first_blk[h * nq + i] = jnp.min(jnp.where(visit, blk_id, first_diag)).astype(jnp.int32)

    def qblock(qi, _):
        qs = pl.ds(pl.multiple_of(qi * tq, tq), tq)
        qp = [q_ref[qs, p * LANES:(p + 1) * LANES] for p in range(n_heads // 2)]
        m_scr[...] = jnp.full(m_scr.shape, -jnp.inf, F32)
        acc_scr[...] = jnp.zeros(acc_scr.shape, F32)

        def full(j, _):
            for h in range(n_heads):
                block(qp[h // 2], j, all_rows, h, False)
            return 0

        starts = [first_blk[h * nq + qi] for h in range(n_heads)]
        common = functools.reduce(jnp.maximum, starts)
        for h in range(n_heads):
            def one(j, _, h=h):
                block(qp[h // 2], j, all_rows, h, False)
                return 0
            lax.fori_loop(starts[h], common, one, 0)
        lax.fori_loop(common, qi * nsub, full, 0)
        for r in range(nsub):
            rows = slice(r * tk, tq)
            for h in range(n_heads):
                block(qp[h // 2][rows], qi * nsub + r, rows, h, True)
        for p in range(n_heads // 2):
            a0 = acc_scr[:, (2 * p) * LANES:(2 * p + 1) * LANES]
            a1 = acc_scr[:, (2 * p + 1) * LANES:(2 * p + 2) * LANES]
            out = jnp.where(lane < dh, a0 / a0[:, sum_lane[0]:sum_lane[0] + 1],
                            a1 / a1[:, sum_lane[1]:sum_lane[1] + 1])
            o_ref[qs, p * LANES:(p + 1) * LANES] = out.astype(o_ref.dtype)
        return 0

    lax.fori_loop(0, seq // tq, qblock, 0)


def _fox(q, k, v, cum_t, bsz, seq):
    t, w = q.shape
    assert FOX_TQ % FOX_TK == 0 and seq % FOX_TQ == 0 and FOX_HEADS_PER_STEP % 2 == 0
    hps = FOX_HEADS_PER_STEP
    wblk = hps * FOX_HEAD_DIM
    blk = pl.BlockSpec((seq, wblk), lambda bi, hi: (bi, hi))
    return pl.pallas_call(
        _fox_kernel,
        grid=(bsz, w // wblk),
        in_specs=[blk, blk, blk,
                  pl.BlockSpec((1, hps) + cum_t.shape[2:], lambda bi, hi: (bi, hi, 0, 0))],
        out_specs=blk,
        out_shape=jax.ShapeDtypeStruct((t, w), BF16),
        scratch_shapes=[pltpu.VMEM((FOX_TQ, hps * LANES), F32),
                        pltpu.VMEM((FOX_TQ, hps * LANES), F32),
                        pltpu.SMEM((hps * (seq // FOX_TQ),), jnp.int32)],
        compiler_params=_cparams(("arbitrary", "arbitrary")),
        name="fox",
    )(q, k, v, cum_t)


def _post_kernel(x_ref, oa_ref, att_ref, gmix_ref, wg_ref, wfo_ref, wmix_ref, gq_ref, wq_ref,
                 km_ref, vm_ref, wo_ref, gf_ref, win_ref, wout_ref, gfin_ref, o_ref):
    d = x_ref.shape[-1]
    x = x_ref[...]
    gate = jax.nn.sigmoid(_dot(_rms(x, gmix_ref[...]).astype(BF16), wg_ref[...]))
    out_b = _dot(att_ref[...], wfo_ref[...])
    mix = gate[:, :d] * oa_ref[...] + gate[:, d:] * out_b
    h1 = x + _dot(mix.astype(BF16), wmix_ref[...])

    n = _rms(h1, gq_ref[...]).astype(BF16)
    qm = _dot(n, wq_ref[...])
    hd = qm.shape[-1] // MEM_HEADS
    qm = (qm * (hd ** -0.5)).astype(BF16)
    outs = []
    for hh in range(MEM_HEADS):
        hs = slice(hh * hd, (hh + 1) * hd)
        s = _dot_nt(qm[:, hs], km_ref[0, :, hs])
        s = s - jnp.max(s, axis=-1, keepdims=True)
        p = jnp.exp(s)
        p = p / jnp.sum(p, axis=-1, keepdims=True)
        outs.append(_dot(p.astype(BF16), vm_ref[0, :, hs]))
    o = jnp.concatenate(outs, axis=-1).astype(BF16)
    h2 = h1 + _dot(o, wo_ref[...])

    f = _rms(h2, gf_ref[...]).astype(BF16)
    hidden = wout_ref.shape[0]
    hc = hidden // FFN_CHUNKS
    acc = jnp.zeros_like(h2)
    for c in range(FFN_CHUNKS):
        fa = _dot(f, win_ref[:, c * hc:(c + 1) * hc])
        fb = _dot(f, win_ref[:, hidden + c * hc:hidden + (c + 1) * hc])
        g = (fa * jax.nn.sigmoid(fa) * fb).astype(BF16)
        acc = acc + _dot(g, wout_ref[c * hc:(c + 1) * hc, :])
    o_ref[...] = _rms(h2 + acc, gfin_ref[...])


def _resident(shape):
    n = len(shape)
    return pl.BlockSpec(shape, lambda *_: (0,) * n, pipeline_mode=pl.Buffered(1))


def _post(x2, out_a, att, gain_mix, w_gate, w_fox_o, w_mix, gain_q, w_q, k_m, v_m, w_o,
          gain_f, w_ffn_in, w_ffn_out, gain_fin, bsz, seq):
    t, d = x2.shape
    tm = TM_POST
    nb = seq // tm
    assert (w_ffn_out.shape[0] // FFN_CHUNKS) % LANES == 0
    row = lambda w: pl.BlockSpec((tm, w), lambda b, i: (b * nb + i, 0))
    mem = pl.BlockSpec((1,) + k_m.shape[1:], lambda b, i: (b, 0, 0))
    gain = lambda g: g.reshape(1, d)
    weights = lambda *ws: [_resident(w.shape) for w in ws]
    return pl.pallas_call(
        _post_kernel,
        grid=(bsz, nb),
        in_specs=[row(d), row(d), row(att.shape[1]), _full((1, d))]
                 + weights(w_gate, w_fox_o, w_mix) + [_full((1, d))] + weights(w_q)
                 + [mem, mem] + weights(w_o) + [_full((1, d))] + weights(w_ffn_in, w_ffn_out)
                 + [_full((1, d))],
        out_specs=row(d),
        out_shape=jax.ShapeDtypeStruct((t, d), F32),
        compiler_params=_cparams(("arbitrary", "arbitrary")),
        name="post",
    )(x2, out_a, att, gain(gain_mix), w_gate, w_fox_o, w_mix, gain(gain_q), w_q, k_m, v_m, w_o,
      gain(gain_f), w_ffn_in, w_ffn_out, gain(gain_fin))


def _block_diag(blocks, per):
    g, r, c = blocks.shape
    b = blocks.reshape(g // per, per, r, c)
    eye = jnp.eye(per, dtype=blocks.dtype)
    out = b[:, :, :, None, :] * eye[None, :, None, :, None]
    return out.reshape(g // per, per * r, per * c)


def _seg_perm(q):
    seg = q // SUBLANES
    r = jnp.arange(q)
    src = (r % SUBLANES) * seg + r // SUBLANES
    return (src[:, None] == jnp.arange(q)[None, :]).astype(BF16)


def _layer(x2, mem, bsz, seq, norm_mix, w_in, b_forget, lam_re, lam_im, log_dt, b_re, b_im, c_re,
           c_im, d_skip, w_glu, w_fox_o, w_mix_out, norm_mem_q, norm_mem_kv, w_mem_q, w_mem_kv,
           w_mem_o, norm_ffn, w_ffn_in, w_ffn_out, norm_final):
    d = x2.shape[-1]
    groups, states = lam_re.shape
    ssm_w = groups * SSM_GROUP
    n_heads = b_forget.shape[0]
    fox_w = n_heads * FOX_HEAD_DIM
    o_q = ssm_w
    o_k = o_q + fox_w
    o_v = o_k + fox_w
    o_f = o_v + fox_w
    o_g = o_f + n_heads

    a_re, a_im, s_re, s_im, bb_re, bb_im = _ssm_prep(lam_re, lam_im, log_dt, b_re, b_im)
    nstate = groups * states
    per = LANES // SSM_GROUP
    to_rows = lambda a: a.reshape(1, nstate)
    bd_in = lambda bb: _block_diag(
        bb.reshape(SSM_GROUP, groups, states).transpose(1, 0, 2), per).astype(BF16)
    bd_out = lambda c: _block_diag(c.transpose(0, 2, 1), per).astype(BF16)
    perm = _seg_perm(SSM_CHUNK)

    wf = jnp.zeros((d, LANES), BF16).at[:, :n_heads].set(w_in[:, o_f:o_g].astype(BF16))
    bf = jnp.zeros((1, LANES), F32).at[0, :n_heads].set(b_forget)
    u, q, k, v, cum_t = _in_proj(x2, norm_mix, w_in[:, :o_f].astype(BF16), wf, bf,
                                 ssm_w, fox_w, n_heads, bsz, seq)

    out_a = _ssm(u, perm, perm.T, bd_in(bb_re), bd_in(bb_im), to_rows(a_re), to_rows(a_im),
                 to_rows(s_re), to_rows(s_im), bd_out(c_re), bd_out(c_im),
                 d_skip.reshape(1, ssm_w), w_glu.astype(BF16), bsz, seq)

    att = _fox(q, k, v, cum_t.reshape(bsz, n_heads, seq // FOX_TK, FOX_TK), bsz, seq)

    k_m, v_m = _mem_kv(mem, norm_mem_kv, w_mem_kv)
    bf16 = lambda w: w.astype(BF16)
    return _post(x2, out_a, att, norm_mix, bf16(w_in[:, o_g:]), bf16(w_fox_o), bf16(w_mix_out),
                 norm_mem_q, bf16(w_mem_q), k_m, v_m, bf16(w_mem_o), norm_ffn, bf16(w_ffn_in),
                 bf16(w_ffn_out), norm_final, bsz, seq)


def kernel(x, mem, norm_mix, w_in, b_forget, lam_re, lam_im, log_dt, b_re, b_im, c_re, c_im, d_skip,
           w_glu, w_fox_o, w_mix_out, norm_mem_q, norm_mem_kv, w_mem_q, w_mem_kv, w_mem_o, norm_ffn,
           w_ffn_in, w_ffn_out, norm_final):
    bsz, seq, d = x.shape
    assert w_in.shape[0] == 1, "single-layer block"
    out = _layer(x.reshape(bsz * seq, d), mem, bsz, seq, norm_mix[0], w_in[0], b_forget[0],
                 lam_re[0], lam_im[0], log_dt[0], b_re[0], b_im[0], c_re[0], c_im[0], d_skip[0],
                 w_glu[0], w_fox_o[0], w_mix_out[0], norm_mem_q[0], norm_mem_kv[0], w_mem_q[0],
                 w_mem_kv[0], w_mem_o[0], norm_ffn[0], w_ffn_in[0], w_ffn_out[0], norm_final)
    return out.reshape(bsz, seq, d)
```

```python
import functools
import math

import jax
import jax.numpy as jnp
from jax import lax
from jax.experimental import pallas as pl
from jax.experimental.pallas import tpu as pltpu

F32 = jnp.float32
BF16 = jnp.bfloat16

RMS_EPS = 1e-6
LOG2E = math.log2(math.e)
SSM_GROUP = 16
SSM_STATE = 64
FOX_HEAD_DIM = 64
MEM_HEADS = 4
LANES = 128
SUBLANES = 8
MXU_TILE = 256
VMEM_LIMIT = 56 * 1024 * 1024

TM_PROJ = 512
TM_POST = 512
FFN_CHUNKS = 2
SSM_CHUNK = 256
SSM_SEG = SSM_CHUNK // SUBLANES
SSM_COLS = 512
FOX_TQ = 1024
FOX_TK = 512
FOX_HEADS_PER_STEP = 4
FOX_SKIP_LOG2 = 160.0
FOX_NORM_SLACK = 1.02


def _cparams(sem):
    return pltpu.CompilerParams(dimension_semantics=sem, vmem_limit_bytes=VMEM_LIMIT)


def _rms(x, gain):
    return x * lax.rsqrt(jnp.mean(x * x, axis=-1, keepdims=True) + RMS_EPS) * gain


def _dot(a, b):
    return jnp.dot(a, b, preferred_element_type=F32)


def _dot_nt(a, b):
    return lax.dot_general(a, b, (((1,), (1,)), ((), ())), preferred_element_type=F32)


def _full(shape):
    n = len(shape)
    return pl.BlockSpec(shape, lambda *_: (0,) * n)


def _ssm_prep_kernel(lre_ref, lim_ref, ldt_ref, bre_ref, bim_ref,
                     are_ref, aim_ref, sre_ref, sim_ref, bbre_ref, bbim_ref):
    lre = lre_ref[...]
    lim = lim_ref[...]
    dt = jnp.exp(ldt_ref[...])
    zr = lre * dt
    zi = lim * dt
    mag = jnp.exp(zr)
    are = mag * jnp.cos(zi)
    aim = mag * jnp.sin(zi)
    are_ref[...] = are
    aim_ref[...] = aim
    mag_s = jnp.exp(zr * SSM_SEG)
    sre_ref[...] = mag_s * jnp.cos(zi * SSM_SEG)
    sim_ref[...] = mag_s * jnp.sin(zi * SSM_SEG)
    nr = are - 1.0
    ni = aim
    den = lre * lre + lim * lim
    fr = (nr * lre + ni * lim) / den
    fi = (ni * lre - nr * lim) / den
    bre = bre_ref[...]
    bim = bim_ref[...]
    bbre_ref[...] = fr * bre - fi * bim
    bbim_ref[...] = fr * bim + fi * bre


def _ssm_prep(lam_re, lam_im, log_dt, b_re, b_im):
    g, p = lam_re.shape
    n = b_re.shape[-1]
    c = g * p
    row = lambda a: a.reshape(1, c)
    chan = lambda b: b.transpose(2, 0, 1).reshape(n, c)
    ldt = jnp.broadcast_to(log_dt[:, None], (g, p))
    outs = pl.pallas_call(
        _ssm_prep_kernel,
        out_shape=[jax.ShapeDtypeStruct((1, c), F32)] * 4 + [jax.ShapeDtypeStruct((n, c), F32)] * 2,
        name="ssm_prep",
    )(row(lam_re), row(lam_im), row(ldt), chan(b_re), chan(b_im))
    return outs


def _mem_kv_kernel(m_ref, g_ref, w_ref, k_ref, v_ref):
    n = _rms(m_ref[0], g_ref[...]).astype(BF16)
    kv = _dot(n, w_ref[...])
    half = kv.shape[-1] // 2
    k_ref[0] = kv[:, :half].astype(BF16)
    v_ref[0] = kv[:, half:].astype(BF16)


def _mem_kv(mem, gain, w_kv):
    b, m, d = mem.shape
    w2 = w_kv.shape[-1]
    return pl.pallas_call(
        _mem_kv_kernel,
        grid=(b,),
        in_specs=[pl.BlockSpec((1, m, d), lambda i: (i, 0, 0)), _full((1, d)), _full((d, w2))],
        out_specs=[pl.BlockSpec((1, m, w2 // 2), lambda i: (i, 0, 0))] * 2,
        out_shape=[jax.ShapeDtypeStruct((b, m, w2 // 2), BF16)] * 2,
        compiler_params=_cparams(("arbitrary",)),
        name="mem_kv",
    )(mem, gain.reshape(1, d), w_kv.astype(BF16))


def _gelu_tanh(x):
    return 0.5 * x * (1.0 + jnp.tanh(math.sqrt(2.0 / math.pi) * (x + 0.044715 * (x * x * x))))


def _ssm_chunk(u, perm_ref, permt_ref, bre_ref, bim_ref, are_ref, aim_ref, sre_ref, sim_ref,
               cre_ref, cim_ref, d_ref, wglu_ref, hre, him, car_re, car_im, anchors=()):
    up = _dot(perm_ref[...], u.astype(BF16)).astype(BF16)
    nkb = bre_ref.shape[0]
    kw = bre_ref.shape[1]
    nw = bre_ref.shape[2]
    for kb in range(nkb):
        ukb = up[:, kb * kw:(kb + 1) * kw]
        hre[:, kb * nw:(kb + 1) * nw] = _dot(ukb, bre_ref[kb])
        him[:, kb * nw:(kb + 1) * nw] = _dot(ukb, bim_ref[kb])
    if callable(anchors):
        anchors = anchors()

    nstate = hre.shape[1]
    for cb in range(nstate // SSM_COLS):
        cs = slice(cb * SSM_COLS, (cb + 1) * SSM_COLS)
        a_re = jnp.broadcast_to(are_ref[:, cs], (SUBLANES, SSM_COLS))
        a_im = jnp.broadcast_to(aim_ref[:, cs], (SUBLANES, SSM_COLS))

        e_re = hre[0:SUBLANES, cs]
        e_im = him[0:SUBLANES, cs]
        for k in range(1, SSM_SEG):
            r = slice(k * SUBLANES, (k + 1) * SUBLANES)
            e_re, e_im = (a_re * e_re - a_im * e_im + hre[r, cs],
                          a_re * e_im + a_im * e_re + him[r, cs])
            hre[r, cs] = e_re
            him[r, cs] = e_im

        s_re = sre_ref[:, cs]
        s_im = sim_ref[:, cs]
        c_re = car_re[:, cs]
        c_im = car_im[:, cs]
        rows_re, rows_im = [], []
        for j in range(SUBLANES):
            rows_re.append(c_re)
            rows_im.append(c_im)
            n_re = s_re * c_re - s_im * c_im + e_re[j:j + 1]
            n_im = s_re * c_im + s_im * c_re + e_im[j:j + 1]
            c_re, c_im = n_re, n_im
        car_re[:, cs] = c_re
        car_im[:, cs] = c_im

        d_re = jnp.concatenate(rows_re, axis=0)
        d_im = jnp.concatenate(rows_im, axis=0)
        for k in range(SSM_SEG):
            r = slice(k * SUBLANES, (k + 1) * SUBLANES)
            d_re, d_im = a_re * d_re - a_im * d_im, a_re * d_im + a_im * d_re
            if k == SSM_SEG // 2:
                for z in anchors[cb::nstate // SSM_COLS]:
                    d_re = d_re + jnp.concatenate([z] * (SSM_COLS // LANES), axis=1)
            hre[r, cs] = hre[r, cs] + d_re
            him[r, cs] = him[r, cs] + d_im

    ncb = cre_ref.shape[0]
    cw = cre_ref.shape[1]
    ys = []
    for kb in range(ncb):
        h_re = hre[:, kb * cw:(kb + 1) * cw].astype(BF16)
        h_im = him[:, kb * cw:(kb + 1) * cw].astype(BF16)
        ys.append(_dot(h_re, cre_ref[kb]) - _dot(h_im, cim_ref[kb]))
    yp = jnp.concatenate(ys, axis=-1)
    hi = yp.astype(BF16)
    lo = (yp - hi.astype(F32)).astype(BF16)
    y = _dot(permt_ref[...], hi) + _dot(permt_ref[...], lo)
    y = y + d_ref[...] * u
    z = _dot(_gelu_tanh(y).astype(BF16), wglu_ref[...])
    half = z.shape[-1] // 2
    return z[:, :half] * jax.nn.sigmoid(z[:, half:])


def _schedule_anchor(x):
    rows, cols = x.shape
    s = x[:, :LANES]
    for c in range(1, cols // LANES):
        s = s + x[:, c * LANES:(c + 1) * LANES]
    s = jnp.sum(s.reshape(rows // SUBLANES, SUBLANES, LANES), axis=0)
    bits = pltpu.bitcast(s, jnp.uint32)
    return ((bits >> 16) >> 16).astype(F32)


def _in_ssm_kernel(x_ref, g_ref, w_ref, wf_ref, bf_ref, *rest):
    nconst = 12
    ssm_refs = rest[:nconst]
    q_ref, k_ref, v_ref, cum_ref, oa_ref, carry_ref, car_re, car_im = rest[nconst:nconst + 8]
    h_scr = rest[nconst + 8:]

    @pl.when(pl.program_id(1) == 0)
    def _():
        carry_ref[...] = jnp.zeros_like(carry_ref)
        car_re[...] = jnp.zeros_like(car_re)
        car_im[...] = jnp.zeros_like(car_im)

    un = _rms(x_ref[...], g_ref[...]).astype(BF16)
    wu = ssm_refs[10].shape[1]
    wh = q_ref.shape[1]
    col = lambda lo, hi: _dot(un, w_ref[:, lo:hi])
    u = col(0, wu)
    pieces = [(ref, lo + p * MXU_TILE, p * MXU_TILE, scale)
              for ref, lo, scale in ((q_ref, wu, FOX_HEAD_DIM ** -0.5 * LOG2E),
                                     (k_ref, wu + wh, None), (v_ref, wu + 2 * wh, None))
              for p in range(wh // MXU_TILE)]
    n_chunks = u.shape[0] // SSM_CHUNK
    per_chunk = -(-len(pieces) // n_chunks)

    def project(todo):
        anchors = []
        for ref, src, dst, scale in todo:
            piece = col(src, src + MXU_TILE)
            if scale is not None:
                piece = piece * scale
            ref[:, dst:dst + MXU_TILE] = piece.astype(BF16)
            anchors.append(_schedule_anchor(piece))
        return anchors

    for c in range(n_chunks):
        rows = slice(c * SSM_CHUNK, (c + 1) * SSM_CHUNK)
        todo = pieces[c * per_chunk:(c + 1) * per_chunk]
        oa_ref[rows, :] = _ssm_chunk(u[rows], *ssm_refs, h_scr[2 * c], h_scr[2 * c + 1],
                                     car_re, car_im, functools.partial(project, todo))
    c = jax.nn.log_sigmoid(_dot(un, wf_ref[...]) + bf_ref[...])
    rows = c.shape[0]
    row = lax.broadcasted_iota(jnp.int32, c.shape, 0)
    shift = 1
    while shift < rows:
        c = c + jnp.where(row >= shift, pltpu.roll(c, shift, 0), 0.0)
        shift *= 2
    c = c + carry_ref[...]
    carry_ref[...] = c[rows - 1:rows, :]
    cum_ref[0] = (c * LOG2E).T[:cum_ref.shape[1], :]


def _in_ssm(x2, gain, w_uqkv, wf, bf, ssm_consts, fox_w, n_heads, bsz, seq):
    t, d = x2.shape
    tm = TM_PROJ
    nb = seq // tm
    assert tm % SSM_CHUNK == 0 and len(ssm_consts) == 12
    nstate = ssm_consts[4].shape[1]
    dm = ssm_consts[11].shape[1] // 2
    row = lambda w: pl.BlockSpec((tm, w), lambda b, i: (b * nb + i, 0))
    h_scr = [pltpu.VMEM((SSM_CHUNK, nstate), F32)] * (2 * (tm // SSM_CHUNK))
    return pl.pallas_call(
        _in_ssm_kernel,
        grid=(bsz, nb),
        in_specs=[row(d), _full((1, d)), _full(w_uqkv.shape), _full(wf.shape), _full((1, LANES))]
                 + [_full(c.shape) for c in ssm_consts],
        out_specs=[row(fox_w), row(fox_w), row(fox_w),
                   pl.BlockSpec((1, n_heads, tm), lambda b, i: (b, 0, i)), row(dm)],
        out_shape=[jax.ShapeDtypeStruct((t, fox_w), BF16),
                   jax.ShapeDtypeStruct((t, fox_w), BF16),
                   jax.ShapeDtypeStruct((t, fox_w), BF16),
                   jax.ShapeDtypeStruct((bsz, n_heads, seq), F32),
                   jax.ShapeDtypeStruct((t, dm), F32)],
        scratch_shapes=[pltpu.VMEM((1, LANES), F32), pltpu.VMEM((1, nstate), F32),
                        pltpu.VMEM((1, nstate), F32)] + h_scr,
        compiler_params=_cparams(("arbitrary", "arbitrary")),
        name="in_ssm",
    )(x2, gain.reshape(1, d), w_uqkv, wf, bf, *ssm_consts)


def _fox_kernel(q_ref, k_ref, v_ref, ck_ref, o_ref, m_scr, acc_scr, first_blk):
    seq = q_ref.shape[0]
    tq = FOX_TQ
    tk = FOX_TK
    dh = FOX_HEAD_DIM
    n_heads = ck_ref.shape[1]
    lane = lax.broadcasted_iota(jnp.int32, (1, LANES), 1)
    keep = [jnp.where(lane < dh, 1.0, 0.0).astype(BF16), jnp.where(lane >= dh, 1.0, 0.0).astype(BF16)]
    sum_lane = [dh, 0]
    ones = [jnp.where(lane == sl, 1.0, 0.0).astype(BF16) for sl in sum_lane]

    def block(q, j, rows, h, masked):
        e = h % 2
        ps = slice((h // 2) * LANES, (h // 2 + 1) * LANES)
        hs = slice(h * LANES, (h + 1) * LANES)
        ks = pl.ds(pl.multiple_of(j * tk, tk), tk)
        s = _dot_nt(q, k_ref[ks, ps] * keep[e]) - ck_ref[0, h, pl.ds(j, 1), :]
        if masked:
            r = lax.broadcasted_iota(jnp.int32, (tk, tk), 0)
            c = lax.broadcasted_iota(jnp.int32, (tk, tk), 1)
            top = jnp.where(c <= r, s[:tk], -jnp.inf)
            s = top if s.shape[0] == tk else jnp.concatenate([top, s[tk:]], axis=0)
        m_prev = m_scr[rows, hs]
        m_new = jnp.maximum(m_prev, jnp.max(s, axis=1, keepdims=True))
        p = jnp.exp2(s - jnp.concatenate([m_new] * (tk // LANES), axis=1))
        acc_scr[rows, hs] = (jnp.exp2(m_prev - m_new) * acc_scr[rows, hs]
                             + _dot(p.astype(BF16), v_ref[ks, ps] * keep[e] + ones[e]))
        m_scr[rows, hs] = m_new

    nsub = tq // tk
    nq = seq // tq
    nk = seq // tk
    all_rows = slice(0, tq)

    col = lax.broadcasted_iota(jnp.int32, (q_ref.shape[1], LANES), 0) // dh
    head_sel = jnp.where(col == lax.broadcasted_iota(jnp.int32, (q_ref.shape[1], LANES), 1),
                         1.0, 0.0).astype(BF16)

    def max_sq_norm(ref, lo, n):
        x = ref[lo:lo + n, :]
        sq = _dot(x * x, head_sel)
        return jnp.max(sq, axis=0, keepdims=True) * FOX_NORM_SLACK

    k_sq = jnp.concatenate([max_sq_norm(k_ref, j * tk, tk) for j in range(nk)], axis=0)
    q_sq = jnp.concatenate([max_sq_norm(q_ref, i * tq, tq) for i in range(nq)], axis=0)
    blk_id = lax.broadcasted_iota(jnp.int32, (nk, 1), 0).astype(F32)
    for h in range(n_heads):
        k_max = jnp.sqrt(k_sq[:, h:h + 1])
        c_end = ck_ref[0, h, :, tk - 1:tk]
        for i in range(nq):
            first_diag = float(i * nsub)
            q_max = jnp.sqrt(q_sq[i:i + 1, h:h + 1])
            k_own = jnp.max(k_max[i * nsub:(i + 1) * nsub], axis=0, keepdims=True)
            c_start = ck_ref[0, h, i * nsub:i * nsub + 1, 0:1]
            bound = q_max * (k_max + k_own) + (c_start - c_end)
            visit = jnp.logical_and(bound >= -FOX_SKIP_LOG2, blk_id < first_diag)
            first_blk[h * nq + i] = jnp.min(jnp.where(visit, blk_id, first_diag)).astype(jnp.int32)

    def qblock(qi, _):
        qs = pl.ds(pl.multiple_of(qi * tq, tq), tq)
        qp = [q_ref[qs, p * LANES:(p + 1) * LANES] for p in range(n_heads // 2)]
        m_scr[...] = jnp.full(m_scr.shape, -jnp.inf, F32)
        acc_scr[...] = jnp.zeros(acc_scr.shape, F32)

        def full(j, _):
            for h in range(n_heads):
                block(qp[h // 2], j, all_rows, h, False)
            return 0

        starts = [first_blk[h * nq + qi] for h in range(n_heads)]
        common = functools.reduce(jnp.maximum, starts)
        for h in range(n_heads):
            def one(j, _, h=h):
                block(qp[h // 2], j, all_rows, h, False)
                return 0
            lax.fori_loop(starts[h], common, one, 0)
        lax.fori_loop(common, qi * nsub, full, 0)
        for r in range(nsub):
            rows = slice(r * tk, tq)
            for h in range(n_heads):
                block(qp[h // 2][rows], qi * nsub + r, rows, h, True)
        for p in range(n_heads // 2):
            a0 = acc_scr[:, (2 * p) * LANES:(2 * p + 1) * LANES]
            a1 = acc_scr[:, (2 * p + 1) * LANES:(2 * p + 2) * LANES]
            out = jnp.where(lane < dh, a0 / a0[:, sum_lane[0]:sum_lane[0] + 1],
                            a1 / a1[:, sum_lane[1]:sum_lane[1] + 1])
            o_ref[qs, p * LANES:(p + 1) * LANES] = out.astype(o_ref.dtype)
        return 0

    lax.fori_loop(0, seq // tq, qblock, 0)


def _fox(q, k, v, cum_t, bsz, seq):
    t, w = q.shape
    assert FOX_TQ % FOX_TK == 0 and seq % FOX_TQ == 0 and FOX_HEADS_PER_STEP % 2 == 0
    hps = FOX_HEADS_PER_STEP
    wblk = hps * FOX_HEAD_DIM
    blk = pl.BlockSpec((seq, wblk), lambda bi, hi: (bi, hi))
    return pl.pallas_call(
        _fox_kernel,
        grid=(bsz, w // wblk),
        in_specs=[blk, blk, blk,
                  pl.BlockSpec((1, hps) + cum_t.shape[2:], lambda bi, hi: (bi, hi, 0, 0))],
        out_specs=blk,
        out_shape=jax.ShapeDtypeStruct((t, w), BF16),
        scratch_shapes=[pltpu.VMEM((FOX_TQ, hps * LANES), F32),
                        pltpu.VMEM((FOX_TQ, hps * LANES), F32),
                        pltpu.SMEM((hps * (seq // FOX_TQ),), jnp.int32)],
        compiler_params=_cparams(("arbitrary", "arbitrary")),
        name="fox",
    )(q, k, v, cum_t)


def _post_kernel(x_ref, oa_ref, att_ref, gmix_ref, wg_ref, wfo_ref, wmix_ref, gq_ref, wq_ref,
                 km_ref, vm_ref, wo_ref, gf_ref, win_ref, wout_ref, gfin_ref, o_ref):
    d = x_ref.shape[-1]
    x = x_ref[...]
    gate = jax.nn.sigmoid(_dot(_rms(x, gmix_ref[...]).astype(BF16), wg_ref[...]))
    out_b = _dot(att_ref[...], wfo_ref[...])
    mix = gate[:, :d] * oa_ref[...] + gate[:, d:] * out_b
    h1 = x + _dot(mix.astype(BF16), wmix_ref[...])

    n = _rms(h1, gq_ref[...]).astype(BF16)
    qm = _dot(n, wq_ref[...])
    hd = qm.shape[-1] // MEM_HEADS
    qm = (qm * (hd ** -0.5)).astype(BF16)
    outs = []
    for hh in range(MEM_HEADS):
        hs = slice(hh * hd, (hh + 1) * hd)
        s = _dot_nt(qm[:, hs], km_ref[0, :, hs])
        s = s - jnp.max(s, axis=-1, keepdims=True)
        p = jnp.exp(s)
        p = p / jnp.sum(p, axis=-1, keepdims=True)
        outs.append(_dot(p.astype(BF16), vm_ref[0, :, hs]))
    o = jnp.concatenate(outs, axis=-1).astype(BF16)
    h2 = h1 + _dot(o, wo_ref[...])

    f = _rms(h2, gf_ref[...]).astype(BF16)
    hidden = wout_ref.shape[0]
    acc = jnp.zeros_like(h2)
    for lo, hi in _ffn_chunks(hidden):
        fa = _dot(f, win_ref[:, lo:hi])
        fb = _dot(f, win_ref[:, hidden + lo:hidden + hi])
        g = (fa * jax.nn.sigmoid(fa) * fb).astype(BF16)
        acc = acc + _dot(g, wout_ref[lo:hi, :])
    o_ref[...] = _rms(h2 + acc, gfin_ref[...])


def _ffn_chunks(hidden):
    tiles = hidden // MXU_TILE
    assert tiles * MXU_TILE == hidden
    cuts = [-(-tiles * c // FFN_CHUNKS) * MXU_TILE for c in range(FFN_CHUNKS + 1)]
    return list(zip(cuts[:-1], cuts[1:]))


def _resident(shape):
    n = len(shape)
    return pl.BlockSpec(shape, lambda *_: (0,) * n, pipeline_mode=pl.Buffered(1))


def _post(x2, out_a, att, gain_mix, w_gate, w_fox_o, w_mix, gain_q, w_q, k_m, v_m, w_o,
          gain_f, w_ffn_in, w_ffn_out, gain_fin, bsz, seq):
    t, d = x2.shape
    tm = TM_POST
    nb = seq // tm
    row = lambda w: pl.BlockSpec((tm, w), lambda b, i: (b * nb + i, 0))
    mem = pl.BlockSpec((1,) + k_m.shape[1:], lambda b, i: (b, 0, 0))
    gain = lambda g: g.reshape(1, d)
    weights = lambda *ws: [_resident(w.shape) for w in ws]
    return pl.pallas_call(
        _post_kernel,
        grid=(bsz, nb),
        in_specs=[row(d), row(d), row(att.shape[1]), _full((1, d))]
                 + weights(w_gate, w_fox_o, w_mix) + [_full((1, d))] + weights(w_q)
                 + [mem, mem] + weights(w_o) + [_full((1, d))] + weights(w_ffn_in, w_ffn_out)
                 + [_full((1, d))],
        out_specs=row(d),
        out_shape=jax.ShapeDtypeStruct((t, d), F32),
        compiler_params=_cparams(("arbitrary", "arbitrary")),
        name="post",
    )(x2, out_a, att, gain(gain_mix), w_gate, w_fox_o, w_mix, gain(gain_q), w_q, k_m, v_m, w_o,
      gain(gain_f), w_ffn_in, w_ffn_out, gain(gain_fin))


def _block_diag(blocks, per):
    g, r, c = blocks.shape
    b = blocks.reshape(g // per, per, r, c)
    eye = jnp.eye(per, dtype=blocks.dtype)
    out = b[:, :, :, None, :] * eye[None, :, None, :, None]
    return out.reshape(g // per, per * r, per * c)


def _seg_perm(q):
    seg = q // SUBLANES
    r = jnp.arange(q)
    src = (r % SUBLANES) * seg + r // SUBLANES
    return (src[:, None] == jnp.arange(q)[None, :]).astype(BF16)


def _layer(x2, mem, bsz, seq, norm_mix, w_in, b_forget, lam_re, lam_im, log_dt, b_re, b_im, c_re,
           c_im, d_skip, w_glu, w_fox_o, w_mix_out, norm_mem_q, norm_mem_kv, w_mem_q, w_mem_kv,
           w_mem_o, norm_ffn, w_ffn_in, w_ffn_out, norm_final):
    d = x2.shape[-1]
    groups, states = lam_re.shape
    ssm_w = groups * SSM_GROUP
    n_heads = b_forget.shape[0]
    fox_w = n_heads * FOX_HEAD_DIM
    o_q = ssm_w
    o_k = o_q + fox_w
    o_v = o_k + fox_w
    o_f = o_v + fox_w
    o_g = o_f + n_heads

    a_re, a_im, s_re, s_im, bb_re, bb_im = _ssm_prep(lam_re, lam_im, log_dt, b_re, b_im)
    nstate = groups * states
    per = LANES // SSM_GROUP
    to_rows = lambda a: a.reshape(1, nstate)
    bd_in = lambda bb: _block_diag(
        bb.reshape(SSM_GROUP, groups, states).transpose(1, 0, 2), per).astype(BF16)
    bd_out = lambda c: _block_diag(c.transpose(0, 2, 1), per).astype(BF16)
    perm = _seg_perm(SSM_CHUNK)

    wf = jnp.zeros((d, LANES), BF16).at[:, :n_heads].set(w_in[:, o_f:o_g].astype(BF16))
    bf = jnp.zeros((1, LANES), F32).at[0, :n_heads].set(b_forget)
    ssm_consts = [perm, perm.T, bd_in(bb_re), bd_in(bb_im), to_rows(a_re), to_rows(a_im),
                  to_rows(s_re), to_rows(s_im), bd_out(c_re), bd_out(c_im),
                  d_skip.reshape(1, ssm_w), w_glu.astype(BF16)]
    q, k, v, cum_t, out_a = _in_ssm(x2, norm_mix, w_in[:, :o_f].astype(BF16), wf, bf, ssm_consts,
                                    fox_w, n_heads, bsz, seq)

    att = _fox(q, k, v, cum_t.reshape(bsz, n_heads, seq // FOX_TK, FOX_TK), bsz, seq)

    k_m, v_m = _mem_kv(mem, norm_mem_kv, w_mem_kv)
    bf16 = lambda w: w.astype(BF16)
    return _post(x2, out_a, att, norm_mix, bf16(w_in[:, o_g:]), bf16(w_fox_o), bf16(w_mix_out),
                 norm_mem_q, bf16(w_mem_q), k_m, v_m, bf16(w_mem_o), norm_ffn, bf16(w_ffn_in),
                 bf16(w_ffn_out), norm_final, bsz, seq)


def kernel(x, mem, norm_mix, w_in, b_forget, lam_re, lam_im, log_dt, b_re, b_im, c_re, c_im, d_skip,
           w_glu, w_fox_o, w_mix_out, norm_mem_q, norm_mem_kv, w_mem_q, w_mem_kv, w_mem_o, norm_ffn,
           w_ffn_in, w_ffn_out, norm_final):
    bsz, seq, d = x.shape
    assert w_in.shape[0] == 1, "single-layer block"
    out = _layer(x.reshape(bsz * seq, d), mem, bsz, seq, norm_mix[0], w_in[0], b_forget[0],
                 lam_re[0], lam_im[0], log_dt[0], b_re[0], b_im[0], c_re[0], c_im[0], d_skip[0],
                 w_glu[0], w_fox_o[0], w_mix_out[0], norm_mem_q[0], norm_mem_kv[0], w_mem_q[0],
                 w_mem_kv[0], w_mem_o[0], norm_ffn[0], w_ffn_in[0], w_ffn_out[0], norm_final)
    return out.reshape(bsz, seq, d)
```

```python
import functools
import math

import jax
import jax.numpy as jnp
from jax import lax
from jax.experimental import pallas as pl
from jax.experimental.pallas import tpu as pltpu

F32 = jnp.float32
BF16 = jnp.bfloat16

RMS_EPS = 1e-6
LOG2E = math.log2(math.e)
SSM_GROUP = 16
SSM_STATE = 64
FOX_HEAD_DIM = 64
MEM_HEADS = 4
LANES = 128
SUBLANES = 8
MXU_TILE = 256
VMEM_LIMIT = 56 * 1024 * 1024

TM_PROJ = 512
TM_POST = 512
FFN_CHUNKS = 2
SSM_CHUNK = 256
SSM_SEG = SSM_CHUNK // SUBLANES
SSM_COLS = 512
FOX_TQ = 1024
FOX_TK = 512
FOX_HEADS_PER_STEP = 4
FOX_SKIP_LOG2 = 160.0
FOX_NORM_SLACK = 1.02


def _cparams(sem):
    return pltpu.CompilerParams(dimension_semantics=sem, vmem_limit_bytes=VMEM_LIMIT)


def _rms(x, gain):
    return x * lax.rsqrt(jnp.mean(x * x, axis=-1, keepdims=True) + RMS_EPS) * gain


def _dot(a, b):
    return jnp.dot(a, b, preferred_element_type=F32)


def _dot_nt(a, b):
    return lax.dot_general(a, b, (((1,), (1,)), ((), ())), preferred_element_type=F32)


def _full(shape):
    n = len(shape)
    return pl.BlockSpec(shape, lambda *_: (0,) * n)


def _ssm_prep_kernel(lre_ref, lim_ref, ldt_ref, bre_ref, bim_ref,
                     are_ref, aim_ref, sre_ref, sim_ref, bbre_ref, bbim_ref):
    lre = lre_ref[...]
    lim = lim_ref[...]
    dt = jnp.exp(ldt_ref[...])
    zr = lre * dt
    zi = lim * dt
    mag = jnp.exp(zr)
    are = mag * jnp.cos(zi)
    aim = mag * jnp.sin(zi)
    are_ref[...] = are
    aim_ref[...] = aim
    mag_s = jnp.exp(zr * SSM_SEG)
    sre_ref[...] = mag_s * jnp.cos(zi * SSM_SEG)
    sim_ref[...] = mag_s * jnp.sin(zi * SSM_SEG)
    nr = are - 1.0
    ni = aim
    den = lre * lre + lim * lim
    fr = (nr * lre + ni * lim) / den
    fi = (ni * lre - nr * lim) / den
    bre = bre_ref[...]
    bim = bim_ref[...]
    bbre_ref[...] = fr * bre - fi * bim
    bbim_ref[...] = fr * bim + fi * bre


def _ssm_prep(lam_re, lam_im, log_dt, b_re, b_im):
    g, p = lam_re.shape
    n = b_re.shape[-1]
    c = g * p
    row = lambda a: a.reshape(1, c)
    chan = lambda b: b.transpose(2, 0, 1).reshape(n, c)
    ldt = jnp.broadcast_to(log_dt[:, None], (g, p))
    outs = pl.pallas_call(
        _ssm_prep_kernel,
        out_shape=[jax.ShapeDtypeStruct((1, c), F32)] * 4 + [jax.ShapeDtypeStruct((n, c), F32)] * 2,
        name="ssm_prep",
    )(row(lam_re), row(lam_im), row(ldt), chan(b_re), chan(b_im))
    return outs


def _mem_kv_kernel(m_ref, g_ref, w_ref, k_ref, v_ref):
    n = _rms(m_ref[0], g_ref[...]).astype(BF16)
    kv = _dot(n, w_ref[...])
    half = kv.shape[-1] // 2
    k_ref[0] = kv[:, :half].astype(BF16)
    v_ref[0] = kv[:, half:].astype(BF16)


def _mem_kv(mem, gain, w_kv):
    b, m, d = mem.shape
    w2 = w_kv.shape[-1]
    return pl.pallas_call(
        _mem_kv_kernel,
        grid=(b,),
        in_specs=[pl.BlockSpec((1, m, d), lambda i: (i, 0, 0)), _full((1, d)), _full((d, w2))],
        out_specs=[pl.BlockSpec((1, m, w2 // 2), lambda i: (i, 0, 0))] * 2,
        out_shape=[jax.ShapeDtypeStruct((b, m, w2 // 2), BF16)] * 2,
        compiler_params=_cparams(("arbitrary",)),
        name="mem_kv",
    )(mem, gain.reshape(1, d), w_kv.astype(BF16))


def _gelu_tanh(x):
    return 0.5 * x * (1.0 + jnp.tanh(math.sqrt(2.0 / math.pi) * (x + 0.044715 * (x * x * x))))


def _ssm_chunk(u, perm_ref, permt_ref, bre_ref, bim_ref, are_ref, aim_ref, sre_ref, sim_ref,
               cre_ref, cim_ref, d_ref, wglu_ref, hre, him, car_re, car_im, anchors=()):
    up = _dot(perm_ref[...], u.astype(BF16)).astype(BF16)
    nkb = bre_ref.shape[0]
    kw = bre_ref.shape[1]
    nw = bre_ref.shape[2]
    for kb in range(nkb):
        ukb = up[:, kb * kw:(kb + 1) * kw]
        hre[:, kb * nw:(kb + 1) * nw] = _dot(ukb, bre_ref[kb])
        him[:, kb * nw:(kb + 1) * nw] = _dot(ukb, bim_ref[kb])
    if callable(anchors):
        anchors = anchors()

    nstate = hre.shape[1]
    for cb in range(nstate // SSM_COLS):
        cs = slice(cb * SSM_COLS, (cb + 1) * SSM_COLS)
        a_re = jnp.broadcast_to(are_ref[:, cs], (SUBLANES, SSM_COLS))
        a_im = jnp.broadcast_to(aim_ref[:, cs], (SUBLANES, SSM_COLS))

        e_re = hre[0:SUBLANES, cs]
        e_im = him[0:SUBLANES, cs]
        for k in range(1, SSM_SEG):
            r = slice(k * SUBLANES, (k + 1) * SUBLANES)
            e_re, e_im = (a_re * e_re - a_im * e_im + hre[r, cs],
                          a_re * e_im + a_im * e_re + him[r, cs])
            hre[r, cs] = e_re
            him[r, cs] = e_im

        s_re = sre_ref[:, cs]
        s_im = sim_ref[:, cs]
        c_re = car_re[:, cs]
        c_im = car_im[:, cs]
        rows_re, rows_im = [], []
        for j in range(SUBLANES):
            rows_re.append(c_re)
            rows_im.append(c_im)
            n_re = s_re * c_re - s_im * c_im + e_re[j:j + 1]
            n_im = s_re * c_im + s_im * c_re + e_im[j:j + 1]
            c_re, c_im = n_re, n_im
        car_re[:, cs] = c_re
        car_im[:, cs] = c_im

        d_re = jnp.concatenate(rows_re, axis=0)
        d_im = jnp.concatenate(rows_im, axis=0)
        for k in range(SSM_SEG):
            r = slice(k * SUBLANES, (k + 1) * SUBLANES)
            d_re, d_im = a_re * d_re - a_im * d_im, a_re * d_im + a_im * d_re
            if k == SSM_SEG // 2:
                for z in anchors[cb::nstate // SSM_COLS]:
                    d_re = d_re + jnp.concatenate([z] * (SSM_COLS // LANES), axis=1)
            hre[r, cs] = hre[r, cs] + d_re
            him[r, cs] = him[r, cs] + d_im

    ncb = cre_ref.shape[0]
    cw = cre_ref.shape[1]
    ys = []
    for kb in range(ncb):
        h_re = hre[:, kb * cw:(kb + 1) * cw].astype(BF16)
        h_im = him[:, kb * cw:(kb + 1) * cw].astype(BF16)
        ys.append(_dot(h_re, cre_ref[kb]) - _dot(h_im, cim_ref[kb]))
    yp = jnp.concatenate(ys, axis=-1)
    hi = yp.astype(BF16)
    lo = (yp - hi.astype(F32)).astype(BF16)
    y = _dot(permt_ref[...], hi) + _dot(permt_ref[...], lo)
    y = y + d_ref[...] * u
    z = _dot(_gelu_tanh(y).astype(BF16), wglu_ref[...])
    half = z.shape[-1] // 2
    return z[:, :half] * jax.nn.sigmoid(z[:, half:])


def _schedule_anchor(x):
    rows, cols = x.shape
    s = x[:, :LANES]
    for c in range(1, cols // LANES):
        s = s + x[:, c * LANES:(c + 1) * LANES]
    s = jnp.sum(s.reshape(rows // SUBLANES, SUBLANES, LANES), axis=0)
    bits = pltpu.bitcast(s, jnp.uint32)
    return ((bits >> 16) >> 16).astype(F32)


def _in_ssm_kernel(x_ref, g_ref, wa_ref, wb_ref, wc_ref, bf_ref, *rest):
    nconst = 12
    ssm_refs = rest[:nconst]
    (q_ref, k_ref, v_ref, cum_ref, oa_ref, wg_ref,
     carry_ref, car_re, car_im, w_bf, wf_bf) = rest[nconst:nconst + 11]
    h_scr = rest[nconst + 11:]
    n_forget = cum_ref.shape[1]

    @pl.when(pl.program_id(1) == 0)
    def _():
        carry_ref[...] = jnp.zeros_like(carry_ref)
        car_re[...] = jnp.zeros_like(car_re)
        car_im[...] = jnp.zeros_like(car_im)
        w_bf[...] = wa_ref[...].astype(BF16)
        wf_bf[...] = wb_ref[:, :LANES].astype(BF16)
        wg_ref[...] = jnp.concatenate([wb_ref[:, n_forget:], wc_ref[:, :n_forget]],
                                      axis=1).astype(BF16)

    un = _rms(x_ref[...], g_ref[...]).astype(BF16)
    wu = ssm_refs[10].shape[1]
    wh = q_ref.shape[1]
    col = lambda lo, hi: _dot(un, w_bf[:, lo:hi])
    u = col(0, wu)
    pieces = [(ref, lo + p * MXU_TILE, p * MXU_TILE, scale)
              for ref, lo, scale in ((q_ref, wu, FOX_HEAD_DIM ** -0.5 * LOG2E),
                                     (k_ref, wu + wh, None), (v_ref, wu + 2 * wh, None))
              for p in range(wh // MXU_TILE)]
    n_chunks = u.shape[0] // SSM_CHUNK
    per_chunk = -(-len(pieces) // n_chunks)

    def project(todo):
        anchors = []
        for ref, src, dst, scale in todo:
            piece = col(src, src + MXU_TILE)
            if scale is not None:
                piece = piece * scale
            ref[:, dst:dst + MXU_TILE] = piece.astype(BF16)
            anchors.append(_schedule_anchor(piece))
        return anchors

    for c in range(n_chunks):
        rows = slice(c * SSM_CHUNK, (c + 1) * SSM_CHUNK)
        todo = pieces[c * per_chunk:(c + 1) * per_chunk]
        oa_ref[rows, :] = _ssm_chunk(u[rows], *ssm_refs, h_scr[2 * c], h_scr[2 * c + 1],
                                     car_re, car_im, functools.partial(project, todo))
    c = jax.nn.log_sigmoid(_dot(un, wf_bf[...]) + bf_ref[...])
    rows = c.shape[0]
    row = lax.broadcasted_iota(jnp.int32, c.shape, 0)
    shift = 1
    while shift < rows:
        c = c + jnp.where(row >= shift, pltpu.roll(c, shift, 0), 0.0)
        shift *= 2
    c = c + carry_ref[...]
    carry_ref[...] = c[rows - 1:rows, :]
    cum_ref[0] = (c * LOG2E).T[:cum_ref.shape[1], :]


def _in_ssm(x2, gain, w_in, bf, ssm_consts, ssm_w, fox_w, n_heads, bsz, seq):
    t, d = x2.shape
    tm = TM_PROJ
    nb = seq // tm
    assert tm % SSM_CHUNK == 0 and len(ssm_consts) == 12
    nstate = ssm_consts[4].shape[1]
    dm = ssm_consts[11].shape[1] // 2
    main_w = ssm_w + 3 * fox_w
    gate_w = w_in.shape[1] - main_w - n_heads
    assert gate_w == main_w and main_w % LANES == 0 and n_heads <= LANES
    win = lambda width, idx: pl.BlockSpec((d, width), lambda *_: (0, idx),
                                          pipeline_mode=pl.Buffered(1))
    row = lambda w: pl.BlockSpec((tm, w), lambda b, i: (b * nb + i, 0))
    h_scr = [pltpu.VMEM((SSM_CHUNK, nstate), F32)] * (2 * (tm // SSM_CHUNK))
    return pl.pallas_call(
        _in_ssm_kernel,
        grid=(bsz, nb),
        in_specs=[row(d), _full((1, d)), win(main_w, 0), win(main_w, 1),
                  win(LANES, 2 * main_w // LANES), _full((1, LANES))]
                 + [_full(c.shape) for c in ssm_consts],
        out_specs=[row(fox_w), row(fox_w), row(fox_w),
                   pl.BlockSpec((1, n_heads, tm), lambda b, i: (b, 0, i)), row(dm),
                   _full((d, gate_w))],
        out_shape=[jax.ShapeDtypeStruct((t, fox_w), BF16),
                   jax.ShapeDtypeStruct((t, fox_w), BF16),
                   jax.ShapeDtypeStruct((t, fox_w), BF16),
                   jax.ShapeDtypeStruct((bsz, n_heads, seq), F32),
                   jax.ShapeDtypeStruct((t, dm), F32),
                   jax.ShapeDtypeStruct((d, gate_w), BF16)],
        scratch_shapes=[pltpu.VMEM((1, LANES), F32), pltpu.VMEM((1, nstate), F32),
                        pltpu.VMEM((1, nstate), F32), pltpu.VMEM((d, main_w), BF16),
                        pltpu.VMEM((d, LANES), BF16)] + h_scr,
        compiler_params=_cparams(("arbitrary", "arbitrary")),
        name="in_ssm",
    )(x2, gain.reshape(1, d), w_in, w_in, w_in, bf, *ssm_consts)


def _fox_kernel(q_ref, k_ref, v_ref, ck_ref, o_ref, m_scr, acc_scr, first_blk):
    seq = q_ref.shape[0]
    tq = FOX_TQ
    tk = FOX_TK
    dh = FOX_HEAD_DIM
    n_heads = ck_ref.shape[1]
    lane = lax.broadcasted_iota(jnp.int32, (1, LANES), 1)
    keep = [jnp.where(lane < dh, 1.0, 0.0).astype(BF16), jnp.where(lane >= dh, 1.0, 0.0).astype(BF16)]
    sum_lane = [dh, 0]
    ones = [jnp.where(lane == sl, 1.0, 0.0).astype(BF16) for sl in sum_lane]

    def block(q, j, rows, h, masked):
        e = h % 2
        ps = slice((h // 2) * LANES, (h // 2 + 1) * LANES)
        hs = slice(h * LANES, (h + 1) * LANES)
        ks = pl.ds(pl.multiple_of(j * tk, tk), tk)
        s = _dot_nt(q, k_ref[ks, ps] * keep[e]) - ck_ref[0, h, pl.ds(j, 1), :]
        if masked:
            r = lax.broadcasted_iota(jnp.int32, (tk, tk), 0)
            c = lax.broadcasted_iota(jnp.int32, (tk, tk), 1)
            top = jnp.where(c <= r, s[:tk], -jnp.inf)
            s = top if s.shape[0] == tk else jnp.concatenate([top, s[tk:]], axis=0)
        m_prev = m_scr[rows, hs]
        m_new = jnp.maximum(m_prev, jnp.max(s, axis=1, keepdims=True))
        p = jnp.exp2(s - jnp.concatenate([m_new] * (tk // LANES), axis=1))
        acc_scr[rows, hs] = (jnp.exp2(m_prev - m_new) * acc_scr[rows, hs]
                             + _dot(p.astype(BF16), v_ref[ks, ps] * keep[e] + ones[e]))
        m_scr[rows, hs] = m_new

    nsub = tq // tk
    nq = seq // tq
    nk = seq // tk
    all_rows = slice(0, tq)

    col = lax.broadcasted_iota(jnp.int32, (q_ref.shape[1], LANES), 0) // dh
    head_sel = jnp.where(col == lax.broadcasted_iota(jnp.int32, (q_ref.shape[1], LANES), 1),
                         1.0, 0.0).astype(BF16)

    def max_sq_norm(ref, lo, n):
        x = ref[lo:lo + n, :]
        sq = _dot(x * x, head_sel)
        return jnp.max(sq, axis=0, keepdims=True) * FOX_NORM_SLACK

    k_sq = jnp.concatenate([max_sq_norm(k_ref, j * tk, tk) for j in range(nk)], axis=0)
    q_sq = jnp.concatenate([max_sq_norm(q_ref, i * tq, tq) for i in range(nq)], axis=0)
    blk_id = lax.broadcasted_iota(jnp.int32, (nk, 1), 0).astype(F32)
    for h in range(n_heads):
        k_max = jnp.sqrt(k_sq[:, h:h + 1])
        c_end = ck_ref[0, h, :, tk - 1:tk]
        for i in range(nq):
            first_diag = float(i * nsub)
            q_max = jnp.sqrt(q_sq[i:i + 1, h:h + 1])
            k_own = jnp.max(k_max[i * nsub:(i + 1) * nsub], axis=0, keepdims=True)
            c_start = ck_ref[0, h, i * nsub:i * nsub + 1, 0:1]
            bound = q_max * (k_max + k_own) + (c_start - c_end)
            visit = jnp.logical_and(bound >= -FOX_SKIP_LOG2, blk_id < first_diag)
            first_blk[h * nq + i] = jnp.min(jnp.where(visit, blk_id, first_diag)).astype(jnp.int32)

    def qblock(qi, _):
        qs = pl.ds(pl.multiple_of(qi * tq, tq), tq)
        qp = [q_ref[qs, p * LANES:(p + 1) * LANES] for p in range(n_heads // 2)]
        m_scr[...] = jnp.full(m_scr.shape, -jnp.inf, F32)
        acc_scr[...] = jnp.zeros(acc_scr.shape, F32)

        def full(j, _):
            for h in range(n_heads):
                block(qp[h // 2], j, all_rows, h, False)
            return 0

        starts = [first_blk[h * nq + qi] for h in range(n_heads)]
        common = functools.reduce(jnp.maximum, starts)
        for h in range(n_heads):
            def one(j, _, h=h):
                block(qp[h // 2], j, all_rows, h, False)
                return 0

            def two(t, _, h=h):
                block(qp[h // 2], starts[h] + 2 * t, all_rows, h, False)
                block(qp[h // 2], starts[h] + 2 * t + 1, all_rows, h, False)
                return 0

            pairs = lax.shift_right_logical(common - starts[h], 1)
            lax.fori_loop(0, pairs, two, 0)
            lax.fori_loop(starts[h] + 2 * pairs, common, one, 0)
        lax.fori_loop(common, qi * nsub, full, 0)
        for r in range(nsub):
            rows = slice(r * tk, tq)
            for h in range(n_heads):
                block(qp[h // 2][rows], qi * nsub + r, rows, h, True)
        for p in range(n_heads // 2):
            a0 = acc_scr[:, (2 * p) * LANES:(2 * p + 1) * LANES]
            a1 = acc_scr[:, (2 * p + 1) * LANES:(2 * p + 2) * LANES]
            out = jnp.where(lane < dh, a0 / a0[:, sum_lane[0]:sum_lane[0] + 1],
                            a1 / a1[:, sum_lane[1]:sum_lane[1] + 1])
            o_ref[qs, p * LANES:(p + 1) * LANES] = out.astype(o_ref.dtype)
        return 0

    lax.fori_loop(0, seq // tq, qblock, 0)


def _fox(q, k, v, cum_t, bsz, seq):
    t, w = q.shape
    assert FOX_TQ % FOX_TK == 0 and seq % FOX_TQ == 0 and FOX_HEADS_PER_STEP % 2 == 0
    hps = FOX_HEADS_PER_STEP
    wblk = hps * FOX_HEAD_DIM
    blk = pl.BlockSpec((seq, wblk), lambda bi, hi: (bi, hi))
    return pl.pallas_call(
        _fox_kernel,
        grid=(bsz, w // wblk),
        in_specs=[blk, blk, blk,
                  pl.BlockSpec((1, hps) + cum_t.shape[2:], lambda bi, hi: (bi, hi, 0, 0))],
        out_specs=blk,
        out_shape=jax.ShapeDtypeStruct((t, w), BF16),
        scratch_shapes=[pltpu.VMEM((FOX_TQ, hps * LANES), F32),
                        pltpu.VMEM((FOX_TQ, hps * LANES), F32),
                        pltpu.SMEM((hps * (seq // FOX_TQ),), jnp.int32)],
        compiler_params=_cparams(("arbitrary", "arbitrary")),
        name="fox",
    )(q, k, v, cum_t)


def _post_kernel(x_ref, oa_ref, att_ref, gmix_ref, wg_ref, wfo_ref, wmix_ref, gq_ref, wq_ref,
                 km_ref, vm_ref, wo_ref, gf_ref, win_ref, wout_ref, gfin_ref, o_ref):
    d = x_ref.shape[-1]
    x = x_ref[...]
    gate = jax.nn.sigmoid(_dot(_rms(x, gmix_ref[...]).astype(BF16), wg_ref[...]))
    out_b = _dot(att_ref[...], wfo_ref[...])
    mix = gate[:, :d] * oa_ref[...] + gate[:, d:] * out_b
    h1 = x + _dot(mix.astype(BF16), wmix_ref[...])

    n = _rms(h1, gq_ref[...]).astype(BF16)
    qm = _dot(n, wq_ref[...])
    hd = qm.shape[-1] // MEM_HEADS
    qm = (qm * (hd ** -0.5)).astype(BF16)
    outs = []
    for hh in range(MEM_HEADS):
        hs = slice(hh * hd, (hh + 1) * hd)
        s = _dot_nt(qm[:, hs], km_ref[0, :, hs])
        s = s - jnp.max(s, axis=-1, keepdims=True)
        p = jnp.exp(s)
        p = p / jnp.sum(p, axis=-1, keepdims=True)
        outs.append(_dot(p.astype(BF16), vm_ref[0, :, hs]))
    o = jnp.concatenate(outs, axis=-1).astype(BF16)
    h2 = h1 + _dot(o, wo_ref[...])

    f = _rms(h2, gf_ref[...]).astype(BF16)
    hidden = wout_ref.shape[0]
    acc = jnp.zeros_like(h2)
    for lo, hi in _ffn_chunks(hidden):
        fa = _dot(f, win_ref[:, lo:hi])
        fb = _dot(f, win_ref[:, hidden + lo:hidden + hi])
        g = (fa * jax.nn.sigmoid(fa) * fb).astype(BF16)
        acc = acc + _dot(g, wout_ref[lo:hi, :])
    o_ref[...] = _rms(h2 + acc, gfin_ref[...])


def _ffn_chunks(hidden):
    tiles = hidden // MXU_TILE
    assert tiles * MXU_TILE == hidden
    cuts = [-(-tiles * c // FFN_CHUNKS) * MXU_TILE for c in range(FFN_CHUNKS + 1)]
    return list(zip(cuts[:-1], cuts[1:]))


def _resident(shape):
    n = len(shape)
    return pl.BlockSpec(shape, lambda *_: (0,) * n, pipeline_mode=pl.Buffered(1))


def _post(x2, out_a, att, gain_mix, w_gate, w_fox_o, w_mix, gain_q, w_q, k_m, v_m, w_o,
          gain_f, w_ffn_in, w_ffn_out, gain_fin, bsz, seq):
    t, d = x2.shape
    tm = TM_POST
    nb = seq // tm
    row = lambda w: pl.BlockSpec((tm, w), lambda b, i: (b * nb + i, 0))
    mem = pl.BlockSpec((1,) + k_m.shape[1:], lambda b, i: (b, 0, 0))
    gain = lambda g: g.reshape(1, d)
    weights = lambda *ws: [_resident(w.shape) for w in ws]
    return pl.pallas_call(
        _post_kernel,
        grid=(bsz, nb),
        in_specs=[row(d), row(d), row(att.shape[1]), _full((1, d))]
                 + weights(w_gate, w_fox_o, w_mix) + [_full((1, d))] + weights(w_q)
                 + [mem, mem] + weights(w_o) + [_full((1, d))] + weights(w_ffn_in, w_ffn_out)
                 + [_full((1, d))],
        out_specs=row(d),
        out_shape=jax.ShapeDtypeStruct((t, d), F32),
        compiler_params=_cparams(("arbitrary", "arbitrary")),
        name="post",
    )(x2, out_a, att, gain(gain_mix), w_gate, w_fox_o, w_mix, gain(gain_q), w_q, k_m, v_m, w_o,
      gain(gain_f), w_ffn_in, w_ffn_out, gain(gain_fin))


def _block_diag(blocks, per):
    g, r, c = blocks.shape
    b = blocks.reshape(g // per, per, r, c)
    eye = jnp.eye(per, dtype=blocks.dtype)
    out = b[:, :, :, None, :] * eye[None, :, None, :, None]
    return out.reshape(g // per, per * r, per * c)


def _seg_perm(q):
    seg = q // SUBLANES
    r = jnp.arange(q)
    src = (r % SUBLANES) * seg + r // SUBLANES
    return (src[:, None] == jnp.arange(q)[None, :]).astype(BF16)


def _layer(x2, mem, bsz, seq, norm_mix, w_in, b_forget, lam_re, lam_im, log_dt, b_re, b_im, c_re,
           c_im, d_skip, w_glu, w_fox_o, w_mix_out, norm_mem_q, norm_mem_kv, w_mem_q, w_mem_kv,
           w_mem_o, norm_ffn, w_ffn_in, w_ffn_out, norm_final):
    d = x2.shape[-1]
    groups, states = lam_re.shape
    ssm_w = groups * SSM_GROUP
    n_heads = b_forget.shape[0]
    fox_w = n_heads * FOX_HEAD_DIM

    a_re, a_im, s_re, s_im, bb_re, bb_im = _ssm_prep(lam_re, lam_im, log_dt, b_re, b_im)
    nstate = groups * states
    per = LANES // SSM_GROUP
    to_rows = lambda a: a.reshape(1, nstate)
    bd_in = lambda bb: _block_diag(
        bb.reshape(SSM_GROUP, groups, states).transpose(1, 0, 2), per).astype(BF16)
    bd_out = lambda c: _block_diag(c.transpose(0, 2, 1), per).astype(BF16)
    perm = _seg_perm(SSM_CHUNK)

    bf = jnp.zeros((1, LANES), F32).at[0, :n_heads].set(b_forget)
    ssm_consts = [perm, perm.T, bd_in(bb_re), bd_in(bb_im), to_rows(a_re), to_rows(a_im),
                  to_rows(s_re), to_rows(s_im), bd_out(c_re), bd_out(c_im),
                  d_skip.reshape(1, ssm_w), w_glu.astype(BF16)]
    q, k, v, cum_t, out_a, w_gate = _in_ssm(x2, norm_mix, w_in, bf, ssm_consts, ssm_w, fox_w,
                                            n_heads, bsz, seq)

    att = _fox(q, k, v, cum_t.reshape(bsz, n_heads, seq // FOX_TK, FOX_TK), bsz, seq)

    k_m, v_m = _mem_kv(mem, norm_mem_kv, w_mem_kv)
    bf16 = lambda w: w.astype(BF16)
    return _post(x2, out_a, att, norm_mix, w_gate, bf16(w_fox_o), bf16(w_mix_out),
                 norm_mem_q, bf16(w_mem_q), k_m, v_m, bf16(w_mem_o), norm_ffn, bf16(w_ffn_in),
                 bf16(w_ffn_out), norm_final, bsz, seq)


def kernel(x, mem, norm_mix, w_in, b_forget, lam_re, lam_im, log_dt, b_re, b_im, c_re, c_im, d_skip,
           w_glu, w_fox_o, w_mix_out, norm_mem_q, norm_mem_kv, w_mem_q, w_mem_kv, w_mem_o, norm_ffn,
           w_ffn_in, w_ffn_out, norm_final):
    bsz, seq, d = x.shape
    assert w_in.shape[0] == 1, "single-layer block"
    out = _layer(x.reshape(bsz * seq, d), mem, bsz, seq, norm_mix[0], w_in[0], b_forget[0],
                 lam_re[0], lam_im[0], log_dt[0], b_re[0], b_im[0], c_re[0], c_im[0], d_skip[0],
                 w_glu[0], w_fox_o[0], w_mix_out[0], norm_mem_q[0], norm_mem_kv[0], w_mem_q[0],
                 w_mem_kv[0], w_mem_o[0], norm_ffn[0], w_ffn_in[0], w_ffn_out[0], norm_final)
    return out.reshape(bsz, seq, d)
```

```python
import functools
import math

import jax
import jax.numpy as jnp
from jax import lax
from jax.experimental import pallas as pl
from jax.experimental.pallas import tpu as pltpu

F32 = jnp.float32
BF16 = jnp.bfloat16

RMS_EPS = 1e-6
LOG2E = math.log2(math.e)
SSM_GROUP = 16
SSM_STATE = 64
FOX_HEAD_DIM = 64
MEM_HEADS = 4
LANES = 128
SUBLANES = 8
MXU_TILE = 256
VMEM_LIMIT = 56 * 1024 * 1024

TM_PROJ = 512
TM_POST = 512
FFN_CHUNKS = 2
SSM_CHUNK = 256
SSM_SEG = SSM_CHUNK // SUBLANES
SSM_COLS = 512
FOX_TQ = 1024
FOX_TK = 512
FOX_HEADS_PER_STEP = 4
FOX_SKIP_LOG2 = 160.0
FOX_NORM_SLACK = 1.02


def _cparams(sem):
    return pltpu.CompilerParams(dimension_semantics=sem, vmem_limit_bytes=VMEM_LIMIT)


def _rms(x, gain):
    return x * lax.rsqrt(jnp.mean(x * x, axis=-1, keepdims=True) + RMS_EPS) * gain


def _dot(a, b):
    return jnp.dot(a, b, preferred_element_type=F32)


def _dot_nt(a, b):
    return lax.dot_general(a, b, (((1,), (1,)), ((), ())), preferred_element_type=F32)


def _full(shape):
    n = len(shape)
    return pl.BlockSpec(shape, lambda *_: (0,) * n)


def _ssm_prep_kernel(lre_ref, lim_ref, ldt_ref, bre_ref, bim_ref,
                     are_ref, aim_ref, sre_ref, sim_ref, bbre_ref, bbim_ref):
    lre = lre_ref[...]
    lim = lim_ref[...]
    dt = jnp.exp(ldt_ref[...])
    zr = lre * dt
    zi = lim * dt
    mag = jnp.exp(zr)
    are = mag * jnp.cos(zi)
    aim = mag * jnp.sin(zi)
    are_ref[...] = are
    aim_ref[...] = aim
    mag_s = jnp.exp(zr * SSM_SEG)
    sre_ref[...] = mag_s * jnp.cos(zi * SSM_SEG)
    sim_ref[...] = mag_s * jnp.sin(zi * SSM_SEG)
    nr = are - 1.0
    ni = aim
    den = lre * lre + lim * lim
    fr = (nr * lre + ni * lim) / den
    fi = (ni * lre - nr * lim) / den
    bre = bre_ref[...]
    bim = bim_ref[...]
    bbre_ref[...] = fr * bre - fi * bim
    bbim_ref[...] = fr * bim + fi * bre


def _ssm_prep(lam_re, lam_im, log_dt, b_re, b_im):
    g, p = lam_re.shape
    n = b_re.shape[-1]
    c = g * p
    row = lambda a: a.reshape(1, c)
    chan = lambda b: b.transpose(2, 0, 1).reshape(n, c)
    ldt = jnp.broadcast_to(log_dt[:, None], (g, p))
    outs = pl.pallas_call(
        _ssm_prep_kernel,
        out_shape=[jax.ShapeDtypeStruct((1, c), F32)] * 4 + [jax.ShapeDtypeStruct((n, c), F32)] * 2,
        name="ssm_prep",
    )(row(lam_re), row(lam_im), row(ldt), chan(b_re), chan(b_im))
    return outs


def _mem_kv_kernel(m_ref, g_ref, w_ref, k_ref, v_ref):
    n = _rms(m_ref[0], g_ref[...]).astype(BF16)
    kv = _dot(n, w_ref[...])
    half = kv.shape[-1] // 2
    k_ref[0] = kv[:, :half].astype(BF16)
    v_ref[0] = kv[:, half:].astype(BF16)


def _mem_kv(mem, gain, w_kv):
    b, m, d = mem.shape
    w2 = w_kv.shape[-1]
    return pl.pallas_call(
        _mem_kv_kernel,
        grid=(b,),
        in_specs=[pl.BlockSpec((1, m, d), lambda i: (i, 0, 0)), _full((1, d)), _full((d, w2))],
        out_specs=[pl.BlockSpec((1, m, w2 // 2), lambda i: (i, 0, 0))] * 2,
        out_shape=[jax.ShapeDtypeStruct((b, m, w2 // 2), BF16)] * 2,
        compiler_params=_cparams(("arbitrary",)),
        name="mem_kv",
    )(mem, gain.reshape(1, d), w_kv.astype(BF16))


def _gelu_tanh(x):
    return 0.5 * x * (1.0 + jnp.tanh(math.sqrt(2.0 / math.pi) * (x + 0.044715 * (x * x * x))))


def _ssm_chunk(u, perm_ref, permt_ref, bre_ref, bim_ref, are_ref, aim_ref, sre_ref, sim_ref,
               cre_ref, cim_ref, d_ref, wglu_ref, hre, him, car_re, car_im, anchors=()):
    up = _dot(perm_ref[...], u.astype(BF16)).astype(BF16)
    nkb = bre_ref.shape[0]
    kw = bre_ref.shape[1]
    nw = bre_ref.shape[2]
    for kb in range(nkb):
        ukb = up[:, kb * kw:(kb + 1) * kw]
        hre[:, kb * nw:(kb + 1) * nw] = _dot(ukb, bre_ref[kb])
        him[:, kb * nw:(kb + 1) * nw] = _dot(ukb, bim_ref[kb])
    if callable(anchors):
        anchors = anchors()

    nstate = hre.shape[1]
    for cb in range(nstate // SSM_COLS):
        cs = slice(cb * SSM_COLS, (cb + 1) * SSM_COLS)
        a_re = jnp.broadcast_to(are_ref[:, cs], (SUBLANES, SSM_COLS))
        a_im = jnp.broadcast_to(aim_ref[:, cs], (SUBLANES, SSM_COLS))

        e_re = hre[0:SUBLANES, cs]
        e_im = him[0:SUBLANES, cs]
        for k in range(1, SSM_SEG):
            r = slice(k * SUBLANES, (k + 1) * SUBLANES)
            e_re, e_im = (a_re * e_re - a_im * e_im + hre[r, cs],
                          a_re * e_im + a_im * e_re + him[r, cs])
            hre[r, cs] = e_re
            him[r, cs] = e_im

        s_re = sre_ref[:, cs]
        s_im = sim_ref[:, cs]
        c_re = car_re[:, cs]
        c_im = car_im[:, cs]
        rows_re, rows_im = [], []
        for j in range(SUBLANES):
            rows_re.append(c_re)
            rows_im.append(c_im)
            n_re = s_re * c_re - s_im * c_im + e_re[j:j + 1]
            n_im = s_re * c_im + s_im * c_re + e_im[j:j + 1]
            c_re, c_im = n_re, n_im
        car_re[:, cs] = c_re
        car_im[:, cs] = c_im

        d_re = jnp.concatenate(rows_re, axis=0)
        d_im = jnp.concatenate(rows_im, axis=0)
        for k in range(SSM_SEG):
            r = slice(k * SUBLANES, (k + 1) * SUBLANES)
            d_re, d_im = a_re * d_re - a_im * d_im, a_re * d_im + a_im * d_re
            if k == SSM_SEG // 2:
                for z in anchors[cb::nstate // SSM_COLS]:
                    d_re = d_re + jnp.concatenate([z] * (SSM_COLS // LANES), axis=1)
            hre[r, cs] = hre[r, cs] + d_re
            him[r, cs] = him[r, cs] + d_im

    ncb = cre_ref.shape[0]
    cw = cre_ref.shape[1]
    ys = []
    for kb in range(ncb):
        h_re = hre[:, kb * cw:(kb + 1) * cw].astype(BF16)
        h_im = him[:, kb * cw:(kb + 1) * cw].astype(BF16)
        ys.append(_dot(h_re, cre_ref[kb]) - _dot(h_im, cim_ref[kb]))
    yp = jnp.concatenate(ys, axis=-1)
    hi = yp.astype(BF16)
    lo = (yp - hi.astype(F32)).astype(BF16)
    y = _dot(permt_ref[...], hi) + _dot(permt_ref[...], lo)
    y = y + d_ref[...] * u
    z = _dot(_gelu_tanh(y).astype(BF16), wglu_ref[...])
    half = z.shape[-1] // 2
    return z[:, :half] * jax.nn.sigmoid(z[:, half:])


def _schedule_anchor(x):
    rows, cols = x.shape
    s = x[:, :LANES]
    for c in range(1, cols // LANES):
        s = s + x[:, c * LANES:(c + 1) * LANES]
    s = jnp.sum(s.reshape(rows // SUBLANES, SUBLANES, LANES), axis=0)
    bits = pltpu.bitcast(s, jnp.uint32)
    return ((bits >> 16) >> 16).astype(F32)


def _in_ssm_kernel(x_ref, g_ref, wa_ref, wb_ref, wc_ref, bf_ref, *rest, n_cast):
    nconst = 12
    ssm_refs = rest[:nconst]
    cast_in = rest[nconst:nconst + n_cast]
    outs = rest[nconst + n_cast:]
    q_ref, k_ref, v_ref, cum_ref, oa_ref, wg_ref = outs[:6]
    cast_out = outs[6:6 + n_cast]
    carry_ref, car_re, car_im, w_bf, wf_bf = outs[6 + n_cast:11 + n_cast]
    h_scr = outs[11 + n_cast:]
    n_forget = cum_ref.shape[1]
    for src, dst in zip(cast_in, cast_out):
        dst[...] = src[...].astype(BF16)

    @pl.when(pl.program_id(1) == 0)
    def _():
        carry_ref[...] = jnp.zeros_like(carry_ref)
        car_re[...] = jnp.zeros_like(car_re)
        car_im[...] = jnp.zeros_like(car_im)
        w_bf[...] = wa_ref[...].astype(BF16)
        wf_bf[...] = wb_ref[:, :LANES].astype(BF16)
        wg_ref[...] = jnp.concatenate([wb_ref[:, n_forget:], wc_ref[:, :n_forget]],
                                      axis=1).astype(BF16)

    un = _rms(x_ref[...], g_ref[...]).astype(BF16)
    wu = ssm_refs[10].shape[1]
    wh = q_ref.shape[1]
    col = lambda lo, hi: _dot(un, w_bf[:, lo:hi])
    u = col(0, wu)
    pieces = [(ref, lo + p * MXU_TILE, p * MXU_TILE, scale)
              for ref, lo, scale in ((q_ref, wu, FOX_HEAD_DIM ** -0.5 * LOG2E),
                                     (k_ref, wu + wh, None), (v_ref, wu + 2 * wh, None))
              for p in range(wh // MXU_TILE)]
    n_chunks = u.shape[0] // SSM_CHUNK
    per_chunk = -(-len(pieces) // n_chunks)

    def project(todo):
        anchors = []
        for ref, src, dst, scale in todo:
            piece = col(src, src + MXU_TILE)
            if scale is not None:
                piece = piece * scale
            ref[:, dst:dst + MXU_TILE] = piece.astype(BF16)
            anchors.append(_schedule_anchor(piece))
        return anchors

    for c in range(n_chunks):
        rows = slice(c * SSM_CHUNK, (c + 1) * SSM_CHUNK)
        todo = pieces[c * per_chunk:(c + 1) * per_chunk]
        oa_ref[rows, :] = _ssm_chunk(u[rows], *ssm_refs, h_scr[2 * c], h_scr[2 * c + 1],
                                     car_re, car_im, functools.partial(project, todo))
    c = jax.nn.log_sigmoid(_dot(un, wf_bf[...]) + bf_ref[...])
    rows = c.shape[0]
    row = lax.broadcasted_iota(jnp.int32, c.shape, 0)
    shift = 1
    while shift < rows:
        c = c + jnp.where(row >= shift, pltpu.roll(c, shift, 0), 0.0)
        shift *= 2
    c = c + carry_ref[...]
    carry_ref[...] = c[rows - 1:rows, :]
    cum_ref[0] = (c * LOG2E).T[:cum_ref.shape[1], :]


def _in_ssm(x2, gain, w_in, bf, ssm_consts, later_weights, ssm_w, fox_w, n_heads, bsz, seq):
    t, d = x2.shape
    tm = TM_PROJ
    nb = seq // tm
    steps = bsz * nb
    slabs = [w.reshape(steps, w.shape[0] // steps, w.shape[1]) for w in later_weights]
    slab = lambda s: pl.BlockSpec((1,) + s.shape[1:], lambda b, i: (b * nb + i, 0, 0))
    assert tm % SSM_CHUNK == 0 and len(ssm_consts) == 12
    nstate = ssm_consts[4].shape[1]
    dm = ssm_consts[11].shape[1] // 2
    main_w = ssm_w + 3 * fox_w
    gate_w = w_in.shape[1] - main_w - n_heads
    assert gate_w == main_w and main_w % LANES == 0 and n_heads <= LANES
    win = lambda width, idx: pl.BlockSpec((d, width), lambda *_: (0, idx),
                                          pipeline_mode=pl.Buffered(1))
    row = lambda w: pl.BlockSpec((tm, w), lambda b, i: (b * nb + i, 0))
    h_scr = [pltpu.VMEM((SSM_CHUNK, nstate), F32)] * (2 * (tm // SSM_CHUNK))
    outs = pl.pallas_call(
        functools.partial(_in_ssm_kernel, n_cast=len(slabs)),
        grid=(bsz, nb),
        in_specs=[row(d), _full((1, d)), win(main_w, 0), win(main_w, 1),
                  win(LANES, 2 * main_w // LANES), _full((1, LANES))]
                 + [_full(c.shape) for c in ssm_consts] + [slab(s) for s in slabs],
        out_specs=[row(fox_w), row(fox_w), row(fox_w),
                   pl.BlockSpec((1, n_heads, tm), lambda b, i: (b, 0, i)), row(dm),
                   _full((d, gate_w))] + [slab(s) for s in slabs],
        out_shape=[jax.ShapeDtypeStruct((t, fox_w), BF16),
                   jax.ShapeDtypeStruct((t, fox_w), BF16),
                   jax.ShapeDtypeStruct((t, fox_w), BF16),
                   jax.ShapeDtypeStruct((bsz, n_heads, seq), F32),
                   jax.ShapeDtypeStruct((t, dm), F32),
                   jax.ShapeDtypeStruct((d, gate_w), BF16)]
                  + [jax.ShapeDtypeStruct(s.shape, BF16) for s in slabs],
        scratch_shapes=[pltpu.VMEM((1, LANES), F32), pltpu.VMEM((1, nstate), F32),
                        pltpu.VMEM((1, nstate), F32), pltpu.VMEM((d, main_w), BF16),
                        pltpu.VMEM((d, LANES), BF16)] + h_scr,
        compiler_params=_cparams(("arbitrary", "arbitrary")),
        name="in_ssm",
    )(x2, gain.reshape(1, d), w_in, w_in, w_in, bf, *ssm_consts, *slabs)
    cast = [o.reshape(w.shape) for o, w in zip(outs[6:], later_weights)]
    return list(outs[:6]) + cast


def _fox_kernel(q_ref, k_ref, v_ref, ck_ref, o_ref, m_scr, acc_scr, first_blk):
    seq = q_ref.shape[0]
    tq = FOX_TQ
    tk = FOX_TK
    dh = FOX_HEAD_DIM
    n_heads = ck_ref.shape[1]
    lane = lax.broadcasted_iota(jnp.int32, (1, LANES), 1)
    keep = [jnp.where(lane < dh, 1.0, 0.0).astype(BF16), jnp.where(lane >= dh, 1.0, 0.0).astype(BF16)]
    sum_lane = [dh, 0]
    ones = [jnp.where(lane == sl, 1.0, 0.0).astype(BF16) for sl in sum_lane]

    def block(q, j, rows, h, masked):
        e = h % 2
        ps = slice((h // 2) * LANES, (h // 2 + 1) * LANES)
        hs = slice(h * LANES, (h + 1) * LANES)
        ks = pl.ds(pl.multiple_of(j * tk, tk), tk)
        s = _dot_nt(q, k_ref[ks, ps] * keep[e]) - ck_ref[0, h, pl.ds(j, 1), :]
        if masked:
            r = lax.broadcasted_iota(jnp.int32, (tk, tk), 0)
            c = lax.broadcasted_iota(jnp.int32, (tk, tk), 1)
            top = jnp.where(c <= r, s[:tk], -jnp.inf)
            s = top if s.shape[0] == tk else jnp.concatenate([top, s[tk:]], axis=0)
        m_prev = m_scr[rows, hs]
        m_new = jnp.maximum(m_prev, jnp.max(s, axis=1, keepdims=True))
        p = jnp.exp2(s - jnp.concatenate([m_new] * (tk // LANES), axis=1))
        acc_scr[rows, hs] = (jnp.exp2(m_prev - m_new) * acc_scr[rows, hs]
                             + _dot(p.astype(BF16), v_ref[ks, ps] * keep[e] + ones[e]))
        m_scr[rows, hs] = m_new

    nsub = tq // tk
    nq = seq // tq
    nk = seq // tk
    all_rows = slice(0, tq)

    col = lax.broadcasted_iota(jnp.int32, (q_ref.shape[1], LANES), 0) // dh
    head_sel = jnp.where(col == lax.broadcasted_iota(jnp.int32, (q_ref.shape[1], LANES), 1),
                         1.0, 0.0).astype(BF16)

    def max_sq_norm(ref, lo, n):
        x = ref[lo:lo + n, :]
        sq = _dot(x * x, head_sel)
        return jnp.max(sq, axis=0, keepdims=True) * FOX_NORM_SLACK

    k_sq = jnp.concatenate([max_sq_norm(k_ref, j * tk, tk) for j in range(nk)], axis=0)
    q_sq = jnp.concatenate([max_sq_norm(q_ref, i * tq, tq) for i in range(nq)], axis=0)
    blk_id = lax.broadcasted_iota(jnp.int32, (nk, 1), 0).astype(F32)
    for h in range(n_heads):
        k_max = jnp.sqrt(k_sq[:, h:h + 1])
        c_end = ck_ref[0, h, :, tk - 1:tk]
        for i in range(nq):
            first_diag = float(i * nsub)
            q_max = jnp.sqrt(q_sq[i:i + 1, h:h + 1])
            k_own = jnp.max(k_max[i * nsub:(i + 1) * nsub], axis=0, keepdims=True)
            c_start = ck_ref[0, h, i * nsub:i * nsub + 1, 0:1]
            bound = q_max * (k_max + k_own) + (c_start - c_end)
            visit = jnp.logical_and(bound >= -FOX_SKIP_LOG2, blk_id < first_diag)
            first_blk[h * nq + i] = jnp.min(jnp.where(visit, blk_id, first_diag)).astype(jnp.int32)

    def qblock(qi, _):
        qs = pl.ds(pl.multiple_of(qi * tq, tq), tq)
        qp = [q_ref[qs, p * LANES:(p + 1) * LANES] for p in range(n_heads // 2)]
        m_scr[...] = jnp.full(m_scr.shape, -jnp.inf, F32)
        acc_scr[...] = jnp.zeros(acc_scr.shape, F32)

        def full(j, _):
            for h in range(n_heads):
                block(qp[h // 2], j, all_rows, h, False)
            return 0

        starts = [first_blk[h * nq + qi] for h in range(n_heads)]
        common = functools.reduce(jnp.maximum, starts)
        for h in range(n_heads):
            def one(j, _, h=h):
                block(qp[h // 2], j, all_rows, h, False)
                return 0

            def two(t, _, h=h):
                block(qp[h // 2], starts[h] + 2 * t, all_rows, h, False)
                block(qp[h // 2], starts[h] + 2 * t + 1, all_rows, h, False)
                return 0

            pairs = lax.shift_right_logical(common - starts[h], 1)
            lax.fori_loop(0, pairs, two, 0)
            lax.fori_loop(starts[h] + 2 * pairs, common, one, 0)
        lax.fori_loop(common, qi * nsub, full, 0)
        for r in range(nsub):
            rows = slice(r * tk, tq)
            for h in range(n_heads):
                block(qp[h // 2][rows], qi * nsub + r, rows, h, True)
        for p in range(n_heads // 2):
            a0 = acc_scr[:, (2 * p) * LANES:(2 * p + 1) * LANES]
            a1 = acc_scr[:, (2 * p + 1) * LANES:(2 * p + 2) * LANES]
            out = jnp.where(lane < dh, a0 / a0[:, sum_lane[0]:sum_lane[0] + 1],
                            a1 / a1[:, sum_lane[1]:sum_lane[1] + 1])
            o_ref[qs, p * LANES:(p + 1) * LANES] = out.astype(o_ref.dtype)
        return 0

    lax.fori_loop(0, seq // tq, qblock, 0)


def _fox(q, k, v, cum_t, bsz, seq):
    t, w = q.shape
    assert FOX_TQ % FOX_TK == 0 and seq % FOX_TQ == 0 and FOX_HEADS_PER_STEP % 2 == 0
    hps = FOX_HEADS_PER_STEP
    wblk = hps * FOX_HEAD_DIM
    blk = pl.BlockSpec((seq, wblk), lambda bi, hi: (bi, hi))
    return pl.pallas_call(
        _fox_kernel,
        grid=(bsz, w // wblk),
        in_specs=[blk, blk, blk,
                  pl.BlockSpec((1, hps) + cum_t.shape[2:], lambda bi, hi: (bi, hi, 0, 0))],
        out_specs=blk,
        out_shape=jax.ShapeDtypeStruct((t, w), BF16),
        scratch_shapes=[pltpu.VMEM((FOX_TQ, hps * LANES), F32),
                        pltpu.VMEM((FOX_TQ, hps * LANES), F32),
                        pltpu.SMEM((hps * (seq // FOX_TQ),), jnp.int32)],
        compiler_params=_cparams(("arbitrary", "arbitrary")),
        name="fox",
    )(q, k, v, cum_t)


def _post_kernel(x_ref, oa_ref, att_ref, gmix_ref, wg_ref, wfo_ref, wmix_ref, gq_ref, wq_ref,
                 km_ref, vm_ref, wo_ref, gf_ref, win_ref, wout_ref, gfin_ref, o_ref):
    d = x_ref.shape[-1]
    x = x_ref[...]
    gate = jax.nn.sigmoid(_dot(_rms(x, gmix_ref[...]).astype(BF16), wg_ref[...]))
    out_b = _dot(att_ref[...], wfo_ref[...])
    mix = gate[:, :d] * oa_ref[...] + gate[:, d:] * out_b
    h1 = x + _dot(mix.astype(BF16), wmix_ref[...])

    n = _rms(h1, gq_ref[...]).astype(BF16)
    qm = _dot(n, wq_ref[...])
    hd = qm.shape[-1] // MEM_HEADS
    qm = (qm * (hd ** -0.5)).astype(BF16)
    outs = []
    for hh in range(MEM_HEADS):
        hs = slice(hh * hd, (hh + 1) * hd)
        s = _dot_nt(qm[:, hs], km_ref[0, :, hs])
        s = s - jnp.max(s, axis=-1, keepdims=True)
        p = jnp.exp(s)
        p = p / jnp.sum(p, axis=-1, keepdims=True)
        outs.append(_dot(p.astype(BF16), vm_ref[0, :, hs]))
    o = jnp.concatenate(outs, axis=-1).astype(BF16)
    h2 = h1 + _dot(o, wo_ref[...])

    f = _rms(h2, gf_ref[...]).astype(BF16)
    hidden = wout_ref.shape[0]
    acc = jnp.zeros_like(h2)
    for lo, hi in _ffn_chunks(hidden):
        fa = _dot(f, win_ref[:, lo:hi])
        fb = _dot(f, win_ref[:, hidden + lo:hidden + hi])
        g = (fa * jax.nn.sigmoid(fa) * fb).astype(BF16)
        acc = acc + _dot(g, wout_ref[lo:hi, :])
    o_ref[...] = _rms(h2 + acc, gfin_ref[...])


def _ffn_chunks(hidden):
    tiles = hidden // MXU_TILE
    assert tiles * MXU_TILE == hidden
    cuts = [-(-tiles * c // FFN_CHUNKS) * MXU_TILE for c in range(FFN_CHUNKS + 1)]
    return list(zip(cuts[:-1], cuts[1:]))


def _resident(shape):
    n = len(shape)
    return pl.BlockSpec(shape, lambda *_: (0,) * n, pipeline_mode=pl.Buffered(1))


def _post(x2, out_a, att, gain_mix, w_gate, w_fox_o, w_mix, gain_q, w_q, k_m, v_m, w_o,
          gain_f, w_ffn_in, w_ffn_out, gain_fin, bsz, seq):
    t, d = x2.shape
    tm = TM_POST
    nb = seq // tm
    row = lambda w: pl.BlockSpec((tm, w), lambda b, i: (b * nb + i, 0))
    mem = pl.BlockSpec((1,) + k_m.shape[1:], lambda b, i: (b, 0, 0))
    gain = lambda g: g.reshape(1, d)
    weights = lambda *ws: [_resident(w.shape) for w in ws]
    return pl.pallas_call(
        _post_kernel,
        grid=(bsz, nb),
        in_specs=[row(d), row(d), row(att.shape[1]), _full((1, d))]
                 + weights(w_gate, w_fox_o, w_mix) + [_full((1, d))] + weights(w_q)
                 + [mem, mem] + weights(w_o) + [_full((1, d))] + weights(w_ffn_in, w_ffn_out)
                 + [_full((1, d))],
        out_specs=row(d),
        out_shape=jax.ShapeDtypeStruct((t, d), F32),
        compiler_params=_cparams(("arbitrary", "arbitrary")),
        name="post",
    )(x2, out_a, att, gain(gain_mix), w_gate, w_fox_o, w_mix, gain(gain_q), w_q, k_m, v_m, w_o,
      gain(gain_f), w_ffn_in, w_ffn_out, gain(gain_fin))


def _block_diag(blocks, per):
    g, r, c = blocks.shape
    b = blocks.reshape(g // per, per, r, c)
    eye = jnp.eye(per, dtype=blocks.dtype)
    out = b[:, :, :, None, :] * eye[None, :, None, :, None]
    return out.reshape(g // per, per * r, per * c)


def _seg_perm(q):
    seg = q // SUBLANES
    r = jnp.arange(q)
    src = (r % SUBLANES) * seg + r // SUBLANES
    return (src[:, None] == jnp.arange(q)[None, :]).astype(BF16)


def _layer(x2, mem, bsz, seq, norm_mix, w_in, b_forget, lam_re, lam_im, log_dt, b_re, b_im, c_re,
           c_im, d_skip, w_glu, w_fox_o, w_mix_out, norm_mem_q, norm_mem_kv, w_mem_q, w_mem_kv,
           w_mem_o, norm_ffn, w_ffn_in, w_ffn_out, norm_final):
    d = x2.shape[-1]
    groups, states = lam_re.shape
    ssm_w = groups * SSM_GROUP
    n_heads = b_forget.shape[0]
    fox_w = n_heads * FOX_HEAD_DIM

    a_re, a_im, s_re, s_im, bb_re, bb_im = _ssm_prep(lam_re, lam_im, log_dt, b_re, b_im)
    nstate = groups * states
    per = LANES // SSM_GROUP
    to_rows = lambda a: a.reshape(1, nstate)
    bd_in = lambda bb: _block_diag(
        bb.reshape(SSM_GROUP, groups, states).transpose(1, 0, 2), per).astype(BF16)
    bd_out = lambda c: _block_diag(c.transpose(0, 2, 1), per).astype(BF16)
    perm = _seg_perm(SSM_CHUNK)

    bf = jnp.zeros((1, LANES), F32).at[0, :n_heads].set(b_forget)
    ssm_consts = [perm, perm.T, bd_in(bb_re), bd_in(bb_im), to_rows(a_re), to_rows(a_im),
                  to_rows(s_re), to_rows(s_im), bd_out(c_re), bd_out(c_im),
                  d_skip.reshape(1, ssm_w), w_glu.astype(BF16)]
    post_weights = [w_fox_o, w_mix_out, w_mem_q, w_mem_o, w_ffn_in, w_ffn_out]
    q, k, v, cum_t, out_a, w_gate, *post_bf16 = _in_ssm(
        x2, norm_mix, w_in, bf, ssm_consts, post_weights, ssm_w, fox_w, n_heads, bsz, seq)
    wb_fox_o, wb_mix, wb_mem_q, wb_mem_o, wb_ffn_in, wb_ffn_out = post_bf16

    att = _fox(q, k, v, cum_t.reshape(bsz, n_heads, seq // FOX_TK, FOX_TK), bsz, seq)

    k_m, v_m = _mem_kv(mem, norm_mem_kv, w_mem_kv)
    return _post(x2, out_a, att, norm_mix, w_gate, wb_fox_o, wb_mix, norm_mem_q, wb_mem_q, k_m, v_m,
                 wb_mem_o, norm_ffn, wb_ffn_in, wb_ffn_out, norm_final, bsz, seq)


def kernel(x, mem, norm_mix, w_in, b_forget, lam_re, lam_im, log_dt, b_re, b_im, c_re, c_im, d_skip,
           w_glu, w_fox_o, w_mix_out, norm_mem_q, norm_mem_kv, w_mem_q, w_mem_kv, w_mem_o, norm_ffn,
           w_ffn_in, w_ffn_out, norm_final):
    bsz, seq, d = x.shape
    assert w_in.shape[0] == 1, "single-layer block"
    out = _layer(x.reshape(bsz * seq, d), mem, bsz, seq, norm_mix[0], w_in[0], b_forget[0],
                 lam_re[0], lam_im[0], log_dt[0], b_re[0], b_im[0], c_re[0], c_im[0], d_skip[0],
                 w_glu[0], w_fox_o[0], w_mix_out[0], norm_mem_q[0], norm_mem_kv[0], w_mem_q[0],
                 w_mem_kv[0], w_mem_o[0], norm_ffn[0], w_ffn_in[0], w_ffn_out[0], norm_final)
    return out.reshape(bsz, seq, d)
```

```python
import functools
import math

import jax
import jax.numpy as jnp
from jax import lax
from jax.experimental import pallas as pl
from jax.experimental.pallas import tpu as pltpu

F32 = jnp.float32
BF16 = jnp.bfloat16

RMS_EPS = 1e-6
LOG2E = math.log2(math.e)
SSM_GROUP = 16
SSM_STATE = 64
FOX_HEAD_DIM = 64
MEM_HEADS = 4
LANES = 128
SUBLANES = 8
MXU_TILE = 256
VMEM_LIMIT = 56 * 1024 * 1024

TM_PROJ = 512
TM_POST = 512
FFN_CHUNKS = 2
SSM_CHUNK = 256
SSM_SEG = SSM_CHUNK // SUBLANES
SSM_COLS = 512
FOX_TQ = 1024
FOX_TK = 512
FOX_HEADS_PER_STEP = 4
FOX_SKIP_LOG2 = 160.0
FOX_NORM_SLACK = 1.02


def _cparams(sem):
    return pltpu.CompilerParams(dimension_semantics=sem, vmem_limit_bytes=VMEM_LIMIT)


def _rms(x, gain):
    return x * lax.rsqrt(jnp.mean(x * x, axis=-1, keepdims=True) + RMS_EPS) * gain


def _dot(a, b):
    return jnp.dot(a, b, preferred_element_type=F32)


def _dot_nt(a, b):
    return lax.dot_general(a, b, (((1,), (1,)), ((), ())), preferred_element_type=F32)


def _full(shape):
    n = len(shape)
    return pl.BlockSpec(shape, lambda *_: (0,) * n)


def _ssm_prep_kernel(lre_ref, lim_ref, ldt_ref, bre_ref, bim_ref,
                     are_ref, aim_ref, sre_ref, sim_ref, bbre_ref, bbim_ref):
    lre = lre_ref[...]
    lim = lim_ref[...]
    dt = jnp.exp(ldt_ref[...])
    zr = lre * dt
    zi = lim * dt
    mag = jnp.exp(zr)
    are = mag * jnp.cos(zi)
    aim = mag * jnp.sin(zi)
    are_ref[...] = are
    aim_ref[...] = aim
    mag_s = jnp.exp(zr * SSM_SEG)
    sre_ref[...] = mag_s * jnp.cos(zi * SSM_SEG)
    sim_ref[...] = mag_s * jnp.sin(zi * SSM_SEG)
    nr = are - 1.0
    ni = aim
    den = lre * lre + lim * lim
    fr = (nr * lre + ni * lim) / den
    fi = (ni * lre - nr * lim) / den
    bre = bre_ref[...]
    bim = bim_ref[...]
    bbre_ref[...] = fr * bre - fi * bim
    bbim_ref[...] = fr * bim + fi * bre


def _ssm_prep(lam_re, lam_im, log_dt, b_re, b_im):
    g, p = lam_re.shape
    n = b_re.shape[-1]
    c = g * p
    row = lambda a: a.reshape(1, c)
    chan = lambda b: b.transpose(2, 0, 1).reshape(n, c)
    ldt = jnp.broadcast_to(log_dt[:, None], (g, p))
    outs = pl.pallas_call(
        _ssm_prep_kernel,
        out_shape=[jax.ShapeDtypeStruct((1, c), F32)] * 4 + [jax.ShapeDtypeStruct((n, c), F32)] * 2,
        name="ssm_prep",
    )(row(lam_re), row(lam_im), row(ldt), chan(b_re), chan(b_im))
    return outs


def _mem_kv_kernel(m_ref, g_ref, w_ref, k_ref, v_ref):
    n = _rms(m_ref[0], g_ref[...]).astype(BF16)
    kv = _dot(n, w_ref[...])
    half = kv.shape[-1] // 2
    k_ref[0] = kv[:, :half].astype(BF16)
    v_ref[0] = kv[:, half:].astype(BF16)


def _mem_kv(mem, gain, w_kv):
    b, m, d = mem.shape
    w2 = w_kv.shape[-1]
    return pl.pallas_call(
        _mem_kv_kernel,
        grid=(b,),
        in_specs=[pl.BlockSpec((1, m, d), lambda i: (i, 0, 0)), _full((1, d)), _full((d, w2))],
        out_specs=[pl.BlockSpec((1, m, w2 // 2), lambda i: (i, 0, 0))] * 2,
        out_shape=[jax.ShapeDtypeStruct((b, m, w2 // 2), BF16)] * 2,
        compiler_params=_cparams(("arbitrary",)),
        name="mem_kv",
    )(mem, gain.reshape(1, d), w_kv.astype(BF16))


def _gelu_tanh(x):
    return 0.5 * x * (1.0 + jnp.tanh(math.sqrt(2.0 / math.pi) * (x + 0.044715 * (x * x * x))))


def _ssm_chunk(u, perm_ref, permt_ref, bre_ref, bim_ref, are_ref, aim_ref, sre_ref, sim_ref,
               cre_ref, cim_ref, d_ref, wglu_ref, hre, him, car_re, car_im, anchors=()):
    up = _dot(perm_ref[...], u.astype(BF16)).astype(BF16)
    nkb = bre_ref.shape[0]
    kw = bre_ref.shape[1]
    nw = bre_ref.shape[2]
    for kb in range(nkb):
        ukb = up[:, kb * kw:(kb + 1) * kw]
        hre[:, kb * nw:(kb + 1) * nw] = _dot(ukb, bre_ref[kb])
        him[:, kb * nw:(kb + 1) * nw] = _dot(ukb, bim_ref[kb])
    if callable(anchors):
        anchors = anchors()

    nstate = hre.shape[1]
    for cb in range(nstate // SSM_COLS):
        cs = slice(cb * SSM_COLS, (cb + 1) * SSM_COLS)
        a_re = jnp.broadcast_to(are_ref[:, cs], (SUBLANES, SSM_COLS))
        a_im = jnp.broadcast_to(aim_ref[:, cs], (SUBLANES, SSM_COLS))

        e_re = hre[0:SUBLANES, cs]
        e_im = him[0:SUBLANES, cs]
        for k in range(1, SSM_SEG):
            r = slice(k * SUBLANES, (k + 1) * SUBLANES)
            e_re, e_im = (a_re * e_re - a_im * e_im + hre[r, cs],
                          a_re * e_im + a_im * e_re + him[r, cs])
            hre[r, cs] = e_re
            him[r, cs] = e_im

        s_re = sre_ref[:, cs]
        s_im = sim_ref[:, cs]
        c_re = car_re[:, cs]
        c_im = car_im[:, cs]
        rows_re, rows_im = [], []
        for j in range(SUBLANES):
            rows_re.append(c_re)
            rows_im.append(c_im)
            n_re = s_re * c_re - s_im * c_im + e_re[j:j + 1]
            n_im = s_re * c_im + s_im * c_re + e_im[j:j + 1]
            c_re, c_im = n_re, n_im
        car_re[:, cs] = c_re
        car_im[:, cs] = c_im

        d_re = jnp.concatenate(rows_re, axis=0)
        d_im = jnp.concatenate(rows_im, axis=0)
        for k in range(SSM_SEG):
            r = slice(k * SUBLANES, (k + 1) * SUBLANES)
            d_re, d_im = a_re * d_re - a_im * d_im, a_re * d_im + a_im * d_re
            if k == SSM_SEG // 2:
                for z in anchors[cb::nstate // SSM_COLS]:
                    d_re = d_re + jnp.concatenate([z] * (SSM_COLS // LANES), axis=1)
            hre[r, cs] = hre[r, cs] + d_re
            him[r, cs] = him[r, cs] + d_im

    ncb = cre_ref.shape[0]
    cw = cre_ref.shape[1]
    ys = []
    for kb in range(ncb):
        h_re = hre[:, kb * cw:(kb + 1) * cw].astype(BF16)
        h_im = him[:, kb * cw:(kb + 1) * cw].astype(BF16)
        ys.append(_dot(h_re, cre_ref[kb]) - _dot(h_im, cim_ref[kb]))
    yp = jnp.concatenate(ys, axis=-1)
    hi = yp.astype(BF16)
    lo = (yp - hi.astype(F32)).astype(BF16)
    y = _dot(permt_ref[...], hi) + _dot(permt_ref[...], lo)
    y = y + d_ref[...] * u
    z = _dot(_gelu_tanh(y).astype(BF16), wglu_ref[...])
    half = z.shape[-1] // 2
    return z[:, :half] * jax.nn.sigmoid(z[:, half:])


def _schedule_anchor(x):
    rows, cols = x.shape
    s = x[:, :LANES]
    for c in range(1, cols // LANES):
        s = s + x[:, c * LANES:(c + 1) * LANES]
    s = jnp.sum(s.reshape(rows // SUBLANES, SUBLANES, LANES), axis=0)
    bits = pltpu.bitcast(s, jnp.uint32)
    return ((bits >> 16) >> 16).astype(F32)


def _in_ssm_kernel(x_ref, g_ref, wa_ref, wb_ref, wc_ref, bf_ref, *rest, n_cast):
    nconst = 12
    ssm_refs = rest[:nconst]
    cast_in = rest[nconst:nconst + n_cast]
    outs = rest[nconst + n_cast:]
    q_ref, k_ref, v_ref, cum_ref, oa_ref, wg_ref = outs[:6]
    cast_out = outs[6:6 + n_cast]
    carry_ref, car_re, car_im, w_bf, wf_bf = outs[6 + n_cast:11 + n_cast]
    h_scr = outs[11 + n_cast:]
    n_forget = cum_ref.shape[1]
    for src, dst in zip(cast_in, cast_out):
        dst[...] = src[...].astype(BF16)

    @pl.when(pl.program_id(1) == 0)
    def _():
        carry_ref[...] = jnp.zeros_like(carry_ref)
        car_re[...] = jnp.zeros_like(car_re)
        car_im[...] = jnp.zeros_like(car_im)
        w_bf[...] = wa_ref[...].T.astype(BF16)
        wf_bf[...] = wb_ref[:LANES, :].T.astype(BF16)
        wg_ref[...] = jnp.concatenate([wb_ref[n_forget:, :], wc_ref[...]], axis=0).T.astype(BF16)

    un = _rms(x_ref[...], g_ref[...]).astype(BF16)
    wu = ssm_refs[10].shape[1]
    wh = q_ref.shape[1]
    col = lambda lo, hi: _dot(un, w_bf[:, lo:hi])
    u = col(0, wu)
    pieces = [(ref, lo + p * MXU_TILE, p * MXU_TILE, scale)
              for ref, lo, scale in ((q_ref, wu, FOX_HEAD_DIM ** -0.5 * LOG2E),
                                     (k_ref, wu + wh, None), (v_ref, wu + 2 * wh, None))
              for p in range(wh // MXU_TILE)]
    n_chunks = u.shape[0] // SSM_CHUNK
    per_chunk = -(-len(pieces) // n_chunks)

    def project(todo):
        anchors = []
        for ref, src, dst, scale in todo:
            piece = col(src, src + MXU_TILE)
            if scale is not None:
                piece = piece * scale
            ref[:, dst:dst + MXU_TILE] = piece.astype(BF16)
            anchors.append(_schedule_anchor(piece))
        return anchors

    for c in range(n_chunks):
        rows = slice(c * SSM_CHUNK, (c + 1) * SSM_CHUNK)
        todo = pieces[c * per_chunk:(c + 1) * per_chunk]
        oa_ref[rows, :] = _ssm_chunk(u[rows], *ssm_refs, h_scr[2 * c], h_scr[2 * c + 1],
                                     car_re, car_im, functools.partial(project, todo))
    c = jax.nn.log_sigmoid(_dot(un, wf_bf[...]) + bf_ref[...])
    rows = c.shape[0]
    row = lax.broadcasted_iota(jnp.int32, c.shape, 0)
    shift = 1
    while shift < rows:
        c = c + jnp.where(row >= shift, pltpu.roll(c, shift, 0), 0.0)
        shift *= 2
    c = c + carry_ref[...]
    carry_ref[...] = c[rows - 1:rows, :]
    cum_ref[0] = (c * LOG2E).T[:cum_ref.shape[1], :]


def _in_ssm(x2, gain, w_in, bf, ssm_consts, later_weights, ssm_w, fox_w, n_heads, bsz, seq):
    t, d = x2.shape
    tm = TM_PROJ
    nb = seq // tm
    steps = bsz * nb
    slabs = [w.reshape(steps, w.shape[0] // steps, w.shape[1]) for w in later_weights]
    slab = lambda s: pl.BlockSpec((1,) + s.shape[1:], lambda b, i: (b * nb + i, 0, 0))
    assert tm % SSM_CHUNK == 0 and len(ssm_consts) == 12
    nstate = ssm_consts[4].shape[1]
    dm = ssm_consts[11].shape[1] // 2
    main_w = ssm_w + 3 * fox_w
    gate_w = w_in.shape[1] - main_w - n_heads
    assert gate_w == main_w and main_w % LANES == 0 and n_heads == SUBLANES
    win = lambda rows, idx: pl.BlockSpec((rows, d), lambda *_: (idx, 0),
                                         pipeline_mode=pl.Buffered(1))
    w_t = w_in.T
    row = lambda w: pl.BlockSpec((tm, w), lambda b, i: (b * nb + i, 0))
    h_scr = [pltpu.VMEM((SSM_CHUNK, nstate), F32)] * (2 * (tm // SSM_CHUNK))
    outs = pl.pallas_call(
        functools.partial(_in_ssm_kernel, n_cast=len(slabs)),
        grid=(bsz, nb),
        in_specs=[row(d), _full((1, d)), win(main_w, 0), win(main_w, 1),
                  win(n_heads, 2 * main_w // n_heads), _full((1, LANES))]
                 + [_full(c.shape) for c in ssm_consts] + [slab(s) for s in slabs],
        out_specs=[row(fox_w), row(fox_w), row(fox_w),
                   pl.BlockSpec((1, n_heads, tm), lambda b, i: (b, 0, i)), row(dm),
                   _full((d, gate_w))] + [slab(s) for s in slabs],
        out_shape=[jax.ShapeDtypeStruct((t, fox_w), BF16),
                   jax.ShapeDtypeStruct((t, fox_w), BF16),
                   jax.ShapeDtypeStruct((t, fox_w), BF16),
                   jax.ShapeDtypeStruct((bsz, n_heads, seq), F32),
                   jax.ShapeDtypeStruct((t, dm), F32),
                   jax.ShapeDtypeStruct((d, gate_w), BF16)]
                  + [jax.ShapeDtypeStruct(s.shape, BF16) for s in slabs],
        scratch_shapes=[pltpu.VMEM((1, LANES), F32), pltpu.VMEM((1, nstate), F32),
                        pltpu.VMEM((1, nstate), F32), pltpu.VMEM((d, main_w), BF16),
                        pltpu.VMEM((d, LANES), BF16)] + h_scr,
        compiler_params=_cparams(("arbitrary", "arbitrary")),
        name="in_ssm",
    )(x2, gain.reshape(1, d), w_t, w_t, w_t, bf, *ssm_consts, *slabs)
    cast = [o.reshape(w.shape) for o, w in zip(outs[6:], later_weights)]
    return list(outs[:6]) + cast


def _fox_kernel(q_ref, k_ref, v_ref, ck_ref, o_ref, m_scr, acc_scr, first_blk):
    seq = q_ref.shape[0]
    tq = FOX_TQ
    tk = FOX_TK
    dh = FOX_HEAD_DIM
    n_heads = ck_ref.shape[1]
    lane = lax.broadcasted_iota(jnp.int32, (1, LANES), 1)
    keep = [jnp.where(lane < dh, 1.0, 0.0).astype(BF16), jnp.where(lane >= dh, 1.0, 0.0).astype(BF16)]
    sum_lane = [dh, 0]
    ones = [jnp.where(lane == sl, 1.0, 0.0).astype(BF16) for sl in sum_lane]

    def block(q, j, rows, h, masked):
        e = h % 2
        ps = slice((h // 2) * LANES, (h // 2 + 1) * LANES)
        hs = slice(h * LANES, (h + 1) * LANES)
        ks = pl.ds(pl.multiple_of(j * tk, tk), tk)
        s = _dot_nt(q, k_ref[ks, ps] * keep[e]) - ck_ref[0, h, pl.ds(j, 1), :]
        if masked:
            r = lax.broadcasted_iota(jnp.int32, (tk, tk), 0)
            c = lax.broadcasted_iota(jnp.int32, (tk, tk), 1)
            top = jnp.where(c <= r, s[:tk], -jnp.inf)
            s = top if s.shape[0] == tk else jnp.concatenate([top, s[tk:]], axis=0)
        m_prev = m_scr[rows, hs]
        m_new = jnp.maximum(m_prev, jnp.max(s, axis=1, keepdims=True))
        p = jnp.exp2(s - jnp.concatenate([m_new] * (tk // LANES), axis=1))
        acc_scr[rows, hs] = (jnp.exp2(m_prev - m_new) * acc_scr[rows, hs]
                             + _dot(p.astype(BF16), v_ref[ks, ps] * keep[e] + ones[e]))
        m_scr[rows, hs] = m_new

    nsub = tq // tk
    nq = seq // tq
    nk = seq // tk
    all_rows = slice(0, tq)

    col = lax.broadcasted_iota(jnp.int32, (q_ref.shape[1], LANES), 0) // dh
    head_sel = jnp.where(col == lax.broadcasted_iota(jnp.int32, (q_ref.shape[1], LANES), 1),
                         1.0, 0.0).astype(BF16)

    def max_sq_norm(ref, lo, n):
        x = ref[lo:lo + n, :]
        sq = _dot(x * x, head_sel)
        return jnp.max(sq, axis=0, keepdims=True) * FOX_NORM_SLACK

    k_sq = jnp.concatenate([max_sq_norm(k_ref, j * tk, tk) for j in range(nk)], axis=0)
    q_sq = jnp.concatenate([max_sq_norm(q_ref, i * tq, tq) for i in range(nq)], axis=0)
    blk_id = lax.broadcasted_iota(jnp.int32, (nk, 1), 0).astype(F32)
    for h in range(n_heads):
        k_max = jnp.sqrt(k_sq[:, h:h + 1])
        c_end = ck_ref[0, h, :, tk - 1:tk]
        for i in range(nq):
            first_diag = float(i * nsub)
            q_max = jnp.sqrt(q_sq[i:i + 1, h:h + 1])
            k_own = jnp.max(k_max[i * nsub:(i + 1) * nsub], axis=0, keepdims=True)
            c_start = ck_ref[0, h, i * nsub:i * nsub + 1, 0:1]
            bound = q_max * (k_max + k_own) + (c_start - c_end)
            visit = jnp.logical_and(bound >= -FOX_SKIP_LOG2, blk_id < first_diag)
            first_blk[h * nq + i] = jnp.min(jnp.where(visit, blk_id, first_diag)).astype(jnp.int32)

    def qblock(qi, _):
        qs = pl.ds(pl.multiple_of(qi * tq, tq), tq)
        qp = [q_ref[qs, p * LANES:(p + 1) * LANES] for p in range(n_heads // 2)]
        m_scr[...] = jnp.full(m_scr.shape, -jnp.inf, F32)
        acc_scr[...] = jnp.zeros(acc_scr.shape, F32)

        def full(j, _):
            for h in range(n_heads):
                block(qp[h // 2], j, all_rows, h, False)
            return 0

        starts = [first_blk[h * nq + qi] for h in range(n_heads)]
        common = functools.reduce(jnp.maximum, starts)
        for h in range(n_heads):
            def one(j, _, h=h):
                block(qp[h // 2], j, all_rows, h, False)
                return 0

            def two(t, _, h=h):
                block(qp[h // 2], starts[h] + 2 * t, all_rows, h, False)
                block(qp[h // 2], starts[h] + 2 * t + 1, all_rows, h, False)
                return 0

            pairs = lax.shift_right_logical(common - starts[h], 1)
            lax.fori_loop(0, pairs, two, 0)
            lax.fori_loop(starts[h] + 2 * pairs, common, one, 0)
        lax.fori_loop(common, qi * nsub, full, 0)
        for r in range(nsub):
            rows = slice(r * tk, tq)
            for h in range(n_heads):
                block(qp[h // 2][rows], qi * nsub + r, rows, h, True)
        for p in range(n_heads // 2):
            a0 = acc_scr[:, (2 * p) * LANES:(2 * p + 1) * LANES]
            a1 = acc_scr[:, (2 * p + 1) * LANES:(2 * p + 2) * LANES]
            out = jnp.where(lane < dh, a0 / a0[:, sum_lane[0]:sum_lane[0] + 1],
                            a1 / a1[:, sum_lane[1]:sum_lane[1] + 1])
            o_ref[qs, p * LANES:(p + 1) * LANES] = out.astype(o_ref.dtype)
        return 0

    lax.fori_loop(0, seq // tq, qblock, 0)


def _fox(q, k, v, cum_t, bsz, seq):
    t, w = q.shape
    assert FOX_TQ % FOX_TK == 0 and seq % FOX_TQ == 0 and FOX_HEADS_PER_STEP % 2 == 0
    hps = FOX_HEADS_PER_STEP
    wblk = hps * FOX_HEAD_DIM
    blk = pl.BlockSpec((seq, wblk), lambda bi, hi: (bi, hi))
    return pl.pallas_call(
        _fox_kernel,
        grid=(bsz, w // wblk),
        in_specs=[blk, blk, blk,
                  pl.BlockSpec((1, hps) + cum_t.shape[2:], lambda bi, hi: (bi, hi, 0, 0))],
        out_specs=blk,
        out_shape=jax.ShapeDtypeStruct((t, w), BF16),
        scratch_shapes=[pltpu.VMEM((FOX_TQ, hps * LANES), F32),
                        pltpu.VMEM((FOX_TQ, hps * LANES), F32),
                        pltpu.SMEM((hps * (seq // FOX_TQ),), jnp.int32)],
        compiler_params=_cparams(("arbitrary", "arbitrary")),
        name="fox",
    )(q, k, v, cum_t)


def _post_kernel(x_ref, oa_ref, att_ref, gmix_ref, wg_ref, wfo_ref, wmix_ref, gq_ref, wq_ref,
                 km_ref, vm_ref, wo_ref, gf_ref, win_ref, wout_ref, gfin_ref, o_ref):
    d = x_ref.shape[-1]
    x = x_ref[...]
    gate = jax.nn.sigmoid(_dot(_rms(x, gmix_ref[...]).astype(BF16), wg_ref[...]))
    out_b = _dot(att_ref[...], wfo_ref[...])
    mix = gate[:, :d] * oa_ref[...] + gate[:, d:] * out_b
    h1 = x + _dot(mix.astype(BF16), wmix_ref[...])

    n = _rms(h1, gq_ref[...]).astype(BF16)
    qm = _dot(n, wq_ref[...])
    hd = qm.shape[-1] // MEM_HEADS
    qm = (qm * (hd ** -0.5)).astype(BF16)
    outs = []
    for hh in range(MEM_HEADS):
        hs = slice(hh * hd, (hh + 1) * hd)
        s = _dot_nt(qm[:, hs], km_ref[0, :, hs])
        s = s - jnp.max(s, axis=-1, keepdims=True)
        p = jnp.exp(s)
        p = p / jnp.sum(p, axis=-1, keepdims=True)
        outs.append(_dot(p.astype(BF16), vm_ref[0, :, hs]))
    o = jnp.concatenate(outs, axis=-1).astype(BF16)
    h2 = h1 + _dot(o, wo_ref[...])

    f = _rms(h2, gf_ref[...]).astype(BF16)
    hidden = wout_ref.shape[0]
    acc = jnp.zeros_like(h2)
    for lo, hi in _ffn_chunks(hidden):
        fa = _dot(f, win_ref[:, lo:hi])
        fb = _dot(f, win_ref[:, hidden + lo:hidden + hi])
        g = (fa * jax.nn.sigmoid(fa) * fb).astype(BF16)
        acc = acc + _dot(g, wout_ref[lo:hi, :])
    o_ref[...] = _rms(h2 + acc, gfin_ref[...])


def _ffn_chunks(hidden):
    tiles = hidden // MXU_TILE
    assert tiles * MXU_TILE == hidden
    cuts = [-(-tiles * c // FFN_CHUNKS) * MXU_TILE for c in range(FFN_CHUNKS + 1)]
    return list(zip(cuts[:-1], cuts[1:]))


def _resident(shape):
    n = len(shape)
    return pl.BlockSpec(shape, lambda *_: (0,) * n, pipeline_mode=pl.Buffered(1))


def _post(x2, out_a, att, gain_mix, w_gate, w_fox_o, w_mix, gain_q, w_q, k_m, v_m, w_o,
          gain_f, w_ffn_in, w_ffn_out, gain_fin, bsz, seq):
    t, d = x2.shape
    tm = TM_POST
    nb = seq // tm
    row = lambda w: pl.BlockSpec((tm, w), lambda b, i: (b * nb + i, 0))
    mem = pl.BlockSpec((1,) + k_m.shape[1:], lambda b, i: (b, 0, 0))
    gain = lambda g: g.reshape(1, d)
    weights = lambda *ws: [_resident(w.shape) for w in ws]
    return pl.pallas_call(
        _post_kernel,
        grid=(bsz, nb),
        in_specs=[row(d), row(d), row(att.shape[1]), _full((1, d))]
                 + weights(w_gate, w_fox_o, w_mix) + [_full((1, d))] + weights(w_q)
                 + [mem, mem] + weights(w_o) + [_full((1, d))] + weights(w_ffn_in, w_ffn_out)
                 + [_full((1, d))],
        out_specs=row(d),
        out_shape=jax.ShapeDtypeStruct((t, d), F32),
        compiler_params=_cparams(("arbitrary", "arbitrary")),
        name="post",
    )(x2, out_a, att, gain(gain_mix), w_gate, w_fox_o, w_mix, gain(gain_q), w_q, k_m, v_m, w_o,
      gain(gain_f), w_ffn_in, w_ffn_out, gain(gain_fin))


def _block_diag(blocks, per):
    g, r, c = blocks.shape
    b = blocks.reshape(g // per, per, r, c)
    eye = jnp.eye(per, dtype=blocks.dtype)
    out = b[:, :, :, None, :] * eye[None, :, None, :, None]
    return out.reshape(g // per, per * r, per * c)


def _seg_perm(q):
    seg = q // SUBLANES
    r = jnp.arange(q)
    src = (r % SUBLANES) * seg + r // SUBLANES
    return (src[:, None] == jnp.arange(q)[None, :]).astype(BF16)


def _layer(x2, mem, bsz, seq, norm_mix, w_in, b_forget, lam_re, lam_im, log_dt, b_re, b_im, c_re,
           c_im, d_skip, w_glu, w_fox_o, w_mix_out, norm_mem_q, norm_mem_kv, w_mem_q, w_mem_kv,
           w_mem_o, norm_ffn, w_ffn_in, w_ffn_out, norm_final):
    d = x2.shape[-1]
    groups, states = lam_re.shape
    ssm_w = groups * SSM_GROUP
    n_heads = b_forget.shape[0]
    fox_w = n_heads * FOX_HEAD_DIM

    a_re, a_im, s_re, s_im, bb_re, bb_im = _ssm_prep(lam_re, lam_im, log_dt, b_re, b_im)
    nstate = groups * states
    per = LANES // SSM_GROUP
    to_rows = lambda a: a.reshape(1, nstate)
    bd_in = lambda bb: _block_diag(
        bb.reshape(SSM_GROUP, groups, states).transpose(1, 0, 2), per).astype(BF16)
    bd_out = lambda c: _block_diag(c.transpose(0, 2, 1), per).astype(BF16)
    perm = _seg_perm(SSM_CHUNK)

    bf = jnp.zeros((1, LANES), F32).at[0, :n_heads].set(b_forget)
    ssm_consts = [perm, perm.T, bd_in(bb_re), bd_in(bb_im), to_rows(a_re), to_rows(a_im),
                  to_rows(s_re), to_rows(s_im), bd_out(c_re), bd_out(c_im),
                  d_skip.reshape(1, ssm_w), w_glu.astype(BF16)]
    post_weights = [w_fox_o, w_mix_out, w_mem_q, w_mem_o, w_ffn_in, w_ffn_out]
    q, k, v, cum_t, out_a, w_gate, *post_bf16 = _in_ssm(
        x2, norm_mix, w_in, bf, ssm_consts, post_weights, ssm_w, fox_w, n_heads, bsz, seq)
    wb_fox_o, wb_mix, wb_mem_q, wb_mem_o, wb_ffn_in, wb_ffn_out = post_bf16

    att = _fox(q, k, v, cum_t.reshape(bsz, n_heads, seq // FOX_TK, FOX_TK), bsz, seq)

    k_m, v_m = _mem_kv(mem, norm_mem_kv, w_mem_kv)
    return _post(x2, out_a, att, norm_mix, w_gate, wb_fox_o, wb_mix, norm_mem_q, wb_mem_q, k_m, v_m,
                 wb_mem_o, norm_ffn, wb_ffn_in, wb_ffn_out, norm_final, bsz, seq)


def kernel(x, mem, norm_mix, w_in, b_forget, lam_re, lam_im, log_dt, b_re, b_im, c_re, c_im, d_skip,
           w_glu, w_fox_o, w_mix_out, norm_mem_q, norm_mem_kv, w_mem_q, w_mem_kv, w_mem_o, norm_ffn,
           w_ffn_in, w_ffn_out, norm_final):
    bsz, seq, d = x.shape
    assert w_in.shape[0] == 1, "single-layer block"
    out = _layer(x.reshape(bsz * seq, d), mem, bsz, seq, norm_mix[0], w_in[0], b_forget[0],
                 lam_re[0], lam_im[0], log_dt[0], b_re[0], b_im[0], c_re[0], c_im[0], d_skip[0],
                 w_glu[0], w_fox_o[0], w_mix_out[0], norm_mem_q[0], norm_mem_kv[0], w_mem_q[0],
                 w_mem_kv[0], w_mem_o[0], norm_ffn[0], w_ffn_in[0], w_ffn_out[0], norm_final)
    return out.reshape(bsz, seq, d)
```

```python
import functools
import math

import jax
import jax.numpy as jnp
from jax import lax
from jax.experimental import pallas as pl
from jax.experimental.pallas import tpu as pltpu

F32 = jnp.float32
BF16 = jnp.bfloat16

RMS_EPS = 1e-6
LOG2E = math.log2(math.e)
SSM_GROUP = 16
FOX_HEAD_DIM = 64
MEM_HEADS = 4
LANES = 128
SUBLANES = 8
MXU_TILE = 256
VMEM_LIMIT = 56 * 1024 * 1024

TM_PROJ = 512
TM_POST = 512
FFN_CHUNKS = 2
SSM_CHUNK = 256
SSM_SEG = SSM_CHUNK // SUBLANES
SSM_COLS = 512
FOX_TQ = 1024
FOX_TK = 512
FOX_HEADS_PER_STEP = 4
FOX_SKIP_LOG2 = 160.0
FOX_NORM_SLACK = 1.02


def _cparams(sem):
    return pltpu.CompilerParams(dimension_semantics=sem, vmem_limit_bytes=VMEM_LIMIT)


def _rms(x, gain):
    return x * lax.rsqrt(jnp.mean(x * x, axis=-1, keepdims=True) + RMS_EPS) * gain


def _dot(a, b):
    return jnp.dot(a, b, preferred_element_type=F32)


def _dot_nt(a, b):
    return lax.dot_general(a, b, (((1,), (1,)), ((), ())), preferred_element_type=F32)


def _full(shape):
    n = len(shape)
    return pl.BlockSpec(shape, lambda *_: (0,) * n)


def _ssm_prep_kernel(lre_ref, lim_ref, ldt_ref, bre_ref, bim_ref,
                     are_ref, aim_ref, sre_ref, sim_ref, bbre_ref, bbim_ref):
    lre = lre_ref[...]
    lim = lim_ref[...]
    dt = jnp.exp(ldt_ref[...])
    zr = lre * dt
    zi = lim * dt
    mag = jnp.exp(zr)
    are = mag * jnp.cos(zi)
    aim = mag * jnp.sin(zi)
    are_ref[...] = are
    aim_ref[...] = aim
    mag_s = jnp.exp(zr * SSM_SEG)
    sre_ref[...] = mag_s * jnp.cos(zi * SSM_SEG)
    sim_ref[...] = mag_s * jnp.sin(zi * SSM_SEG)
    nr = are - 1.0
    ni = aim
    den = lre * lre + lim * lim
    fr = (nr * lre + ni * lim) / den
    fi = (ni * lre - nr * lim) / den
    bre = bre_ref[...]
    bim = bim_ref[...]
    bbre_ref[...] = fr * bre - fi * bim
    bbim_ref[...] = fr * bim + fi * bre


def _ssm_prep(lam_re, lam_im, log_dt, b_re, b_im):
    g, p = lam_re.shape
    n = b_re.shape[-1]
    c = g * p
    row = lambda a: a.reshape(1, c)
    chan = lambda b: b.transpose(2, 0, 1).reshape(n, c)
    ldt = jnp.broadcast_to(log_dt[:, None], (g, p))
    outs = pl.pallas_call(
        _ssm_prep_kernel,
        out_shape=[jax.ShapeDtypeStruct((1, c), F32)] * 4 + [jax.ShapeDtypeStruct((n, c), F32)] * 2,
        name="ssm_prep",
    )(row(lam_re), row(lam_im), row(ldt), chan(b_re), chan(b_im))
    return outs


def _mem_kv_kernel(m_ref, g_ref, w_ref, k_ref, v_ref):
    n = _rms(m_ref[0], g_ref[...]).astype(BF16)
    kv = _dot(n, w_ref[...])
    half = kv.shape[-1] // 2
    k_ref[0] = kv[:, :half].astype(BF16)
    v_ref[0] = kv[:, half:].astype(BF16)


def _mem_kv(mem, gain, w_kv):
    b, m, d = mem.shape
    w2 = w_kv.shape[-1]
    return pl.pallas_call(
        _mem_kv_kernel,
        grid=(b,),
        in_specs=[pl.BlockSpec((1, m, d), lambda i: (i, 0, 0)), _full((1, d)), _full((d, w2))],
        out_specs=[pl.BlockSpec((1, m, w2 // 2), lambda i: (i, 0, 0))] * 2,
        out_shape=[jax.ShapeDtypeStruct((b, m, w2 // 2), BF16)] * 2,
        compiler_params=_cparams(("arbitrary",)),
        name="mem_kv",
    )(mem, gain.reshape(1, d), w_kv.astype(BF16))


def _gelu_tanh(x):
    return 0.5 * x * (1.0 + jnp.tanh(math.sqrt(2.0 / math.pi) * (x + 0.044715 * (x * x * x))))


def _ssm_drive(u, perm_ref, bre_ref, bim_ref, h_scr):
    n_chunks = u.shape[0] // SSM_CHUNK
    ub = u.astype(BF16)
    up = jnp.concatenate(
        [_dot(perm_ref[...], ub[c * SSM_CHUNK:(c + 1) * SSM_CHUNK]).astype(BF16)
         for c in range(n_chunks)], axis=0)
    kw = bre_ref.shape[1]
    nw = bre_ref.shape[2]
    for kb in range(bre_ref.shape[0]):
        ukb = up[:, kb * kw:(kb + 1) * kw]
        for part, w_ref in enumerate((bre_ref, bim_ref)):
            bu = _dot(ukb, w_ref[kb])
            for c in range(n_chunks):
                h_scr[2 * c + part][:, kb * nw:(kb + 1) * nw] = bu[c * SSM_CHUNK:(c + 1) * SSM_CHUNK]


def _ssm_chunk(u, permt_ref, are_ref, aim_ref, sre_ref, sim_ref, cre_ref, cim_ref, d_ref,
               hre, him, car_re, car_im, anchors=()):
    if callable(anchors):
        anchors = anchors()

    nstate = hre.shape[1]
    for cb in range(nstate // SSM_COLS):
        cs = slice(cb * SSM_COLS, (cb + 1) * SSM_COLS)
        a_re = jnp.broadcast_to(are_ref[:, cs], (SUBLANES, SSM_COLS))
        a_im = jnp.broadcast_to(aim_ref[:, cs], (SUBLANES, SSM_COLS))

        e_re = hre[0:SUBLANES, cs]
        e_im = him[0:SUBLANES, cs]
        for k in range(1, SSM_SEG):
            r = slice(k * SUBLANES, (k + 1) * SUBLANES)
            e_re, e_im = (a_re * e_re - a_im * e_im + hre[r, cs],
                          a_re * e_im + a_im * e_re + him[r, cs])
            hre[r, cs] = e_re
            him[r, cs] = e_im

        s_re = sre_ref[:, cs]
        s_im = sim_ref[:, cs]
        c_re = car_re[:, cs]
        c_im = car_im[:, cs]
        rows_re, rows_im = [], []
        for j in range(SUBLANES):
            rows_re.append(c_re)
            rows_im.append(c_im)
            n_re = s_re * c_re - s_im * c_im + e_re[j:j + 1]
            n_im = s_re * c_im + s_im * c_re + e_im[j:j + 1]
            c_re, c_im = n_re, n_im
        car_re[:, cs] = c_re
        car_im[:, cs] = c_im

        d_re = jnp.concatenate(rows_re, axis=0)
        d_im = jnp.concatenate(rows_im, axis=0)
        for k in range(SSM_SEG):
            r = slice(k * SUBLANES, (k + 1) * SUBLANES)
            d_re, d_im = a_re * d_re - a_im * d_im, a_re * d_im + a_im * d_re
            if k == SSM_SEG // 2:
                for z in anchors[cb::nstate // SSM_COLS]:
                    d_re = d_re + jnp.concatenate([z] * (SSM_COLS // LANES), axis=1)
            hre[r, cs] = hre[r, cs] + d_re
            him[r, cs] = him[r, cs] + d_im

    ncb = cre_ref.shape[0]
    cw = cre_ref.shape[1]
    ys = []
    for kb in range(ncb):
        h_re = hre[:, kb * cw:(kb + 1) * cw].astype(BF16)
        h_im = him[:, kb * cw:(kb + 1) * cw].astype(BF16)
        ys.append(_dot(h_re, cre_ref[kb]) - _dot(h_im, cim_ref[kb]))
    yp = jnp.concatenate(ys, axis=-1)
    hi = yp.astype(BF16)
    lo = (yp - hi.astype(F32)).astype(BF16)
    y = _dot(permt_ref[...], hi) + _dot(permt_ref[...], lo)
    y = y + d_ref[...] * u
    return _gelu_tanh(y).astype(BF16)


def _glu(g, wglu_ref):
    z = _dot(g, wglu_ref[...])
    half = z.shape[-1] // 2
    return z[:, :half] * jax.nn.sigmoid(z[:, half:])


def _schedule_anchor(x):
    rows, cols = x.shape
    s = x[:, :LANES]
    for c in range(1, cols // LANES):
        s = s + x[:, c * LANES:(c + 1) * LANES]
    s = jnp.sum(s.reshape(rows // SUBLANES, SUBLANES, LANES), axis=0)
    bits = pltpu.bitcast(s, jnp.uint32)
    return ((bits >> 16) >> 16).astype(F32)


def _in_ssm_kernel(x_ref, g_ref, wa_ref, wb_ref, wc_ref, bf_ref, *rest, n_cast):
    nconst = 12
    (perm_ref, permt_ref, bre_ref, bim_ref, are_ref, aim_ref, sre_ref, sim_ref,
     cre_ref, cim_ref, d_ref, wglu_ref) = rest[:nconst]
    cast_in = rest[nconst:nconst + n_cast]
    outs = rest[nconst + n_cast:]
    q_ref, k_ref, v_ref, cum_ref, oa_ref, wg_ref = outs[:6]
    cast_out = outs[6:6 + n_cast]
    carry_ref, car_re, car_im, w_bf, wf_bf = outs[6 + n_cast:11 + n_cast]
    h_scr = outs[11 + n_cast:]
    n_forget = cum_ref.shape[1]
    for src, dst in zip(cast_in, cast_out):
        dst[...] = src[...].astype(BF16)

    @pl.when(pl.program_id(1) == 0)
    def _():
        carry_ref[...] = jnp.zeros_like(carry_ref)
        car_re[...] = jnp.zeros_like(car_re)
        car_im[...] = jnp.zeros_like(car_im)
        w_bf[...] = wa_ref[...].T.astype(BF16)
        wf_bf[...] = wb_ref[:LANES, :].T.astype(BF16)
        wg_ref[...] = jnp.concatenate([wb_ref[n_forget:, :], wc_ref[...]], axis=0).T.astype(BF16)

    un = _rms(x_ref[...], g_ref[...]).astype(BF16)
    wu = d_ref.shape[1]
    wh = q_ref.shape[1]
    col = lambda lo, hi: _dot(un, w_bf[:, lo:hi])
    u = col(0, wu)
    pieces = [(ref, lo + p * MXU_TILE, p * MXU_TILE, scale)
              for ref, lo, scale in ((q_ref, wu, FOX_HEAD_DIM ** -0.5 * LOG2E),
                                     (k_ref, wu + wh, None), (v_ref, wu + 2 * wh, None))
              for p in range(wh // MXU_TILE)]
    n_chunks = u.shape[0] // SSM_CHUNK
    per_chunk = -(-len(pieces) // n_chunks)

    def project(todo):
        anchors = []
        for ref, src, dst, scale in todo:
            piece = col(src, src + MXU_TILE)
            if scale is not None:
                piece = piece * scale
            ref[:, dst:dst + MXU_TILE] = piece.astype(BF16)
            anchors.append(_schedule_anchor(piece))
        return anchors

    _ssm_drive(u, perm_ref, bre_ref, bim_ref, h_scr)
    acts = []
    for c in range(n_chunks):
        rows = slice(c * SSM_CHUNK, (c + 1) * SSM_CHUNK)
        todo = pieces[c * per_chunk:(c + 1) * per_chunk]
        acts.append(_ssm_chunk(u[rows], permt_ref, are_ref, aim_ref, sre_ref, sim_ref, cre_ref,
                               cim_ref, d_ref, h_scr[2 * c], h_scr[2 * c + 1], car_re, car_im,
                               functools.partial(project, todo)))
    oa_ref[...] = _glu(jnp.concatenate(acts, axis=0), wglu_ref)
    c = jax.nn.log_sigmoid(_dot(un, wf_bf[...]) + bf_ref[...])
    rows = c.shape[0]
    row = lax.broadcasted_iota(jnp.int32, c.shape, 0)
    shift = 1
    while shift < rows:
        c = c + jnp.where(row >= shift, pltpu.roll(c, shift, 0), 0.0)
        shift *= 2
    c = c + carry_ref[...]
    carry_ref[...] = c[rows - 1:rows, :]
    cum_ref[0] = (c * LOG2E).T[:cum_ref.shape[1], :]


def _in_ssm(x2, gain, w_in, bf, ssm_consts, later_weights, ssm_w, fox_w, n_heads, bsz, seq):
    t, d = x2.shape
    tm = TM_PROJ
    nb = seq // tm
    steps = bsz * nb
    slabs = [w.reshape(steps, w.shape[0] // steps, w.shape[1]) for w in later_weights]
    slab = lambda s: pl.BlockSpec((1,) + s.shape[1:], lambda b, i: (b * nb + i, 0, 0))
    assert tm % SSM_CHUNK == 0 and len(ssm_consts) == 12
    nstate = ssm_consts[4].shape[1]
    dm = ssm_consts[11].shape[1] // 2
    main_w = ssm_w + 3 * fox_w
    gate_w = w_in.shape[1] - main_w - n_heads
    assert gate_w == main_w and main_w % LANES == 0 and n_heads == SUBLANES
    win = lambda rows, idx: pl.BlockSpec((rows, d), lambda *_: (idx, 0),
                                         pipeline_mode=pl.Buffered(1))
    w_t = w_in.T
    row = lambda w: pl.BlockSpec((tm, w), lambda b, i: (b * nb + i, 0))
    h_scr = [pltpu.VMEM((SSM_CHUNK, nstate), F32)] * (2 * (tm // SSM_CHUNK))
    outs = pl.pallas_call(
        functools.partial(_in_ssm_kernel, n_cast=len(slabs)),
        grid=(bsz, nb),
        in_specs=[row(d), _full((1, d)), win(main_w, 0), win(main_w, 1),
                  win(n_heads, 2 * main_w // n_heads), _full((1, LANES))]
                 + [_full(c.shape) for c in ssm_consts] + [slab(s) for s in slabs],
        out_specs=[row(fox_w), row(fox_w), row(fox_w),
                   pl.BlockSpec((1, n_heads, tm), lambda b, i: (b, 0, i)), row(dm),
                   _full((d, gate_w))] + [slab(s) for s in slabs],
        out_shape=[jax.ShapeDtypeStruct((t, fox_w), BF16),
                   jax.ShapeDtypeStruct((t, fox_w), BF16),
                   jax.ShapeDtypeStruct((t, fox_w), BF16),
                   jax.ShapeDtypeStruct((bsz, n_heads, seq), F32),
                   jax.ShapeDtypeStruct((t, dm), F32),
                   jax.ShapeDtypeStruct((d, gate_w), BF16)]
                  + [jax.ShapeDtypeStruct(s.shape, BF16) for s in slabs],
        scratch_shapes=[pltpu.VMEM((1, LANES), F32), pltpu.VMEM((1, nstate), F32),
                        pltpu.VMEM((1, nstate), F32), pltpu.VMEM((d, main_w), BF16),
                        pltpu.VMEM((d, LANES), BF16)] + h_scr,
        compiler_params=_cparams(("arbitrary", "arbitrary")),
        name="in_ssm",
    )(x2, gain.reshape(1, d), w_t, w_t, w_t, bf, *ssm_consts, *slabs)
    cast = [o.reshape(w.shape) for o, w in zip(outs[6:], later_weights)]
    return list(outs[:6]) + cast


def _fox_kernel(q_ref, k_ref, v_ref, ck_ref, o_ref, m_scr, acc_scr, first_blk):
    seq = q_ref.shape[0]
    tq = FOX_TQ
    tk = FOX_TK
    dh = FOX_HEAD_DIM
    n_heads = ck_ref.shape[1]
    lane = lax.broadcasted_iota(jnp.int32, (1, LANES), 1)
    keep = [jnp.where(lane < dh, 1.0, 0.0).astype(BF16), jnp.where(lane >= dh, 1.0, 0.0).astype(BF16)]
    sum_lane = [dh, 0]
    ones = [jnp.where(lane == sl, 1.0, 0.0).astype(BF16) for sl in sum_lane]

    def block(q, j, rows, h, masked):
        e = h % 2
        ps = slice((h // 2) * LANES, (h // 2 + 1) * LANES)
        hs = slice(h * LANES, (h + 1) * LANES)
        ks = pl.ds(pl.multiple_of(j * tk, tk), tk)
        s = _dot_nt(q, k_ref[ks, ps] * keep[e]) - ck_ref[0, h, pl.ds(j, 1), :]
        if masked:
            r = lax.broadcasted_iota(jnp.int32, (tk, tk), 0)
            c = lax.broadcasted_iota(jnp.int32, (tk, tk), 1)
            top = jnp.where(c <= r, s[:tk], -jnp.inf)
            s = top if s.shape[0] == tk else jnp.concatenate([top, s[tk:]], axis=0)
        m_prev = m_scr[rows, hs]
        m_new = jnp.maximum(m_prev, jnp.max(s, axis=1, keepdims=True))
        p = jnp.exp2(s - jnp.concatenate([m_new] * (tk // LANES), axis=1))
        acc_scr[rows, hs] = (jnp.exp2(m_prev - m_new) * acc_scr[rows, hs]
                             + _dot(p.astype(BF16), v_ref[ks, ps] * keep[e] + ones[e]))
        m_scr[rows, hs] = m_new

    nsub = tq // tk
    nq = seq // tq
    nk = seq // tk
    all_rows = slice(0, tq)

    col = lax.broadcasted_iota(jnp.int32, (q_ref.shape[1], LANES), 0) // dh
    head_sel = jnp.where(col == lax.broadcasted_iota(jnp.int32, (q_ref.shape[1], LANES), 1),
                         1.0, 0.0).astype(BF16)

    def max_sq_norm(ref, lo, n):
        x = ref[lo:lo + n, :]
        sq = _dot(x * x, head_sel)
        return jnp.max(sq, axis=0, keepdims=True) * FOX_NORM_SLACK

    k_sq = jnp.concatenate([max_sq_norm(k_ref, j * tk, tk) for j in range(nk)], axis=0)
    q_sq = jnp.concatenate([max_sq_norm(q_ref, i * tq, tq) for i in range(nq)], axis=0)
    blk_id = lax.broadcasted_iota(jnp.int32, (nk, 1), 0).astype(F32)
    for h in range(n_heads):
        k_max = jnp.sqrt(k_sq[:, h:h + 1])
        c_end = ck_ref[0, h, :, tk - 1:tk]
        for i in range(nq):
            first_diag = float(i * nsub)
            q_max = jnp.sqrt(q_sq[i:i + 1, h:h + 1])
            k_own = jnp.max(k_max[i * nsub:(i + 1) * nsub], axis=0, keepdims=True)
            c_start = ck_ref[0, h, i * nsub:i * nsub + 1, 0:1]
            bound = q_max * (k_max + k_own) + (c_start - c_end)
            visit = jnp.logical_and(bound >= -FOX_SKIP_LOG2, blk_id < first_diag)
            first_blk[h * nq + i] = jnp.min(jnp.where(visit, blk_id, first_diag)).astype(jnp.int32)

    def qblock(qi, _):
        qs = pl.ds(pl.multiple_of(qi * tq, tq), tq)
        qp = [q_ref[qs, p * LANES:(p + 1) * LANES] for p in range(n_heads // 2)]
        m_scr[...] = jnp.full(m_scr.shape, -jnp.inf, F32)
        acc_scr[...] = jnp.zeros(acc_scr.shape, F32)

        def full(j, _):
            for h in range(n_heads):
                block(qp[h // 2], j, all_rows, h, False)
            return 0

        starts = [first_blk[h * nq + qi] for h in range(n_heads)]
        common = functools.reduce(jnp.maximum, starts)
        for h in range(n_heads):
            def one(j, _, h=h):
                block(qp[h // 2], j, all_rows, h, False)
                return 0

            def two(t, _, h=h):
                block(qp[h // 2], starts[h] + 2 * t, all_rows, h, False)
                block(qp[h // 2], starts[h] + 2 * t + 1, all_rows, h, False)
                return 0

            pairs = lax.shift_right_logical(common - starts[h], 1)
            lax.fori_loop(0, pairs, two, 0)
            lax.fori_loop(starts[h] + 2 * pairs, common, one, 0)
        lax.fori_loop(common, qi * nsub, full, 0)
        for r in range(nsub):
            rows = slice(r * tk, tq)
            for h in range(n_heads):
                block(qp[h // 2][rows], qi * nsub + r, rows, h, True)
        for p in range(n_heads // 2):
            a0 = acc_scr[:, (2 * p) * LANES:(2 * p + 1) * LANES]
            a1 = acc_scr[:, (2 * p + 1) * LANES:(2 * p + 2) * LANES]
            out = jnp.where(lane < dh, a0 / a0[:, sum_lane[0]:sum_lane[0] + 1],
                            a1 / a1[:, sum_lane[1]:sum_lane[1] + 1])
            o_ref[qs, p * LANES:(p + 1) * LANES] = out.astype(o_ref.dtype)
        return 0

    lax.fori_loop(0, seq // tq, qblock, 0)


def _fox(q, k, v, cum_t, bsz, seq):
    t, w = q.shape
    assert FOX_TQ % FOX_TK == 0 and seq % FOX_TQ == 0 and FOX_HEADS_PER_STEP % 2 == 0
    hps = FOX_HEADS_PER_STEP
    wblk = hps * FOX_HEAD_DIM
    blk = pl.BlockSpec((seq, wblk), lambda bi, hi: (bi, hi))
    return pl.pallas_call(
        _fox_kernel,
        grid=(bsz, w // wblk),
        in_specs=[blk, blk, blk,
                  pl.BlockSpec((1, hps) + cum_t.shape[2:], lambda bi, hi: (bi, hi, 0, 0))],
        out_specs=blk,
        out_shape=jax.ShapeDtypeStruct((t, w), BF16),
        scratch_shapes=[pltpu.VMEM((FOX_TQ, hps * LANES), F32),
                        pltpu.VMEM((FOX_TQ, hps * LANES), F32),
                        pltpu.SMEM((hps * (seq // FOX_TQ),), jnp.int32)],
        compiler_params=_cparams(("arbitrary", "arbitrary")),
        name="fox",
    )(q, k, v, cum_t)


def _post_kernel(x_ref, oa_ref, att_ref, gmix_ref, wg_ref, wfo_ref, wmix_ref, gq_ref, wq_ref,
                 km_ref, vm_ref, wo_ref, gf_ref, win_ref, wout_ref, gfin_ref, o_ref):
    d = x_ref.shape[-1]
    x = x_ref[...]
    gate = jax.nn.sigmoid(_dot(_rms(x, gmix_ref[...]).astype(BF16), wg_ref[...]))
    out_b = _dot(att_ref[...], wfo_ref[...])
    mix = gate[:, :d] * oa_ref[...] + gate[:, d:] * out_b
    h1 = x + _dot(mix.astype(BF16), wmix_ref[...])

    n = _rms(h1, gq_ref[...]).astype(BF16)
    qm = _dot(n, wq_ref[...])
    hd = qm.shape[-1] // MEM_HEADS
    qm = (qm * (hd ** -0.5)).astype(BF16)
    outs = []
    for hh in range(MEM_HEADS):
        hs = slice(hh * hd, (hh + 1) * hd)
        s = _dot_nt(qm[:, hs], km_ref[0, :, hs])
        s = s - jnp.max(s, axis=-1, keepdims=True)
        p = jnp.exp(s)
        p = p / jnp.sum(p, axis=-1, keepdims=True)
        outs.append(_dot(p.astype(BF16), vm_ref[0, :, hs]))
    o = jnp.concatenate(outs, axis=-1).astype(BF16)
    h2 = h1 + _dot(o, wo_ref[...])

    f = _rms(h2, gf_ref[...]).astype(BF16)
    hidden = wout_ref.shape[0]
    acc = jnp.zeros_like(h2)
    for lo, hi in _ffn_chunks(hidden):
        fa = _dot(f, win_ref[:, lo:hi])
        fb = _dot(f, win_ref[:, hidden + lo:hidden + hi])
        g = (fa * jax.nn.sigmoid(fa) * fb).astype(BF16)
        acc = acc + _dot(g, wout_ref[lo:hi, :])
    o_ref[...] = _rms(h2 + acc, gfin_ref[...])


def _ffn_chunks(hidden):
    tiles = hidden // MXU_TILE
    assert tiles * MXU_TILE == hidden
    cuts = [-(-tiles * c // FFN_CHUNKS) * MXU_TILE for c in range(FFN_CHUNKS + 1)]
    return list(zip(cuts[:-1], cuts[1:]))


def _resident(shape):
    n = len(shape)
    return pl.BlockSpec(shape, lambda *_: (0,) * n, pipeline_mode=pl.Buffered(1))


def _post(x2, out_a, att, gain_mix, w_gate, w_fox_o, w_mix, gain_q, w_q, k_m, v_m, w_o,
          gain_f, w_ffn_in, w_ffn_out, gain_fin, bsz, seq):
    t, d = x2.shape
    tm = TM_POST
    nb = seq // tm
    row = lambda w: pl.BlockSpec((tm, w), lambda b, i: (b * nb + i, 0))
    mem = pl.BlockSpec((1,) + k_m.shape[1:], lambda b, i: (b, 0, 0))
    gain = lambda g: g.reshape(1, d)
    weights = lambda *ws: [_resident(w.shape) for w in ws]
    return pl.pallas_call(
        _post_kernel,
        grid=(bsz, nb),
        in_specs=[row(d), row(d), row(att.shape[1]), _full((1, d))]
                 + weights(w_gate, w_fox_o, w_mix) + [_full((1, d))] + weights(w_q)
                 + [mem, mem] + weights(w_o) + [_full((1, d))] + weights(w_ffn_in, w_ffn_out)
                 + [_full((1, d))],
        out_specs=row(d),
        out_shape=jax.ShapeDtypeStruct((t, d), F32),
        compiler_params=_cparams(("arbitrary", "arbitrary")),
        name="post",
    )(x2, out_a, att, gain(gain_mix), w_gate, w_fox_o, w_mix, gain(gain_q), w_q, k_m, v_m, w_o,
      gain(gain_f), w_ffn_in, w_ffn_out, gain(gain_fin))


def _block_diag(blocks, per):
    g, r, c = blocks.shape
    b = blocks.reshape(g // per, per, r, c)
    eye = jnp.eye(per, dtype=blocks.dtype)
    out = b[:, :, :, None, :] * eye[None, :, None, :, None]
    return out.reshape(g // per, per * r, per * c)


def _seg_perm(q):
    seg = q // SUBLANES
    r = jnp.arange(q)
    src = (r % SUBLANES) * seg + r // SUBLANES
    return (src[:, None] == jnp.arange(q)[None, :]).astype(BF16)


def _layer(x2, mem, bsz, seq, norm_mix, w_in, b_forget, lam_re, lam_im, log_dt, b_re, b_im, c_re,
           c_im, d_skip, w_glu, w_fox_o, w_mix_out, norm_mem_q, norm_mem_kv, w_mem_q, w_mem_kv,
           w_mem_o, norm_ffn, w_ffn_in, w_ffn_out, norm_final):
    d = x2.shape[-1]
    groups, states = lam_re.shape
    ssm_w = groups * SSM_GROUP
    n_heads = b_forget.shape[0]
    fox_w = n_heads * FOX_HEAD_DIM

    a_re, a_im, s_re, s_im, bb_re, bb_im = _ssm_prep(lam_re, lam_im, log_dt, b_re, b_im)
    nstate = groups * states
    per = LANES // SSM_GROUP
    to_rows = lambda a: a.reshape(1, nstate)
    bd_in = lambda bb: _block_diag(
        bb.reshape(SSM_GROUP, groups, states).transpose(1, 0, 2), per).astype(BF16)
    bd_out = lambda c: _block_diag(c.transpose(0, 2, 1), per).astype(BF16)
    perm = _seg_perm(SSM_CHUNK)

    bf = jnp.zeros((1, LANES), F32).at[0, :n_heads].set(b_forget)
    ssm_consts = [perm, perm.T, bd_in(bb_re), bd_in(bb_im), to_rows(a_re), to_rows(a_im),
                  to_rows(s_re), to_rows(s_im), bd_out(c_re), bd_out(c_im),
                  d_skip.reshape(1, ssm_w), w_glu.astype(BF16)]
    post_weights = [w_fox_o, w_mix_out, w_mem_q, w_mem_o, w_ffn_in, w_ffn_out]
    q, k, v, cum_t, out_a, w_gate, *post_bf16 = _in_ssm(
        x2, norm_mix, w_in, bf, ssm_consts, post_weights, ssm_w, fox_w, n_heads, bsz, seq)
    wb_fox_o, wb_mix, wb_mem_q, wb_mem_o, wb_ffn_in, wb_ffn_out = post_bf16

    att = _fox(q, k, v, cum_t.reshape(bsz, n_heads, seq // FOX_TK, FOX_TK), bsz, seq)

    k_m, v_m = _mem_kv(mem, norm_mem_kv, w_mem_kv)
    return _post(x2, out_a, att, norm_mix, w_gate, wb_fox_o, wb_mix, norm_mem_q, wb_mem_q, k_m, v_m,
                 wb_mem_o, norm_ffn, wb_ffn_in, wb_ffn_out, norm_final, bsz, seq)


def kernel(x, mem, norm_mix, w_in, b_forget, lam_re, lam_im, log_dt, b_re, b_im, c_re, c_im, d_skip,
           w_glu, w_fox_o, w_mix_out, norm_mem_q, norm_mem_kv, w_mem_q, w_mem_kv, w_mem_o, norm_ffn,
           w_ffn_in, w_ffn_out, norm_final):
    bsz, seq, d = x.shape
    assert w_in.shape[0] == 1, "single-layer block"
    out = _layer(x.reshape(bsz * seq, d), mem, bsz, seq, norm_mix[0], w_in[0], b_forget[0],
                 lam_re[0], lam_im[0], log_dt[0], b_re[0], b_im[0], c_re[0], c_im[0], d_skip[0],
                 w_glu[0], w_fox_o[0], w_mix_out[0], norm_mem_q[0], norm_mem_kv[0], w_mem_q[0],
                 w_mem_kv[0], w_mem_o[0], norm_ffn[0], w_ffn_in[0], w_ffn_out[0], norm_final)
    return out.reshape(bsz, seq, d)
```

```python
import functools
import math

import jax
import jax.numpy as jnp
from jax import lax
from jax.experimental import pallas as pl
from jax.experimental.pallas import tpu as pltpu

F32 = jnp.float32
BF16 = jnp.bfloat16

RMS_EPS = 1e-6
LOG2E = math.log2(math.e)
SSM_GROUP = 16
FOX_HEAD_DIM = 64
MEM_HEADS = 4
LANES = 128
SUBLANES = 8
MXU_TILE = 256
VMEM_LIMIT = 56 * 1024 * 1024

TM_PROJ = 512
TM_POST = 1024
FFN_CHUNKS = 2
SSM_CHUNK = 256
SSM_SEG = SSM_CHUNK // SUBLANES
SSM_COLS = 512
FOX_TQ = 1024
FOX_TK = 512
FOX_HEADS_PER_STEP = 4
FOX_SKIP_LOG2 = 160.0
FOX_NORM_SLACK = 1.02


def _cparams(sem):
    return pltpu.CompilerParams(dimension_semantics=sem, vmem_limit_bytes=VMEM_LIMIT)


def _rms(x, gain):
    return x * lax.rsqrt(jnp.mean(x * x, axis=-1, keepdims=True) + RMS_EPS) * gain


def _dot(a, b):
    return jnp.dot(a, b, preferred_element_type=F32)


def _dot_nt(a, b):
    return lax.dot_general(a, b, (((1,), (1,)), ((), ())), preferred_element_type=F32)


def _full(shape):
    n = len(shape)
    return pl.BlockSpec(shape, lambda *_: (0,) * n)


def _ssm_prep_kernel(lre_ref, lim_ref, ldt_ref, bre_ref, bim_ref,
                     are_ref, aim_ref, sre_ref, sim_ref, bbre_ref, bbim_ref):
    lre = lre_ref[...]
    lim = lim_ref[...]
    dt = jnp.exp(ldt_ref[...])
    zr = lre * dt
    zi = lim * dt
    mag = jnp.exp(zr)
    are = mag * jnp.cos(zi)
    aim = mag * jnp.sin(zi)
    are_ref[...] = are
    aim_ref[...] = aim
    mag_s = jnp.exp(zr * SSM_SEG)
    sre_ref[...] = mag_s * jnp.cos(zi * SSM_SEG)
    sim_ref[...] = mag_s * jnp.sin(zi * SSM_SEG)
    nr = are - 1.0
    ni = aim
    den = lre * lre + lim * lim
    fr = (nr * lre + ni * lim) / den
    fi = (ni * lre - nr * lim) / den
    bre = bre_ref[...]
    bim = bim_ref[...]
    bbre_ref[...] = fr * bre - fi * bim
    bbim_ref[...] = fr * bim + fi * bre


def _ssm_prep(lam_re, lam_im, log_dt, b_re, b_im):
    g, p = lam_re.shape
    n = b_re.shape[-1]
    c = g * p
    row = lambda a: a.reshape(1, c)
    chan = lambda b: b.transpose(2, 0, 1).reshape(n, c)
    ldt = jnp.broadcast_to(log_dt[:, None], (g, p))
    outs = pl.pallas_call(
        _ssm_prep_kernel,
        out_shape=[jax.ShapeDtypeStruct((1, c), F32)] * 4 + [jax.ShapeDtypeStruct((n, c), F32)] * 2,
        name="ssm_prep",
    )(row(lam_re), row(lam_im), row(ldt), chan(b_re), chan(b_im))
    return outs


def _mem_kv_kernel(m_ref, g_ref, w_ref, k_ref, v_ref):
    n = _rms(m_ref[0], g_ref[...]).astype(BF16)
    kv = _dot(n, w_ref[...])
    half = kv.shape[-1] // 2
    k_ref[0] = kv[:, :half].astype(BF16)
    v_ref[0] = kv[:, half:].astype(BF16)


def _mem_kv(mem, gain, w_kv):
    b, m, d = mem.shape
    w2 = w_kv.shape[-1]
    return pl.pallas_call(
        _mem_kv_kernel,
        grid=(b,),
        in_specs=[pl.BlockSpec((1, m, d), lambda i: (i, 0, 0)), _full((1, d)), _full((d, w2))],
        out_specs=[pl.BlockSpec((1, m, w2 // 2), lambda i: (i, 0, 0))] * 2,
        out_shape=[jax.ShapeDtypeStruct((b, m, w2 // 2), BF16)] * 2,
        compiler_params=_cparams(("arbitrary",)),
        name="mem_kv",
    )(mem, gain.reshape(1, d), w_kv.astype(BF16))


def _gelu_tanh(x):
    return 0.5 * x * (1.0 + jnp.tanh(math.sqrt(2.0 / math.pi) * (x + 0.044715 * (x * x * x))))


def _ssm_drive(u, perm_ref, bre_ref, bim_ref, h_scr):
    n_chunks = u.shape[0] // SSM_CHUNK
    ub = u.astype(BF16)
    up = jnp.concatenate(
        [_dot(perm_ref[...], ub[c * SSM_CHUNK:(c + 1) * SSM_CHUNK]).astype(BF16)
         for c in range(n_chunks)], axis=0)
    kw = bre_ref.shape[1]
    nw = bre_ref.shape[2]
    for kb in range(bre_ref.shape[0]):
        ukb = up[:, kb * kw:(kb + 1) * kw]
        for part, w_ref in enumerate((bre_ref, bim_ref)):
            bu = _dot(ukb, w_ref[kb])
            for c in range(n_chunks):
                h_scr[2 * c + part][:, kb * nw:(kb + 1) * nw] = bu[c * SSM_CHUNK:(c + 1) * SSM_CHUNK]


def _ssm_chunk(u, permt_ref, are_ref, aim_ref, sre_ref, sim_ref, cre_ref, cim_ref, d_ref,
               hre, him, car_re, car_im, anchors=()):
    if callable(anchors):
        anchors = anchors()

    nstate = hre.shape[1]
    for cb in range(nstate // SSM_COLS):
        cs = slice(cb * SSM_COLS, (cb + 1) * SSM_COLS)
        a_re = jnp.broadcast_to(are_ref[:, cs], (SUBLANES, SSM_COLS))
        a_im = jnp.broadcast_to(aim_ref[:, cs], (SUBLANES, SSM_COLS))

        e_re = hre[0:SUBLANES, cs]
        e_im = him[0:SUBLANES, cs]
        for k in range(1, SSM_SEG):
            r = slice(k * SUBLANES, (k + 1) * SUBLANES)
            e_re, e_im = (a_re * e_re - a_im * e_im + hre[r, cs],
                          a_re * e_im + a_im * e_re + him[r, cs])
            hre[r, cs] = e_re
            him[r, cs] = e_im

        s_re = sre_ref[:, cs]
        s_im = sim_ref[:, cs]
        c_re = car_re[:, cs]
        c_im = car_im[:, cs]
        rows_re, rows_im = [], []
        for j in range(SUBLANES):
            rows_re.append(c_re)
            rows_im.append(c_im)
            n_re = s_re * c_re - s_im * c_im + e_re[j:j + 1]
            n_im = s_re * c_im + s_im * c_re + e_im[j:j + 1]
            c_re, c_im = n_re, n_im
        car_re[:, cs] = c_re
        car_im[:, cs] = c_im

        d_re = jnp.concatenate(rows_re, axis=0)
        d_im = jnp.concatenate(rows_im, axis=0)
        for k in range(SSM_SEG):
            r = slice(k * SUBLANES, (k + 1) * SUBLANES)
            d_re, d_im = a_re * d_re - a_im * d_im, a_re * d_im + a_im * d_re
            if k == SSM_SEG // 2:
                for z in anchors[cb::nstate // SSM_COLS]:
                    d_re = d_re + jnp.concatenate([z] * (SSM_COLS // LANES), axis=1)
            hre[r, cs] = hre[r, cs] + d_re
            him[r, cs] = him[r, cs] + d_im

    ncb = cre_ref.shape[0]
    cw = cre_ref.shape[1]
    ys = []
    for kb in range(ncb):
        h_re = hre[:, kb * cw:(kb + 1) * cw].astype(BF16)
        h_im = him[:, kb * cw:(kb + 1) * cw].astype(BF16)
        ys.append(_dot(h_re, cre_ref[kb]) - _dot(h_im, cim_ref[kb]))
    yp = jnp.concatenate(ys, axis=-1)
    hi = yp.astype(BF16)
    lo = (yp - hi.astype(F32)).astype(BF16)
    y = _dot(permt_ref[...], hi) + _dot(permt_ref[...], lo)
    y = y + d_ref[...] * u
    return _gelu_tanh(y).astype(BF16)


def _glu(g, wglu_ref):
    z = _dot(g, wglu_ref[...])
    half = z.shape[-1] // 2
    return z[:, :half] * jax.nn.sigmoid(z[:, half:])


def _schedule_anchor(x):
    rows, cols = x.shape
    s = x[:, :LANES]
    for c in range(1, cols // LANES):
        s = s + x[:, c * LANES:(c + 1) * LANES]
    s = jnp.sum(s.reshape(rows // SUBLANES, SUBLANES, LANES), axis=0)
    bits = pltpu.bitcast(s, jnp.uint32)
    return ((bits >> 16) >> 16).astype(F32)


def _in_ssm_kernel(x_ref, g_ref, wa_ref, wb_ref, wc_ref, bf_ref, *rest, n_cast):
    nconst = 12
    (perm_ref, permt_ref, bre_ref, bim_ref, are_ref, aim_ref, sre_ref, sim_ref,
     cre_ref, cim_ref, d_ref, wglu_ref) = rest[:nconst]
    cast_in = rest[nconst:nconst + n_cast]
    outs = rest[nconst + n_cast:]
    q_ref, k_ref, v_ref, cum_ref, oa_ref, wg_ref = outs[:6]
    cast_out = outs[6:6 + n_cast]
    carry_ref, car_re, car_im, w_bf, wf_bf = outs[6 + n_cast:11 + n_cast]
    h_scr = outs[11 + n_cast:]
    n_forget = cum_ref.shape[1]
    for src, dst in zip(cast_in, cast_out):
        dst[...] = src[...].astype(BF16)

    @pl.when(pl.program_id(1) == 0)
    def _():
        carry_ref[...] = jnp.zeros_like(carry_ref)
        car_re[...] = jnp.zeros_like(car_re)
        car_im[...] = jnp.zeros_like(car_im)
        w_bf[...] = wa_ref[...].T.astype(BF16)
        wf_bf[...] = wb_ref[:LANES, :].T.astype(BF16)
        wg_ref[...] = jnp.concatenate([wb_ref[n_forget:, :], wc_ref[...]], axis=0).T.astype(BF16)

    un = _rms(x_ref[...], g_ref[...]).astype(BF16)
    wu = d_ref.shape[1]
    wh = q_ref.shape[1]
    col = lambda lo, hi: _dot(un, w_bf[:, lo:hi])
    u = col(0, wu)
    pieces = [(ref, lo + p * MXU_TILE, p * MXU_TILE, scale)
              for ref, lo, scale in ((q_ref, wu, FOX_HEAD_DIM ** -0.5 * LOG2E),
                                     (k_ref, wu + wh, None), (v_ref, wu + 2 * wh, None))
              for p in range(wh // MXU_TILE)]
    n_chunks = u.shape[0] // SSM_CHUNK
    per_chunk = -(-len(pieces) // n_chunks)

    def project(todo):
        anchors = []
        for ref, src, dst, scale in todo:
            piece = col(src, src + MXU_TILE)
            if scale is not None:
                piece = piece * scale
            ref[:, dst:dst + MXU_TILE] = piece.astype(BF16)
            anchors.append(_schedule_anchor(piece))
        return anchors

    _ssm_drive(u, perm_ref, bre_ref, bim_ref, h_scr)
    acts = []
    for c in range(n_chunks):
        rows = slice(c * SSM_CHUNK, (c + 1) * SSM_CHUNK)
        todo = pieces[c * per_chunk:(c + 1) * per_chunk]
        acts.append(_ssm_chunk(u[rows], permt_ref, are_ref, aim_ref, sre_ref, sim_ref, cre_ref,
                               cim_ref, d_ref, h_scr[2 * c], h_scr[2 * c + 1], car_re, car_im,
                               functools.partial(project, todo)))
    oa_ref[...] = _glu(jnp.concatenate(acts, axis=0), wglu_ref)
    c = jax.nn.log_sigmoid(_dot(un, wf_bf[...]) + bf_ref[...])
    rows = c.shape[0]
    row = lax.broadcasted_iota(jnp.int32, c.shape, 0)
    shift = 1
    while shift < rows:
        c = c + jnp.where(row >= shift, pltpu.roll(c, shift, 0), 0.0)
        shift *= 2
    c = c + carry_ref[...]
    carry_ref[...] = c[rows - 1:rows, :]
    cum_ref[0] = (c * LOG2E).T[:cum_ref.shape[1], :]


def _in_ssm(x2, gain, w_in, bf, ssm_consts, later_weights, ssm_w, fox_w, n_heads, bsz, seq):
    t, d = x2.shape
    tm = TM_PROJ
    nb = seq // tm
    steps = bsz * nb
    slabs = [w.reshape(steps, w.shape[0] // steps, w.shape[1]) for w in later_weights]
    slab = lambda s: pl.BlockSpec((1,) + s.shape[1:], lambda b, i: (b * nb + i, 0, 0))
    assert tm % SSM_CHUNK == 0 and len(ssm_consts) == 12
    nstate = ssm_consts[4].shape[1]
    dm = ssm_consts[11].shape[1] // 2
    main_w = ssm_w + 3 * fox_w
    gate_w = w_in.shape[1] - main_w - n_heads
    assert gate_w == main_w and main_w % LANES == 0 and n_heads == SUBLANES
    win = lambda rows, idx: pl.BlockSpec((rows, d), lambda *_: (idx, 0),
                                         pipeline_mode=pl.Buffered(1))
    w_t = w_in.T
    row = lambda w: pl.BlockSpec((tm, w), lambda b, i: (b * nb + i, 0))
    h_scr = [pltpu.VMEM((SSM_CHUNK, nstate), F32)] * (2 * (tm // SSM_CHUNK))
    outs = pl.pallas_call(
        functools.partial(_in_ssm_kernel, n_cast=len(slabs)),
        grid=(bsz, nb),
        in_specs=[row(d), _full((1, d)), win(main_w, 0), win(main_w, 1),
                  win(n_heads, 2 * main_w // n_heads), _full((1, LANES))]
                 + [_full(c.shape) for c in ssm_consts] + [slab(s) for s in slabs],
        out_specs=[row(fox_w), row(fox_w), row(fox_w),
                   pl.BlockSpec((1, n_heads, tm), lambda b, i: (b, 0, i)), row(dm),
                   _full((d, gate_w))] + [slab(s) for s in slabs],
        out_shape=[jax.ShapeDtypeStruct((t, fox_w), BF16),
                   jax.ShapeDtypeStruct((t, fox_w), BF16),
                   jax.ShapeDtypeStruct((t, fox_w), BF16),
                   jax.ShapeDtypeStruct((bsz, n_heads, seq), F32),
                   jax.ShapeDtypeStruct((t, dm), F32),
                   jax.ShapeDtypeStruct((d, gate_w), BF16)]
                  + [jax.ShapeDtypeStruct(s.shape, BF16) for s in slabs],
        scratch_shapes=[pltpu.VMEM((1, LANES), F32), pltpu.VMEM((1, nstate), F32),
                        pltpu.VMEM((1, nstate), F32), pltpu.VMEM((d, main_w), BF16),
                        pltpu.VMEM((d, LANES), BF16)] + h_scr,
        compiler_params=_cparams(("arbitrary", "arbitrary")),
        name="in_ssm",
    )(x2, gain.reshape(1, d), w_t, w_t, w_t, bf, *ssm_consts, *slabs)
    cast = [o.reshape(w.shape) for o, w in zip(outs[6:], later_weights)]
    return list(outs[:6]) + cast


def _fox_kernel(q_ref, k_ref, v_ref, ck_ref, o_ref, m_scr, acc_scr, first_blk):
    seq = q_ref.shape[0]
    tq = FOX_TQ
    tk = FOX_TK
    dh = FOX_HEAD_DIM
    n_heads = ck_ref.shape[1]
    lane = lax.broadcasted_iota(jnp.int32, (1, LANES), 1)
    keep = [jnp.where(lane < dh, 1.0, 0.0).astype(BF16), jnp.where(lane >= dh, 1.0, 0.0).astype(BF16)]
    sum_lane = [dh, 0]
    ones = [jnp.where(lane == sl, 1.0, 0.0).astype(BF16) for sl in sum_lane]

    def block(q, j, rows, h, masked):
        e = h % 2
        ps = slice((h // 2) * LANES, (h // 2 + 1) * LANES)
        hs = slice(h * LANES, (h + 1) * LANES)
        ks = pl.ds(pl.multiple_of(j * tk, tk), tk)
        s = _dot_nt(q, k_ref[ks, ps] * keep[e]) - ck_ref[0, h, pl.ds(j, 1), :]
        if masked:
            r = lax.broadcasted_iota(jnp.int32, (tk, tk), 0)
            c = lax.broadcasted_iota(jnp.int32, (tk, tk), 1)
            top = jnp.where(c <= r, s[:tk], -jnp.inf)
            s = top if s.shape[0] == tk else jnp.concatenate([top, s[tk:]], axis=0)
        m_prev = m_scr[rows, hs]
        m_new = jnp.maximum(m_prev, jnp.max(s, axis=1, keepdims=True))
        p = jnp.exp2(s - jnp.concatenate([m_new] * (tk // LANES), axis=1))
        acc_scr[rows, hs] = (jnp.exp2(m_prev - m_new) * acc_scr[rows, hs]
                             + _dot(p.astype(BF16), v_ref[ks, ps] * keep[e] + ones[e]))
        m_scr[rows, hs] = m_new

    nsub = tq // tk
    nq = seq // tq
    nk = seq // tk
    all_rows = slice(0, tq)

    col = lax.broadcasted_iota(jnp.int32, (q_ref.shape[1], LANES), 0) // dh
    head_sel = jnp.where(col == lax.broadcasted_iota(jnp.int32, (q_ref.shape[1], LANES), 1),
                         1.0, 0.0).astype(BF16)

    def max_sq_norm(ref, lo, n):
        x = ref[lo:lo + n, :]
        sq = _dot(x * x, head_sel)
        return jnp.max(sq, axis=0, keepdims=True) * FOX_NORM_SLACK

    k_sq = jnp.concatenate([max_sq_norm(k_ref, j * tk, tk) for j in range(nk)], axis=0)
    q_sq = jnp.concatenate([max_sq_norm(q_ref, i * tq, tq) for i in range(nq)], axis=0)
    blk_id = lax.broadcasted_iota(jnp.int32, (nk, 1), 0).astype(F32)
    for h in range(n_heads):
        k_max = jnp.sqrt(k_sq[:, h:h + 1])
        c_end = ck_ref[0, h, :, tk - 1:tk]
        for i in range(nq):
            first_diag = float(i * nsub)
            q_max = jnp.sqrt(q_sq[i:i + 1, h:h + 1])
            k_own = jnp.max(k_max[i * nsub:(i + 1) * nsub], axis=0, keepdims=True)
            c_start = ck_ref[0, h, i * nsub:i * nsub + 1, 0:1]
            bound = q_max * (k_max + k_own) + (c_start - c_end)
            visit = jnp.logical_and(bound >= -FOX_SKIP_LOG2, blk_id < first_diag)
            first_blk[h * nq + i] = jnp.min(jnp.where(visit, blk_id, first_diag)).astype(jnp.int32)

    def qblock(qi, _):
        qs = pl.ds(pl.multiple_of(qi * tq, tq), tq)
        qp = [q_ref[qs, p * LANES:(p + 1) * LANES] for p in range(n_heads // 2)]
        m_scr[...] = jnp.full(m_scr.shape, -jnp.inf, F32)
        acc_scr[...] = jnp.zeros(acc_scr.shape, F32)

        def full(j, _):
            for h in range(n_heads):
                block(qp[h // 2], j, all_rows, h, False)
            return 0

        starts = [first_blk[h * nq + qi] for h in range(n_heads)]
        common = functools.reduce(jnp.maximum, starts)
        for h in range(n_heads):
            def one(j, _, h=h):
                block(qp[h // 2], j, all_rows, h, False)
                return 0

            def two(t, _, h=h):
                block(qp[h // 2], starts[h] + 2 * t, all_rows, h, False)
                block(qp[h // 2], starts[h] + 2 * t + 1, all_rows, h, False)
                return 0

            pairs = lax.shift_right_logical(common - starts[h], 1)
            lax.fori_loop(0, pairs, two, 0)
            lax.fori_loop(starts[h] + 2 * pairs, common, one, 0)
        lax.fori_loop(common, qi * nsub, full, 0)
        for r in range(nsub):
            rows = slice(r * tk, tq)
            for h in range(n_heads):
                block(qp[h // 2][rows], qi * nsub + r, rows, h, True)
        for p in range(n_heads // 2):
            a0 = acc_scr[:, (2 * p) * LANES:(2 * p + 1) * LANES]
            a1 = acc_scr[:, (2 * p + 1) * LANES:(2 * p + 2) * LANES]
            out = jnp.where(lane < dh, a0 / a0[:, sum_lane[0]:sum_lane[0] + 1],
                            a1 / a1[:, sum_lane[1]:sum_lane[1] + 1])
            o_ref[qs, p * LANES:(p + 1) * LANES] = out.astype(o_ref.dtype)
        return 0

    lax.fori_loop(0, seq // tq, qblock, 0)


def _fox(q, k, v, cum_t, bsz, seq):
    t, w = q.shape
    assert FOX_TQ % FOX_TK == 0 and seq % FOX_TQ == 0 and FOX_HEADS_PER_STEP % 2 == 0
    hps = FOX_HEADS_PER_STEP
    wblk = hps * FOX_HEAD_DIM
    blk = pl.BlockSpec((seq, wblk), lambda bi, hi: (bi, hi))
    return pl.pallas_call(
        _fox_kernel,
        grid=(bsz, w // wblk),
        in_specs=[blk, blk, blk,
                  pl.BlockSpec((1, hps) + cum_t.shape[2:], lambda bi, hi: (bi, hi, 0, 0))],
        out_specs=blk,
        out_shape=jax.ShapeDtypeStruct((t, w), BF16),
        scratch_shapes=[pltpu.VMEM((FOX_TQ, hps * LANES), F32),
                        pltpu.VMEM((FOX_TQ, hps * LANES), F32),
                        pltpu.SMEM((hps * (seq // FOX_TQ),), jnp.int32)],
        compiler_params=_cparams(("arbitrary", "arbitrary")),
        name="fox",
    )(q, k, v, cum_t)


def _merge_kernel(x_ref, oa_ref, att_ref, gmix_ref, wg_ref, wfo_ref, wmix_ref, gq_ref, wq_ref,
                  km_ref, vm_ref, wo_ref, h_ref):
    d = x_ref.shape[-1]
    x = x_ref[...]
    gate = jax.nn.sigmoid(_dot(_rms(x, gmix_ref[...]).astype(BF16), wg_ref[...]))
    out_b = _dot(att_ref[...], wfo_ref[...])
    mix = gate[:, :d] * oa_ref[...] + gate[:, d:] * out_b
    h1 = x + _dot(mix.astype(BF16), wmix_ref[...])

    n = _rms(h1, gq_ref[...]).astype(BF16)
    qm = _dot(n, wq_ref[...])
    hd = qm.shape[-1] // MEM_HEADS
    qm = (qm * (hd ** -0.5)).astype(BF16)
    outs = []
    for hh in range(MEM_HEADS):
        hs = slice(hh * hd, (hh + 1) * hd)
        s = _dot_nt(qm[:, hs], km_ref[0, :, hs])
        s = s - jnp.max(s, axis=-1, keepdims=True)
        p = jnp.exp(s)
        p = p / jnp.sum(p, axis=-1, keepdims=True)
        outs.append(_dot(p.astype(BF16), vm_ref[0, :, hs]))
    o = jnp.concatenate(outs, axis=-1).astype(BF16)
    h_ref[...] = h1 + _dot(o, wo_ref[...])


def _ffn_kernel(h_ref, gf_ref, win_ref, wout_ref, gfin_ref, o_ref):
    h2 = h_ref[...]
    f = _rms(h2, gf_ref[...]).astype(BF16)
    hidden = wout_ref.shape[0]
    acc = jnp.zeros_like(h2)
    for lo, hi in _ffn_chunks(hidden):
        fa = _dot(f, win_ref[:, lo:hi])
        fb = _dot(f, win_ref[:, hidden + lo:hidden + hi])
        g = (fa * jax.nn.sigmoid(fa) * fb).astype(BF16)
        acc = acc + _dot(g, wout_ref[lo:hi, :])
    o_ref[...] = _rms(h2 + acc, gfin_ref[...])


def _ffn_chunks(hidden):
    tiles = hidden // MXU_TILE
    assert tiles * MXU_TILE == hidden
    cuts = [-(-tiles * c // FFN_CHUNKS) * MXU_TILE for c in range(FFN_CHUNKS + 1)]
    return list(zip(cuts[:-1], cuts[1:]))


def _resident(shape):
    n = len(shape)
    return pl.BlockSpec(shape, lambda *_: (0,) * n, pipeline_mode=pl.Buffered(1))


def _post(x2, out_a, att, gain_mix, w_gate, w_fox_o, w_mix, gain_q, w_q, k_m, v_m, w_o,
          gain_f, w_ffn_in, w_ffn_out, gain_fin, bsz, seq):
    t, d = x2.shape
    tm = TM_POST
    nb = seq // tm
    row = lambda w: pl.BlockSpec((tm, w), lambda b, i: (b * nb + i, 0))
    mem = pl.BlockSpec((1,) + k_m.shape[1:], lambda b, i: (b, 0, 0))
    gain = lambda g: g.reshape(1, d)
    weights = lambda *ws: [_resident(w.shape) for w in ws]
    h2 = pl.pallas_call(
        _merge_kernel,
        grid=(bsz, nb),
        in_specs=[row(d), row(d), row(att.shape[1]), _full((1, d))]
                 + weights(w_gate, w_fox_o, w_mix) + [_full((1, d))] + weights(w_q)
                 + [mem, mem] + weights(w_o),
        out_specs=row(d),
        out_shape=jax.ShapeDtypeStruct((t, d), F32),
        compiler_params=_cparams(("arbitrary", "arbitrary")),
        name="merge",
    )(x2, out_a, att, gain(gain_mix), w_gate, w_fox_o, w_mix, gain(gain_q), w_q, k_m, v_m, w_o)
    blk = pl.BlockSpec((tm, d), lambda i: (i, 0))
    return pl.pallas_call(
        _ffn_kernel,
        grid=(t // tm,),
        in_specs=[blk, _full((1, d))] + weights(w_ffn_in, w_ffn_out) + [_full((1, d))],
        out_specs=blk,
        out_shape=jax.ShapeDtypeStruct((t, d), F32),
        compiler_params=_cparams(("arbitrary",)),
        name="ffn",
    )(h2, gain(gain_f), w_ffn_in, w_ffn_out, gain(gain_fin))


def _block_diag(blocks, per):
    g, r, c = blocks.shape
    b = blocks.reshape(g // per, per, r, c)
    eye = jnp.eye(per, dtype=blocks.dtype)
    out = b[:, :, :, None, :] * eye[None, :, None, :, None]
    return out.reshape(g // per, per * r, per * c)


def _seg_perm(q):
    seg = q // SUBLANES
    r = jnp.arange(q)
    src = (r % SUBLANES) * seg + r // SUBLANES
    return (src[:, None] == jnp.arange(q)[None, :]).astype(BF16)


def _layer(x2, mem, bsz, seq, norm_mix, w_in, b_forget, lam_re, lam_im, log_dt, b_re, b_im, c_re,
           c_im, d_skip, w_glu, w_fox_o, w_mix_out, norm_mem_q, norm_mem_kv, w_mem_q, w_mem_kv,
           w_mem_o, norm_ffn, w_ffn_in, w_ffn_out, norm_final):
    d = x2.shape[-1]
    groups, states = lam_re.shape
    ssm_w = groups * SSM_GROUP
    n_heads = b_forget.shape[0]
    fox_w = n_heads * FOX_HEAD_DIM

    a_re, a_im, s_re, s_im, bb_re, bb_im = _ssm_prep(lam_re, lam_im, log_dt, b_re, b_im)
    nstate = groups * states
    per = LANES // SSM_GROUP
    to_rows = lambda a: a.reshape(1, nstate)
    bd_in = lambda bb: _block_diag(
        bb.reshape(SSM_GROUP, groups, states).transpose(1, 0, 2), per).astype(BF16)
    bd_out = lambda c: _block_diag(c.transpose(0, 2, 1), per).astype(BF16)
    perm = _seg_perm(SSM_CHUNK)

    bf = jnp.zeros((1, LANES), F32).at[0, :n_heads].set(b_forget)
    ssm_consts = [perm, perm.T, bd_in(bb_re), bd_in(bb_im), to_rows(a_re), to_rows(a_im),
                  to_rows(s_re), to_rows(s_im), bd_out(c_re), bd_out(c_im),
                  d_skip.reshape(1, ssm_w), w_glu.astype(BF16)]
    post_weights = [w_fox_o, w_mix_out, w_mem_q, w_mem_o, w_ffn_in, w_ffn_out]
    q, k, v, cum_t, out_a, w_gate, *post_bf16 = _in_ssm(
        x2, norm_mix, w_in, bf, ssm_consts, post_weights, ssm_w, fox_w, n_heads, bsz, seq)
    wb_fox_o, wb_mix, wb_mem_q, wb_mem_o, wb_ffn_in, wb_ffn_out = post_bf16

    att = _fox(q, k, v, cum_t.reshape(bsz, n_heads, seq // FOX_TK, FOX_TK), bsz, seq)

    k_m, v_m = _mem_kv(mem, norm_mem_kv, w_mem_kv)
    return _post(x2, out_a, att, norm_mix, w_gate, wb_fox_o, wb_mix, norm_mem_q, wb_mem_q, k_m, v_m,
                 wb_mem_o, norm_ffn, wb_ffn_in, wb_ffn_out, norm_final, bsz, seq)


def kernel(x, mem, norm_mix, w_in, b_forget, lam_re, lam_im, log_dt, b_re, b_im, c_re, c_im, d_skip,
           w_glu, w_fox_o, w_mix_out, norm_mem_q, norm_mem_kv, w_mem_q, w_mem_kv, w_mem_o, norm_ffn,
           w_ffn_in, w_ffn_out, norm_final):
    bsz, seq, d = x.shape
    assert w_in.shape[0] == 1, "single-layer block"
    out = _layer(x.reshape(bsz * seq, d), mem, bsz, seq, norm_mix[0], w_in[0], b_forget[0],
                 lam_re[0], lam_im[0], log_dt[0], b_re[0], b_im[0], c_re[0], c_im[0], d_skip[0],
                 w_glu[0], w_fox_o[0], w_mix_out[0], norm_mem_q[0], norm_mem_kv[0], w_mem_q[0],
                 w_mem_kv[0], w_mem_o[0], norm_ffn[0], w_ffn_in[0], w_ffn_out[0], norm_final)
    return out.reshape(bsz, seq, d)
```

```python
import functools
import math

import jax
import jax.numpy as jnp
from jax import lax
from jax.experimental import pallas as pl
from jax.experimental.pallas import tpu as pltpu

F32 = jnp.float32
BF16 = jnp.bfloat16

RMS_EPS = 1e-6
LOG2E = math.log2(math.e)
SSM_GROUP = 16
FOX_HEAD_DIM = 64
MEM_HEADS = 4
LANES = 128
SUBLANES = 8
MXU_TILE = 256
VMEM_LIMIT = 56 * 1024 * 1024

TM_PROJ = 512
TM_POST = 1024
FFN_CHUNKS = 2
SSM_CHUNK = 256
SSM_SEG = SSM_CHUNK // SUBLANES
SSM_COLS = 512
FOX_TQ = 1024
FOX_TK = 512
FOX_HEADS_PER_STEP = 4
FOX_SKIP_LOG2 = 160.0
FOX_NORM_SLACK = 1.02


def _cparams(sem):
    return pltpu.CompilerParams(dimension_semantics=sem, vmem_limit_bytes=VMEM_LIMIT)


def _rms(x, gain):
    return x * lax.rsqrt(jnp.mean(x * x, axis=-1, keepdims=True) + RMS_EPS) * gain


def _dot(a, b):
    return jnp.dot(a, b, preferred_element_type=F32)


def _dot_nt(a, b):
    return lax.dot_general(a, b, (((1,), (1,)), ((), ())), preferred_element_type=F32)


def _full(shape):
    n = len(shape)
    return pl.BlockSpec(shape, lambda *_: (0,) * n)


def _ssm_prep_kernel(lre_ref, lim_ref, ldt_ref, bre_ref, bim_ref, cre_ref, cim_ref,
                     are_ref, aim_ref, sre_ref, sim_ref, bbre_ref, bbim_ref, ccre_ref, ccim_ref):
    lre = lre_ref[...]
    lim = lim_ref[...]
    dt = jnp.exp(ldt_ref[...])
    zr = lre * dt
    zi = lim * dt
    mag = jnp.exp(zr)
    are = mag * jnp.cos(zi)
    aim = mag * jnp.sin(zi)
    are_ref[...] = are
    aim_ref[...] = aim
    mag_s = jnp.exp(zr * SSM_SEG)
    sre_ref[...] = mag_s * jnp.cos(zi * SSM_SEG)
    sim_ref[...] = mag_s * jnp.sin(zi * SSM_SEG)
    nr = are - 1.0
    ni = aim
    den = lre * lre + lim * lim
    fr = (nr * lre + ni * lim) / den
    fi = (ni * lre - nr * lim) / den
    bre = bre_ref[...]
    bim = bim_ref[...]
    nblk, rows, cols = bbre_ref.shape
    n = bre.shape[0]
    per = rows // n
    same_group = (lax.broadcasted_iota(jnp.int32, (rows, cols), 0) // n
                  == lax.broadcasted_iota(jnp.int32, (rows, cols), 1) // (cols // per))

    def block_diag(x, kb):
        slab = x[:, kb * cols:(kb + 1) * cols]
        return jnp.where(same_group, jnp.concatenate([slab] * per, axis=0), 0.0)

    b_bar = (fr * bre - fi * bim, fr * bim + fi * bre)
    for kb in range(nblk):
        bbre_ref[kb] = block_diag(b_bar[0], kb).astype(BF16)
        bbim_ref[kb] = block_diag(b_bar[1], kb).astype(BF16)
        ccre_ref[kb] = block_diag(cre_ref[...], kb).T.astype(BF16)
        ccim_ref[kb] = block_diag(cim_ref[...], kb).T.astype(BF16)


def _ssm_prep(lam_re, lam_im, log_dt, b_re, b_im, c_re, c_im):
    g, p = lam_re.shape
    n = b_re.shape[-1]
    c = g * p
    per = LANES // n
    nblk = g // per
    row = lambda a: a.reshape(1, c)
    ldt = jnp.broadcast_to(log_dt[:, None], (g, p))
    blk = lambda r, cc: jax.ShapeDtypeStruct((nblk, r, cc), BF16)
    return pl.pallas_call(
        _ssm_prep_kernel,
        out_shape=[jax.ShapeDtypeStruct((1, c), F32)] * 4
                  + [blk(LANES, per * p)] * 2 + [blk(per * p, LANES)] * 2,
        name="ssm_prep",
    )(row(lam_re), row(lam_im), row(ldt),
      b_re.transpose(2, 0, 1).reshape(n, c), b_im.transpose(2, 0, 1).reshape(n, c),
      c_re.transpose(1, 0, 2).reshape(n, c), c_im.transpose(1, 0, 2).reshape(n, c))


def _mem_kv_kernel(m_ref, g_ref, w_ref, k_ref, v_ref):
    n = _rms(m_ref[0], g_ref[...]).astype(BF16)
    kv = _dot(n, w_ref[...].astype(BF16))
    half = kv.shape[-1] // 2
    k_ref[0] = kv[:, :half].astype(BF16)
    v_ref[0] = kv[:, half:].astype(BF16)


def _mem_kv(mem, gain, w_kv):
    b, m, d = mem.shape
    w2 = w_kv.shape[-1]
    return pl.pallas_call(
        _mem_kv_kernel,
        grid=(b,),
        in_specs=[pl.BlockSpec((1, m, d), lambda i: (i, 0, 0)), _full((1, d)), _full((d, w2))],
        out_specs=[pl.BlockSpec((1, m, w2 // 2), lambda i: (i, 0, 0))] * 2,
        out_shape=[jax.ShapeDtypeStruct((b, m, w2 // 2), BF16)] * 2,
        compiler_params=_cparams(("arbitrary",)),
        name="mem_kv",
    )(mem, gain.reshape(1, d), w_kv)


def _gelu_tanh(x):
    return 0.5 * x * (1.0 + jnp.tanh(math.sqrt(2.0 / math.pi) * (x + 0.044715 * (x * x * x))))


def _ssm_drive(u, perm_ref, bre_ref, bim_ref, h_scr):
    n_chunks = u.shape[0] // SSM_CHUNK
    ub = u.astype(BF16)
    up = jnp.concatenate(
        [_dot(perm_ref[...], ub[c * SSM_CHUNK:(c + 1) * SSM_CHUNK]).astype(BF16)
         for c in range(n_chunks)], axis=0)
    kw = bre_ref.shape[1]
    nw = bre_ref.shape[2]
    for kb in range(bre_ref.shape[0]):
        ukb = up[:, kb * kw:(kb + 1) * kw]
        for part, w_ref in enumerate((bre_ref, bim_ref)):
            bu = _dot(ukb, w_ref[kb])
            for c in range(n_chunks):
                h_scr[2 * c + part][:, kb * nw:(kb + 1) * nw] = bu[c * SSM_CHUNK:(c + 1) * SSM_CHUNK]


def _ssm_chunk(u, permt_ref, are_ref, aim_ref, sre_ref, sim_ref, cre_ref, cim_ref, d_ref,
               hre, him, car_re, car_im, anchors=()):
    if callable(anchors):
        anchors = anchors()

    nstate = hre.shape[1]
    for cb in range(nstate // SSM_COLS):
        cs = slice(cb * SSM_COLS, (cb + 1) * SSM_COLS)
        a_re = jnp.broadcast_to(are_ref[:, cs], (SUBLANES, SSM_COLS))
        a_im = jnp.broadcast_to(aim_ref[:, cs], (SUBLANES, SSM_COLS))

        e_re = hre[0:SUBLANES, cs]
        e_im = him[0:SUBLANES, cs]
        for k in range(1, SSM_SEG):
            r = slice(k * SUBLANES, (k + 1) * SUBLANES)
            e_re, e_im = (a_re * e_re - a_im * e_im + hre[r, cs],
                          a_re * e_im + a_im * e_re + him[r, cs])
            hre[r, cs] = e_re
            him[r, cs] = e_im

        s_re = sre_ref[:, cs]
        s_im = sim_ref[:, cs]
        c_re = car_re[:, cs]
        c_im = car_im[:, cs]
        rows_re, rows_im = [], []
        for j in range(SUBLANES):
            rows_re.append(c_re)
            rows_im.append(c_im)
            n_re = s_re * c_re - s_im * c_im + e_re[j:j + 1]
            n_im = s_re * c_im + s_im * c_re + e_im[j:j + 1]
            c_re, c_im = n_re, n_im
        car_re[:, cs] = c_re
        car_im[:, cs] = c_im

        d_re = jnp.concatenate(rows_re, axis=0)
        d_im = jnp.concatenate(rows_im, axis=0)
        for k in range(SSM_SEG):
            r = slice(k * SUBLANES, (k + 1) * SUBLANES)
            d_re, d_im = a_re * d_re - a_im * d_im, a_re * d_im + a_im * d_re
            if k == SSM_SEG // 2:
                for z in anchors[cb::nstate // SSM_COLS]:
                    d_re = d_re + jnp.concatenate([z] * (SSM_COLS // LANES), axis=1)
            hre[r, cs] = hre[r, cs] + d_re
            him[r, cs] = him[r, cs] + d_im

    ncb = cre_ref.shape[0]
    cw = cre_ref.shape[1]
    ys = []
    for kb in range(ncb):
        h_re = hre[:, kb * cw:(kb + 1) * cw].astype(BF16)
        h_im = him[:, kb * cw:(kb + 1) * cw].astype(BF16)
        ys.append(_dot(h_re, cre_ref[kb]) - _dot(h_im, cim_ref[kb]))
    yp = jnp.concatenate(ys, axis=-1)
    hi = yp.astype(BF16)
    lo = (yp - hi.astype(F32)).astype(BF16)
    y = _dot(permt_ref[...], hi) + _dot(permt_ref[...], lo)
    y = y + d_ref[...] * u
    return _gelu_tanh(y).astype(BF16)


def _glu(g, wglu_ref):
    z = _dot(g, wglu_ref[...])
    half = z.shape[-1] // 2
    return z[:, :half] * jax.nn.sigmoid(z[:, half:])


def _schedule_anchor(x):
    rows, cols = x.shape
    s = x[:, :LANES]
    for c in range(1, cols // LANES):
        s = s + x[:, c * LANES:(c + 1) * LANES]
    s = jnp.sum(s.reshape(rows // SUBLANES, SUBLANES, LANES), axis=0)
    bits = pltpu.bitcast(s, jnp.uint32)
    return ((bits >> 16) >> 16).astype(F32)


def _in_ssm_kernel(x_ref, g_ref, wa_ref, wb_ref, wc_ref, bf_ref, *rest, n_cast):
    nconst = 12
    (perm_ref, permt_ref, bre_ref, bim_ref, are_ref, aim_ref, sre_ref, sim_ref,
     cre_ref, cim_ref, d_ref, wglu_ref) = rest[:nconst]
    cast_in = rest[nconst:nconst + n_cast]
    outs = rest[nconst + n_cast:]
    q_ref, k_ref, v_ref, cum_ref, oa_ref, wg_ref = outs[:6]
    cast_out = outs[6:6 + n_cast]
    carry_ref, car_re, car_im, w_bf, wf_bf = outs[6 + n_cast:11 + n_cast]
    h_scr = outs[11 + n_cast:]
    n_forget = cum_ref.shape[1]
    for src, dst in zip(cast_in, cast_out):
        dst[...] = src[...].astype(BF16)

    @pl.when(pl.program_id(1) == 0)
    def _():
        carry_ref[...] = jnp.zeros_like(carry_ref)
        car_re[...] = jnp.zeros_like(car_re)
        car_im[...] = jnp.zeros_like(car_im)
        w_bf[...] = wa_ref[...].T.astype(BF16)
        wf_bf[...] = wb_ref[:LANES, :].T.astype(BF16)
        wg_ref[...] = jnp.concatenate([wb_ref[n_forget:, :], wc_ref[...]], axis=0).T.astype(BF16)

    un = _rms(x_ref[...], g_ref[...]).astype(BF16)
    wu = d_ref.shape[1]
    wh = q_ref.shape[1]
    col = lambda lo, hi: _dot(un, w_bf[:, lo:hi])
    u = col(0, wu)
    pieces = [(ref, lo + p * MXU_TILE, p * MXU_TILE, scale)
              for ref, lo, scale in ((q_ref, wu, FOX_HEAD_DIM ** -0.5 * LOG2E),
                                     (k_ref, wu + wh, None), (v_ref, wu + 2 * wh, None))
              for p in range(wh // MXU_TILE)]
    n_chunks = u.shape[0] // SSM_CHUNK
    per_chunk = -(-len(pieces) // n_chunks)

    def project(todo):
        anchors = []
        for ref, src, dst, scale in todo:
            piece = col(src, src + MXU_TILE)
            if scale is not None:
                piece = piece * scale
            ref[:, dst:dst + MXU_TILE] = piece.astype(BF16)
            anchors.append(_schedule_anchor(piece))
        return anchors

    _ssm_drive(u, perm_ref, bre_ref, bim_ref, h_scr)
    acts = []
    for c in range(n_chunks):
        rows = slice(c * SSM_CHUNK, (c + 1) * SSM_CHUNK)
        todo = pieces[c * per_chunk:(c + 1) * per_chunk]
        acts.append(_ssm_chunk(u[rows], permt_ref, are_ref, aim_ref, sre_ref, sim_ref, cre_ref,
                               cim_ref, d_ref, h_scr[2 * c], h_scr[2 * c + 1], car_re, car_im,
                               functools.partial(project, todo)))
    oa_ref[...] = _glu(jnp.concatenate(acts, axis=0), wglu_ref)
    c = jax.nn.log_sigmoid(_dot(un, wf_bf[...]) + bf_ref[...])
    rows = c.shape[0]
    row = lax.broadcasted_iota(jnp.int32, c.shape, 0)
    shift = 1
    while shift < rows:
        c = c + jnp.where(row >= shift, pltpu.roll(c, shift, 0), 0.0)
        shift *= 2
    c = c + carry_ref[...]
    carry_ref[...] = c[rows - 1:rows, :]
    cum_ref[0] = (c * LOG2E).T[:cum_ref.shape[1], :]


def _in_ssm(x2, gain, w_in, bf, ssm_consts, later_weights, ssm_w, fox_w, n_heads, bsz, seq):
    t, d = x2.shape
    tm = TM_PROJ
    nb = seq // tm
    steps = bsz * nb
    slabs = [w.reshape(steps, w.shape[0] // steps, w.shape[1]) for w in later_weights]
    slab = lambda s: pl.BlockSpec((1,) + s.shape[1:], lambda b, i: (b * nb + i, 0, 0))
    assert tm % SSM_CHUNK == 0 and len(ssm_consts) == 12
    nstate = ssm_consts[4].shape[1]
    dm = ssm_consts[11].shape[1] // 2
    main_w = ssm_w + 3 * fox_w
    gate_w = w_in.shape[1] - main_w - n_heads
    assert gate_w == main_w and main_w % LANES == 0 and n_heads == SUBLANES
    win = lambda rows, idx: pl.BlockSpec((rows, d), lambda *_: (idx, 0),
                                         pipeline_mode=pl.Buffered(1))
    w_t = w_in.T
    row = lambda w: pl.BlockSpec((tm, w), lambda b, i: (b * nb + i, 0))
    h_scr = [pltpu.VMEM((SSM_CHUNK, nstate), F32)] * (2 * (tm // SSM_CHUNK))
    outs = pl.pallas_call(
        functools.partial(_in_ssm_kernel, n_cast=len(slabs)),
        grid=(bsz, nb),
        in_specs=[row(d), _full((1, d)), win(main_w, 0), win(main_w, 1),
                  win(n_heads, 2 * main_w // n_heads), _full((1, LANES))]
                 + [_full(c.shape) for c in ssm_consts] + [slab(s) for s in slabs],
        out_specs=[row(fox_w), row(fox_w), row(fox_w),
                   pl.BlockSpec((1, n_heads, tm), lambda b, i: (b, 0, i)), row(dm),
                   _full((d, gate_w))] + [slab(s) for s in slabs],
        out_shape=[jax.ShapeDtypeStruct((t, fox_w), BF16),
                   jax.ShapeDtypeStruct((t, fox_w), BF16),
                   jax.ShapeDtypeStruct((t, fox_w), BF16),
                   jax.ShapeDtypeStruct((bsz, n_heads, seq), F32),
                   jax.ShapeDtypeStruct((t, dm), F32),
                   jax.ShapeDtypeStruct((d, gate_w), BF16)]
                  + [jax.ShapeDtypeStruct(s.shape, BF16) for s in slabs],
        scratch_shapes=[pltpu.VMEM((1, LANES), F32), pltpu.VMEM((1, nstate), F32),
                        pltpu.VMEM((1, nstate), F32), pltpu.VMEM((d, main_w), BF16),
                        pltpu.VMEM((d, LANES), BF16)] + h_scr,
        compiler_params=_cparams(("arbitrary", "arbitrary")),
        name="in_ssm",
    )(x2, gain.reshape(1, d), w_t, w_t, w_t, bf, *ssm_consts, *slabs)
    cast = [o.reshape(w.shape) for o, w in zip(outs[6:], later_weights)]
    return list(outs[:6]) + cast


def _fox_kernel(q_ref, k_ref, v_ref, ck_ref, o_ref, m_scr, acc_scr, first_blk):
    seq = q_ref.shape[0]
    tq = FOX_TQ
    tk = FOX_TK
    dh = FOX_HEAD_DIM
    n_heads = ck_ref.shape[1]
    lane = lax.broadcasted_iota(jnp.int32, (1, LANES), 1)
    keep = [jnp.where(lane < dh, 1.0, 0.0).astype(BF16), jnp.where(lane >= dh, 1.0, 0.0).astype(BF16)]
    sum_lane = [dh, 0]
    ones = [jnp.where(lane == sl, 1.0, 0.0).astype(BF16) for sl in sum_lane]

    def block(q, j, rows, h, masked):
        e = h % 2
        ps = slice((h // 2) * LANES, (h // 2 + 1) * LANES)
        hs = slice(h * LANES, (h + 1) * LANES)
        ks = pl.ds(pl.multiple_of(j * tk, tk), tk)
        s = _dot_nt(q, k_ref[ks, ps] * keep[e]) - ck_ref[0, h, pl.ds(j, 1), :]
        if masked:
            r = lax.broadcasted_iota(jnp.int32, (tk, tk), 0)
            c = lax.broadcasted_iota(jnp.int32, (tk, tk), 1)
            top = jnp.where(c <= r, s[:tk], -jnp.inf)
            s = top if s.shape[0] == tk else jnp.concatenate([top, s[tk:]], axis=0)
        m_prev = m_scr[rows, hs]
        m_new = jnp.maximum(m_prev, jnp.max(s, axis=1, keepdims=True))
        p = jnp.exp2(s - jnp.concatenate([m_new] * (tk // LANES), axis=1))
        acc_scr[rows, hs] = (jnp.exp2(m_prev - m_new) * acc_scr[rows, hs]
                             + _dot(p.astype(BF16), v_ref[ks, ps] * keep[e] + ones[e]))
        m_scr[rows, hs] = m_new

    nsub = tq // tk
    nq = seq // tq
    nk = seq // tk
    all_rows = slice(0, tq)

    col = lax.broadcasted_iota(jnp.int32, (q_ref.shape[1], LANES), 0) // dh
    head_sel = jnp.where(col == lax.broadcasted_iota(jnp.int32, (q_ref.shape[1], LANES), 1),
                         1.0, 0.0).astype(BF16)

    def max_sq_norm(ref, lo, n):
        x = ref[lo:lo + n, :]
        sq = _dot(x * x, head_sel)
        return jnp.max(sq, axis=0, keepdims=True) * FOX_NORM_SLACK

    k_sq = jnp.concatenate([max_sq_norm(k_ref, j * tk, tk) for j in range(nk)], axis=0)
    q_sq = jnp.concatenate([max_sq_norm(q_ref, i * tq, tq) for i in range(nq)], axis=0)
    blk_id = lax.broadcasted_iota(jnp.int32, (nk, 1), 0).astype(F32)
    for h in range(n_heads):
        k_max = jnp.sqrt(k_sq[:, h:h + 1])
        c_end = ck_ref[0, h, :, tk - 1:tk]
        for i in range(nq):
            first_diag = float(i * nsub)
            q_max = jnp.sqrt(q_sq[i:i + 1, h:h + 1])
            k_own = jnp.max(k_max[i * nsub:(i + 1) * nsub], axis=0, keepdims=True)
            c_start = ck_ref[0, h, i * nsub:i * nsub + 1, 0:1]
            bound = q_max * (k_max + k_own) + (c_start - c_end)
            visit = jnp.logical_and(bound >= -FOX_SKIP_LOG2, blk_id < first_diag)
            first_blk[h * nq + i] = jnp.min(jnp.where(visit, blk_id, first_diag)).astype(jnp.int32)

    def qblock(qi, _):
        qs = pl.ds(pl.multiple_of(qi * tq, tq), tq)
        qp = [q_ref[qs, p * LANES:(p + 1) * LANES] for p in range(n_heads // 2)]
        m_scr[...] = jnp.full(m_scr.shape, -jnp.inf, F32)
        acc_scr[...] = jnp.zeros(acc_scr.shape, F32)

        def full(j, _):
            for h in range(n_heads):
                block(qp[h // 2], j, all_rows, h, False)
            return 0

        starts = [first_blk[h * nq + qi] for h in range(n_heads)]
        common = functools.reduce(jnp.maximum, starts)
        for h in range(n_heads):
            def one(j, _, h=h):
                block(qp[h // 2], j, all_rows, h, False)
                return 0

            def two(t, _, h=h):
                block(qp[h // 2], starts[h] + 2 * t, all_rows, h, False)
                block(qp[h // 2], starts[h] + 2 * t + 1, all_rows, h, False)
                return 0

            pairs = lax.shift_right_logical(common - starts[h], 1)
            lax.fori_loop(0, pairs, two, 0)
            lax.fori_loop(starts[h] + 2 * pairs, common, one, 0)
        lax.fori_loop(common, qi * nsub, full, 0)
        for r in range(nsub):
            rows = slice(r * tk, tq)
            for h in range(n_heads):
                block(qp[h // 2][rows], qi * nsub + r, rows, h, True)
        for p in range(n_heads // 2):
            a0 = acc_scr[:, (2 * p) * LANES:(2 * p + 1) * LANES]
            a1 = acc_scr[:, (2 * p + 1) * LANES:(2 * p + 2) * LANES]
            out = jnp.where(lane < dh, a0 / a0[:, sum_lane[0]:sum_lane[0] + 1],
                            a1 / a1[:, sum_lane[1]:sum_lane[1] + 1])
            o_ref[qs, p * LANES:(p + 1) * LANES] = out.astype(o_ref.dtype)
        return 0

    lax.fori_loop(0, seq // tq, qblock, 0)


def _fox(q, k, v, cum_t, bsz, seq):
    t, w = q.shape
    assert FOX_TQ % FOX_TK == 0 and seq % FOX_TQ == 0 and FOX_HEADS_PER_STEP % 2 == 0
    hps = FOX_HEADS_PER_STEP
    wblk = hps * FOX_HEAD_DIM
    blk = pl.BlockSpec((seq, wblk), lambda bi, hi: (bi, hi))
    return pl.pallas_call(
        _fox_kernel,
        grid=(bsz, w // wblk),
        in_specs=[blk, blk, blk,
                  pl.BlockSpec((1, hps) + cum_t.shape[2:], lambda bi, hi: (bi, hi, 0, 0))],
        out_specs=blk,
        out_shape=jax.ShapeDtypeStruct((t, w), BF16),
        scratch_shapes=[pltpu.VMEM((FOX_TQ, hps * LANES), F32),
                        pltpu.VMEM((FOX_TQ, hps * LANES), F32),
                        pltpu.SMEM((hps * (seq // FOX_TQ),), jnp.int32)],
        compiler_params=_cparams(("arbitrary", "arbitrary")),
        name="fox",
    )(q, k, v, cum_t)


def _merge_kernel(x_ref, oa_ref, att_ref, gmix_ref, wg_ref, wfo_ref, wmix_ref, gq_ref, wq_ref,
                  km_ref, vm_ref, wo_ref, h_ref):
    d = x_ref.shape[-1]
    x = x_ref[...]
    gate = jax.nn.sigmoid(_dot(_rms(x, gmix_ref[...]).astype(BF16), wg_ref[...]))
    out_b = _dot(att_ref[...], wfo_ref[...])
    mix = gate[:, :d] * oa_ref[...] + gate[:, d:] * out_b
    h1 = x + _dot(mix.astype(BF16), wmix_ref[...])

    n = _rms(h1, gq_ref[...]).astype(BF16)
    qm = _dot(n, wq_ref[...])
    hd = qm.shape[-1] // MEM_HEADS
    qm = (qm * (hd ** -0.5)).astype(BF16)
    outs = []
    for hh in range(MEM_HEADS):
        hs = slice(hh * hd, (hh + 1) * hd)
        s = _dot_nt(qm[:, hs], km_ref[0, :, hs])
        s = s - jnp.max(s, axis=-1, keepdims=True)
        p = jnp.exp(s)
        p = p / jnp.sum(p, axis=-1, keepdims=True)
        outs.append(_dot(p.astype(BF16), vm_ref[0, :, hs]))
    o = jnp.concatenate(outs, axis=-1).astype(BF16)
    h_ref[...] = h1 + _dot(o, wo_ref[...])


def _ffn_kernel(h_ref, gf_ref, win_ref, wout_ref, gfin_ref, o_ref):
    h2 = h_ref[...]
    f = _rms(h2, gf_ref[...]).astype(BF16)
    hidden = wout_ref.shape[0]
    acc = jnp.zeros_like(h2)
    for lo, hi in _ffn_chunks(hidden):
        fa = _dot(f, win_ref[:, lo:hi])
        fb = _dot(f, win_ref[:, hidden + lo:hidden + hi])
        g = (fa * jax.nn.sigmoid(fa) * fb).astype(BF16)
        acc = acc + _dot(g, wout_ref[lo:hi, :])
    o_ref[...] = _rms(h2 + acc, gfin_ref[...])


def _ffn_chunks(hidden):
    tiles = hidden // MXU_TILE
    assert tiles * MXU_TILE == hidden
    cuts = [-(-tiles * c // FFN_CHUNKS) * MXU_TILE for c in range(FFN_CHUNKS + 1)]
    return list(zip(cuts[:-1], cuts[1:]))


def _resident(shape):
    n = len(shape)
    return pl.BlockSpec(shape, lambda *_: (0,) * n, pipeline_mode=pl.Buffered(1))


def _post(x2, out_a, att, gain_mix, w_gate, w_fox_o, w_mix, gain_q, w_q, k_m, v_m, w_o,
          gain_f, w_ffn_in, w_ffn_out, gain_fin, bsz, seq):
    t, d = x2.shape
    tm = TM_POST
    nb = seq // tm
    row = lambda w: pl.BlockSpec((tm, w), lambda b, i: (b * nb + i, 0))
    mem = pl.BlockSpec((1,) + k_m.shape[1:], lambda b, i: (b, 0, 0))
    gain = lambda g: g.reshape(1, d)
    weights = lambda *ws: [_resident(w.shape) for w in ws]
    h2 = pl.pallas_call(
        _merge_kernel,
        grid=(bsz, nb),
        in_specs=[row(d), row(d), row(att.shape[1]), _full((1, d))]
                 + weights(w_gate, w_fox_o, w_mix) + [_full((1, d))] + weights(w_q)
                 + [mem, mem] + weights(w_o),
        out_specs=row(d),
        out_shape=jax.ShapeDtypeStruct((t, d), F32),
        compiler_params=_cparams(("arbitrary", "arbitrary")),
        name="merge",
    )(x2, out_a, att, gain(gain_mix), w_gate, w_fox_o, w_mix, gain(gain_q), w_q, k_m, v_m, w_o)
    blk = pl.BlockSpec((tm, d), lambda i: (i, 0))
    return pl.pallas_call(
        _ffn_kernel,
        grid=(t // tm,),
        in_specs=[blk, _full((1, d))] + weights(w_ffn_in, w_ffn_out) + [_full((1, d))],
        out_specs=blk,
        out_shape=jax.ShapeDtypeStruct((t, d), F32),
        compiler_params=_cparams(("arbitrary",)),
        name="ffn",
    )(h2, gain(gain_f), w_ffn_in, w_ffn_out, gain(gain_fin))


def _seg_perm(q):
    seg = q // SUBLANES
    r = jnp.arange(q)
    src = (r % SUBLANES) * seg + r // SUBLANES
    return (src[:, None] == jnp.arange(q)[None, :]).astype(BF16)


def _layer(x2, mem, bsz, seq, norm_mix, w_in, b_forget, lam_re, lam_im, log_dt, b_re, b_im, c_re,
           c_im, d_skip, w_glu, w_fox_o, w_mix_out, norm_mem_q, norm_mem_kv, w_mem_q, w_mem_kv,
           w_mem_o, norm_ffn, w_ffn_in, w_ffn_out, norm_final):
    d = x2.shape[-1]
    ssm_w = lam_re.shape[0] * SSM_GROUP
    n_heads = b_forget.shape[0]
    fox_w = n_heads * FOX_HEAD_DIM

    assert b_re.shape[-1] == SSM_GROUP
    a_re, a_im, s_re, s_im, bd_bre, bd_bim, bd_cre, bd_cim = _ssm_prep(
        lam_re, lam_im, log_dt, b_re, b_im, c_re, c_im)
    perm = _seg_perm(SSM_CHUNK)

    bf = jnp.zeros((1, LANES), F32).at[0, :n_heads].set(b_forget)
    ssm_consts = [perm, perm.T, bd_bre, bd_bim, a_re, a_im, s_re, s_im, bd_cre, bd_cim,
                  d_skip.reshape(1, ssm_w), w_glu.astype(BF16)]
    post_weights = [w_fox_o, w_mix_out, w_mem_q, w_mem_o, w_ffn_in, w_ffn_out]
    q, k, v, cum_t, out_a, w_gate, *post_bf16 = _in_ssm(
        x2, norm_mix, w_in, bf, ssm_consts, post_weights, ssm_w, fox_w, n_heads, bsz, seq)
    wb_fox_o, wb_mix, wb_mem_q, wb_mem_o, wb_ffn_in, wb_ffn_out = post_bf16

    att = _fox(q, k, v, cum_t.reshape(bsz, n_heads, seq // FOX_TK, FOX_TK), bsz, seq)

    k_m, v_m = _mem_kv(mem, norm_mem_kv, w_mem_kv)
    return _post(x2, out_a, att, norm_mix, w_gate, wb_fox_o, wb_mix, norm_mem_q, wb_mem_q, k_m, v_m,
                 wb_mem_o, norm_ffn, wb_ffn_in, wb_ffn_out, norm_final, bsz, seq)


def kernel(x, mem, norm_mix, w_in, b_forget, lam_re, lam_im, log_dt, b_re, b_im, c_re, c_im, d_skip,
           w_glu, w_fox_o, w_mix_out, norm_mem_q, norm_mem_kv, w_mem_q, w_mem_kv, w_mem_o, norm_ffn,
           w_ffn_in, w_ffn_out, norm_final):
    bsz, seq, d = x.shape
    assert w_in.shape[0] == 1, "single-layer block"
    out = _layer(x.reshape(bsz * seq, d), mem, bsz, seq, norm_mix[0], w_in[0], b_forget[0],
                 lam_re[0], lam_im[0], log_dt[0], b_re[0], b_im[0], c_re[0], c_im[0], d_skip[0],
                 w_glu[0], w_fox_o[0], w_mix_out[0], norm_mem_q[0], norm_mem_kv[0], w_mem_q[0],
                 w_mem_kv[0], w_mem_o[0], norm_ffn[0], w_ffn_in[0], w_ffn_out[0], norm_final)
    return out.reshape(bsz, seq, d)
```

```python
import functools
import math

import jax
import jax.numpy as jnp
from jax import lax
from jax.experimental import pallas as pl
from jax.experimental.pallas import tpu as pltpu

F32 = jnp.float32
BF16 = jnp.bfloat16

RMS_EPS = 1e-6
LOG2E = math.log2(math.e)
SSM_GROUP = 16
FOX_HEAD_DIM = 64
MEM_HEADS = 4
LANES = 128
SUBLANES = 8
MXU_TILE = 256
VMEM_LIMIT = 56 * 1024 * 1024

TM_PROJ = 512
TM_POST = 1024
FFN_CHUNKS = 2
SSM_CHUNK = 256
SSM_SEG = SSM_CHUNK // SUBLANES
SSM_COLS = 512
FOX_TQ = 1024
FOX_TK = 512
FOX_HEADS_PER_STEP = 4
FOX_SKIP_LOG2 = 160.0
FOX_NORM_SLACK = 1.02


def _cparams(sem):
    return pltpu.CompilerParams(dimension_semantics=sem, vmem_limit_bytes=VMEM_LIMIT)


def _rms(x, gain):
    return x * lax.rsqrt(jnp.mean(x * x, axis=-1, keepdims=True) + RMS_EPS) * gain


def _dot(a, b):
    return jnp.dot(a, b, preferred_element_type=F32)


def _dot_nt(a, b):
    return lax.dot_general(a, b, (((1,), (1,)), ((), ())), preferred_element_type=F32)


def _full(shape):
    n = len(shape)
    return pl.BlockSpec(shape, lambda *_: (0,) * n)


def _ssm_prep_kernel(lre_ref, lim_ref, ldt_ref, bre_ref, bim_ref, cre_ref, cim_ref,
                     are_ref, aim_ref, sre_ref, sim_ref, bbre_ref, bbim_ref, ccre_ref, ccim_ref):
    lre = lre_ref[...]
    lim = lim_ref[...]
    dt = jnp.exp(ldt_ref[...])
    zr = lre * dt
    zi = lim * dt
    mag = jnp.exp(zr)
    are = mag * jnp.cos(zi)
    aim = mag * jnp.sin(zi)
    are_ref[...] = are
    aim_ref[...] = aim
    mag_s = jnp.exp(zr * SSM_SEG)
    sre_ref[...] = mag_s * jnp.cos(zi * SSM_SEG)
    sim_ref[...] = mag_s * jnp.sin(zi * SSM_SEG)
    nr = are - 1.0
    ni = aim
    den = lre * lre + lim * lim
    fr = (nr * lre + ni * lim) / den
    fi = (ni * lre - nr * lim) / den
    bre = bre_ref[...]
    bim = bim_ref[...]
    nblk, rows, cols = bbre_ref.shape
    n = bre.shape[0]
    per = rows // n
    same_group = (lax.broadcasted_iota(jnp.int32, (rows, cols), 0) // n
                  == lax.broadcasted_iota(jnp.int32, (rows, cols), 1) // (cols // per))

    def block_diag(x, kb):
        slab = x[:, kb * cols:(kb + 1) * cols]
        return jnp.where(same_group, jnp.concatenate([slab] * per, axis=0), 0.0)

    b_bar = (fr * bre - fi * bim, fr * bim + fi * bre)
    for kb in range(nblk):
        bbre_ref[kb] = block_diag(b_bar[0], kb).astype(BF16)
        bbim_ref[kb] = block_diag(b_bar[1], kb).astype(BF16)
        ccre_ref[kb] = block_diag(cre_ref[...], kb).T.astype(BF16)
        ccim_ref[kb] = block_diag(cim_ref[...], kb).T.astype(BF16)


def _ssm_prep(lam_re, lam_im, log_dt, b_re, b_im, c_re, c_im):
    g, p = lam_re.shape
    n = b_re.shape[-1]
    c = g * p
    per = LANES // n
    nblk = g // per
    row = lambda a: a.reshape(1, c)
    ldt = jnp.broadcast_to(log_dt[:, None], (g, p))
    blk = lambda r, cc: jax.ShapeDtypeStruct((nblk, r, cc), BF16)
    return pl.pallas_call(
        _ssm_prep_kernel,
        out_shape=[jax.ShapeDtypeStruct((1, c), F32)] * 4
                  + [blk(LANES, per * p)] * 2 + [blk(per * p, LANES)] * 2,
        name="ssm_prep",
    )(row(lam_re), row(lam_im), row(ldt),
      b_re.transpose(2, 0, 1).reshape(n, c), b_im.transpose(2, 0, 1).reshape(n, c),
      c_re.transpose(1, 0, 2).reshape(n, c), c_im.transpose(1, 0, 2).reshape(n, c))


def _mem_kv_kernel(m_ref, g_ref, w_ref, k_ref, v_ref):
    n = _rms(m_ref[0], g_ref[...]).astype(BF16)
    kv = _dot(n, w_ref[...].astype(BF16))
    half = kv.shape[-1] // 2
    k_ref[0] = kv[:, :half].astype(BF16)
    v_ref[0] = kv[:, half:].astype(BF16)


def _mem_kv(mem, gain, w_kv):
    b, m, d = mem.shape
    w2 = w_kv.shape[-1]
    return pl.pallas_call(
        _mem_kv_kernel,
        grid=(b,),
        in_specs=[pl.BlockSpec((1, m, d), lambda i: (i, 0, 0)), _full((1, d)), _full((d, w2))],
        out_specs=[pl.BlockSpec((1, m, w2 // 2), lambda i: (i, 0, 0))] * 2,
        out_shape=[jax.ShapeDtypeStruct((b, m, w2 // 2), BF16)] * 2,
        compiler_params=_cparams(("arbitrary",)),
        name="mem_kv",
    )(mem, gain.reshape(1, d), w_kv)


def _gelu_tanh(x):
    return 0.5 * x * (1.0 + jnp.tanh(math.sqrt(2.0 / math.pi) * (x + 0.044715 * (x * x * x))))


def _ssm_drive(u, perm_ref, bre_ref, bim_ref, h_scr):
    n_chunks = u.shape[0] // SSM_CHUNK
    ub = u.astype(BF16)
    up = jnp.concatenate(
        [_dot(perm_ref[...], ub[c * SSM_CHUNK:(c + 1) * SSM_CHUNK]).astype(BF16)
         for c in range(n_chunks)], axis=0)
    kw = bre_ref.shape[1]
    nw = bre_ref.shape[2]
    for kb in range(bre_ref.shape[0]):
        ukb = up[:, kb * kw:(kb + 1) * kw]
        for part, w_ref in enumerate((bre_ref, bim_ref)):
            bu = _dot(ukb, w_ref[kb])
            for c in range(n_chunks):
                h_scr[2 * c + part][:, kb * nw:(kb + 1) * nw] = bu[c * SSM_CHUNK:(c + 1) * SSM_CHUNK]


def _ssm_chunk(u, permt_ref, are_ref, aim_ref, sre_ref, sim_ref, cre_ref, cim_ref, d_ref,
               hre, him, car_re, car_im, anchors=()):
    if callable(anchors):
        anchors = anchors()

    nstate = hre.shape[1]
    for cb in range(nstate // SSM_COLS):
        cs = slice(cb * SSM_COLS, (cb + 1) * SSM_COLS)
        a_re = jnp.broadcast_to(are_ref[:, cs], (SUBLANES, SSM_COLS))
        a_im = jnp.broadcast_to(aim_ref[:, cs], (SUBLANES, SSM_COLS))

        e_re = hre[0:SUBLANES, cs]
        e_im = him[0:SUBLANES, cs]
        for k in range(1, SSM_SEG):
            r = slice(k * SUBLANES, (k + 1) * SUBLANES)
            e_re, e_im = (a_re * e_re - a_im * e_im + hre[r, cs],
                          a_re * e_im + a_im * e_re + him[r, cs])
            hre[r, cs] = e_re
            him[r, cs] = e_im

        s_re = sre_ref[:, cs]
        s_im = sim_ref[:, cs]
        c_re = car_re[:, cs]
        c_im = car_im[:, cs]
        rows_re, rows_im = [], []
        for j in range(SUBLANES):
            rows_re.append(c_re)
            rows_im.append(c_im)
            n_re = s_re * c_re - s_im * c_im + e_re[j:j + 1]
            n_im = s_re * c_im + s_im * c_re + e_im[j:j + 1]
            c_re, c_im = n_re, n_im
        car_re[:, cs] = c_re
        car_im[:, cs] = c_im

        d_re = jnp.concatenate(rows_re, axis=0)
        d_im = jnp.concatenate(rows_im, axis=0)
        for k in range(SSM_SEG):
            r = slice(k * SUBLANES, (k + 1) * SUBLANES)
            d_re, d_im = a_re * d_re - a_im * d_im, a_re * d_im + a_im * d_re
            if k == SSM_SEG // 2:
                for z in anchors[cb::nstate // SSM_COLS]:
                    d_re = d_re + jnp.concatenate([z] * (SSM_COLS // LANES), axis=1)
            hre[r, cs] = hre[r, cs] + d_re
            him[r, cs] = him[r, cs] + d_im

    ncb = cre_ref.shape[0]
    cw = cre_ref.shape[1]
    ys = []
    for kb in range(ncb):
        h_re = hre[:, kb * cw:(kb + 1) * cw].astype(BF16)
        h_im = him[:, kb * cw:(kb + 1) * cw].astype(BF16)
        ys.append(_dot(h_re, cre_ref[kb]) - _dot(h_im, cim_ref[kb]))
    yp = jnp.concatenate(ys, axis=-1)
    hi = yp.astype(BF16)
    lo = (yp - hi.astype(F32)).astype(BF16)
    y = _dot(permt_ref[...], hi) + _dot(permt_ref[...], lo)
    y = y + d_ref[...] * u
    return _gelu_tanh(y).astype(BF16)


def _glu(g, wglu_ref):
    z = _dot(g, wglu_ref[...])
    half = z.shape[-1] // 2
    return z[:, :half] * jax.nn.sigmoid(z[:, half:])


def _schedule_anchor(x):
    rows, cols = x.shape
    s = x[:, :LANES]
    for c in range(1, cols // LANES):
        s = s + x[:, c * LANES:(c + 1) * LANES]
    s = jnp.sum(s.reshape(rows // SUBLANES, SUBLANES, LANES), axis=0)
    bits = pltpu.bitcast(s, jnp.uint32)
    return ((bits >> 16) >> 16).astype(F32)


def _in_ssm_kernel(x_ref, g_ref, wa_ref, wb_ref, wc_ref, bf_ref, *rest, n_cast):
    nconst = 12
    (perm_ref, permt_ref, bre_ref, bim_ref, are_ref, aim_ref, sre_ref, sim_ref,
     cre_ref, cim_ref, d_ref, wglu_ref) = rest[:nconst]
    cast_in = rest[nconst:nconst + n_cast]
    outs = rest[nconst + n_cast:]
    q_ref, k_ref, v_ref, cum_ref, oa_ref, wg_ref = outs[:6]
    cast_out = outs[6:6 + n_cast]
    carry_ref, car_re, car_im, w_bf, wf_bf = outs[6 + n_cast:11 + n_cast]
    h_scr = outs[11 + n_cast:]
    n_forget = cum_ref.shape[1]
    for src, dst in zip(cast_in, cast_out):
        dst[...] = src[...].astype(BF16)

    @pl.when(pl.program_id(1) == 0)
    def _():
        carry_ref[...] = jnp.zeros_like(carry_ref)
        car_re[...] = jnp.zeros_like(car_re)
        car_im[...] = jnp.zeros_like(car_im)
        w_bf[...] = wa_ref[...].T.astype(BF16)
        wf_bf[...] = wb_ref[:LANES, :].T.astype(BF16)
        wg_ref[...] = jnp.concatenate([wb_ref[n_forget:, :], wc_ref[...]], axis=0).T.astype(BF16)

    un = _rms(x_ref[...], g_ref[...]).astype(BF16)
    wu = d_ref.shape[1]
    wh = q_ref.shape[1]
    col = lambda lo, hi: _dot(un, w_bf[:, lo:hi])
    u = col(0, wu)
    pieces = [(ref, lo + p * MXU_TILE, p * MXU_TILE, scale)
              for ref, lo, scale in ((q_ref, wu, FOX_HEAD_DIM ** -0.5 * LOG2E),
                                     (k_ref, wu + wh, None), (v_ref, wu + 2 * wh, None))
              for p in range(wh // MXU_TILE)]
    n_chunks = u.shape[0] // SSM_CHUNK
    per_chunk = -(-len(pieces) // n_chunks)

    def project(todo):
        anchors = []
        for ref, src, dst, scale in todo:
            piece = col(src, src + MXU_TILE)
            if scale is not None:
                piece = piece * scale
            ref[:, dst:dst + MXU_TILE] = piece.astype(BF16)
            anchors.append(_schedule_anchor(piece))
        return anchors

    _ssm_drive(u, perm_ref, bre_ref, bim_ref, h_scr)
    acts = []
    for c in range(n_chunks):
        rows = slice(c * SSM_CHUNK, (c + 1) * SSM_CHUNK)
        todo = pieces[c * per_chunk:(c + 1) * per_chunk]
        acts.append(_ssm_chunk(u[rows], permt_ref, are_ref, aim_ref, sre_ref, sim_ref, cre_ref,
                               cim_ref, d_ref, h_scr[2 * c], h_scr[2 * c + 1], car_re, car_im,
                               functools.partial(project, todo)))
    oa_ref[...] = _glu(jnp.concatenate(acts, axis=0), wglu_ref)
    c = jax.nn.log_sigmoid(_dot(un, wf_bf[...]) + bf_ref[...])
    rows = c.shape[0]
    row = lax.broadcasted_iota(jnp.int32, c.shape, 0)
    shift = 1
    while shift < rows:
        c = c + jnp.where(row >= shift, pltpu.roll(c, shift, 0), 0.0)
        shift *= 2
    c = c + carry_ref[...]
    carry_ref[...] = c[rows - 1:rows, :]
    cum_ref[0] = (c * LOG2E).T[:cum_ref.shape[1], :]


def _in_ssm(x2, gain, w_in, bf, ssm_consts, later_weights, ssm_w, fox_w, n_heads, bsz, seq):
    t, d = x2.shape
    tm = TM_PROJ
    nb = seq // tm
    steps = bsz * nb
    slabs = [w.reshape(steps, w.shape[0] // steps, w.shape[1]) for w in later_weights]
    slab = lambda s: pl.BlockSpec((1,) + s.shape[1:], lambda b, i: (b * nb + i, 0, 0))
    assert tm % SSM_CHUNK == 0 and len(ssm_consts) == 12
    nstate = ssm_consts[4].shape[1]
    dm = ssm_consts[11].shape[1] // 2
    main_w = ssm_w + 3 * fox_w
    gate_w = w_in.shape[1] - main_w - n_heads
    assert gate_w == main_w and main_w % LANES == 0 and n_heads == SUBLANES
    win = lambda rows, idx: pl.BlockSpec((rows, d), lambda *_: (idx, 0),
                                         pipeline_mode=pl.Buffered(1))
    w_t = w_in.T
    row = lambda w: pl.BlockSpec((tm, w), lambda b, i: (b * nb + i, 0))
    h_scr = [pltpu.VMEM((SSM_CHUNK, nstate), F32)] * (2 * (tm // SSM_CHUNK))
    outs = pl.pallas_call(
        functools.partial(_in_ssm_kernel, n_cast=len(slabs)),
        grid=(bsz, nb),
        in_specs=[row(d), _full((1, d)), win(main_w, 0), win(main_w, 1),
                  win(n_heads, 2 * main_w // n_heads), _full((1, LANES))]
                 + [_full(c.shape) for c in ssm_consts] + [slab(s) for s in slabs],
        out_specs=[row(fox_w), row(fox_w), row(fox_w),
                   pl.BlockSpec((1, n_heads, tm), lambda b, i: (b, 0, i)), row(dm),
                   _full((d, gate_w))] + [slab(s) for s in slabs],
        out_shape=[jax.ShapeDtypeStruct((t, fox_w), BF16),
                   jax.ShapeDtypeStruct((t, fox_w), BF16),
                   jax.ShapeDtypeStruct((t, fox_w), BF16),
                   jax.ShapeDtypeStruct((bsz, n_heads, seq), F32),
                   jax.ShapeDtypeStruct((t, dm), F32),
                   jax.ShapeDtypeStruct((d, gate_w), BF16)]
                  + [jax.ShapeDtypeStruct(s.shape, BF16) for s in slabs],
        scratch_shapes=[pltpu.VMEM((1, LANES), F32), pltpu.VMEM((1, nstate), F32),
                        pltpu.VMEM((1, nstate), F32), pltpu.VMEM((d, main_w), BF16),
                        pltpu.VMEM((d, LANES), BF16)] + h_scr,
        compiler_params=_cparams(("arbitrary", "arbitrary")),
        name="in_ssm",
    )(x2, gain.reshape(1, d), w_t, w_t, w_t, bf, *ssm_consts, *slabs)
    cast = [o.reshape(w.shape) for o, w in zip(outs[6:], later_weights)]
    return list(outs[:6]) + cast


def _fox_kernel(q_ref, k_ref, v_ref, ck_ref, o_ref, m_scr, acc_scr, first_blk):
    seq = q_ref.shape[0]
    tq = FOX_TQ
    tk = FOX_TK
    dh = FOX_HEAD_DIM
    n_heads = ck_ref.shape[1]
    lane = lax.broadcasted_iota(jnp.int32, (1, LANES), 1)
    keep = [jnp.where(lane < dh, 1.0, 0.0).astype(BF16), jnp.where(lane >= dh, 1.0, 0.0).astype(BF16)]

    def block(q, j, rows, h, masked, first=False):
        e = h % 2
        ps = slice((h // 2) * LANES, (h // 2 + 1) * LANES)
        hs = slice(h * LANES, (h + 1) * LANES)
        ks = pl.ds(pl.multiple_of(j * tk, tk), tk)
        s = _dot_nt(q, k_ref[ks, ps] * keep[e]) - ck_ref[0, h, pl.ds(j, 1), :]
        if masked:
            r = lax.broadcasted_iota(jnp.int32, (tk, tk), 0)
            c = lax.broadcasted_iota(jnp.int32, (tk, tk), 1)
            top = jnp.where(c <= r, s[:tk], -jnp.inf)
            s = top if s.shape[0] == tk else jnp.concatenate([top, s[tk:]], axis=0)
        m_new = jnp.broadcast_to(jnp.max(s, axis=1, keepdims=True), (s.shape[0], LANES))
        if not first:
            m_prev = m_scr[rows, hs]
            m_new = jnp.maximum(m_prev, m_new)
        p = jnp.exp2(s - jnp.concatenate([m_new] * (tk // LANES), axis=1))
        pv = _dot(p.astype(BF16), v_ref[ks, ps] * keep[e] + keep[1 - e])
        acc_scr[rows, hs] = pv if first else jnp.exp2(m_prev - m_new) * acc_scr[rows, hs] + pv
        m_scr[rows, hs] = m_new

    nsub = tq // tk
    nq = seq // tq
    nk = seq // tk
    all_rows = slice(0, tq)

    col = lax.broadcasted_iota(jnp.int32, (q_ref.shape[1], LANES), 0) // dh
    head_sel = jnp.where(col == lax.broadcasted_iota(jnp.int32, (q_ref.shape[1], LANES), 1),
                         1.0, 0.0).astype(BF16)

    def max_sq_norm(ref, lo, n):
        x = ref[lo:lo + n, :]
        sq = _dot(x * x, head_sel)
        return jnp.max(sq, axis=0, keepdims=True) * FOX_NORM_SLACK

    k_sq = jnp.concatenate([max_sq_norm(k_ref, j * tk, tk) for j in range(nk)], axis=0)
    q_sq = jnp.concatenate([max_sq_norm(q_ref, i * tq, tq) for i in range(nq)], axis=0)
    blk_id = lax.broadcasted_iota(jnp.int32, (nk, 1), 0).astype(F32)
    for h in range(n_heads):
        k_max = jnp.sqrt(k_sq[:, h:h + 1])
        c_end = ck_ref[0, h, :, tk - 1:tk]
        for i in range(nq):
            first_diag = float(i * nsub)
            q_max = jnp.sqrt(q_sq[i:i + 1, h:h + 1])
            k_own = jnp.max(k_max[i * nsub:(i + 1) * nsub], axis=0, keepdims=True)
            c_start = ck_ref[0, h, i * nsub:i * nsub + 1, 0:1]
            bound = q_max * (k_max + k_own) + (c_start - c_end)
            visit = jnp.logical_and(bound >= -FOX_SKIP_LOG2, blk_id < first_diag)
            first_blk[h * nq + i] = jnp.min(jnp.where(visit, blk_id, first_diag)).astype(jnp.int32)

    def qblock(qi, _):
        qs = pl.ds(pl.multiple_of(qi * tq, tq), tq)
        qp = [q_ref[qs, p * LANES:(p + 1) * LANES] for p in range(n_heads // 2)]
        for r in range(nsub):
            rows = slice(r * tk, tq)
            for h in range(n_heads):
                block(qp[h // 2][rows], qi * nsub + r, rows, h, True, first=r == 0)

        def full(j, _):
            for h in range(n_heads):
                block(qp[h // 2], j, all_rows, h, False)
            return 0

        starts = [first_blk[h * nq + qi] for h in range(n_heads)]
        common = functools.reduce(jnp.maximum, starts)
        for h in range(n_heads):
            def one(j, _, h=h):
                block(qp[h // 2], j, all_rows, h, False)
                return 0

            def two(t, _, h=h):
                block(qp[h // 2], starts[h] + 2 * t, all_rows, h, False)
                block(qp[h // 2], starts[h] + 2 * t + 1, all_rows, h, False)
                return 0

            pairs = lax.shift_right_logical(common - starts[h], 1)
            lax.fori_loop(0, pairs, two, 0)
            lax.fori_loop(starts[h] + 2 * pairs, common, one, 0)
        lax.fori_loop(common, qi * nsub, full, 0)
        for p in range(n_heads // 2):
            a0 = acc_scr[:, (2 * p) * LANES:(2 * p + 1) * LANES]
            a1 = acc_scr[:, (2 * p + 1) * LANES:(2 * p + 2) * LANES]
            sums = pltpu.roll(jnp.where(lane < dh, a1, a0), dh, axis=1)
            out = jnp.where(lane < dh, a0, a1) / sums
            o_ref[qs, p * LANES:(p + 1) * LANES] = out.astype(o_ref.dtype)
        return 0

    lax.fori_loop(0, seq // tq, qblock, 0)


def _fox(q, k, v, cum_t, bsz, seq):
    t, w = q.shape
    assert FOX_TQ % FOX_TK == 0 and seq % FOX_TQ == 0 and FOX_HEADS_PER_STEP % 2 == 0
    hps = FOX_HEADS_PER_STEP
    wblk = hps * FOX_HEAD_DIM
    blk = pl.BlockSpec((seq, wblk), lambda bi, hi: (bi, hi))
    return pl.pallas_call(
        _fox_kernel,
        grid=(bsz, w // wblk),
        in_specs=[blk, blk, blk,
                  pl.BlockSpec((1, hps) + cum_t.shape[2:], lambda bi, hi: (bi, hi, 0, 0))],
        out_specs=blk,
        out_shape=jax.ShapeDtypeStruct((t, w), BF16),
        scratch_shapes=[pltpu.VMEM((FOX_TQ, hps * LANES), F32),
                        pltpu.VMEM((FOX_TQ, hps * LANES), F32),
                        pltpu.SMEM((hps * (seq // FOX_TQ),), jnp.int32)],
        compiler_params=_cparams(("arbitrary", "arbitrary")),
        name="fox",
    )(q, k, v, cum_t)


def _merge_kernel(x_ref, oa_ref, att_ref, gmix_ref, wg_ref, wfo_ref, wmix_ref, gq_ref, wq_ref,
                  km_ref, vm_ref, wo_ref, h_ref):
    d = x_ref.shape[-1]
    x = x_ref[...]
    gate = jax.nn.sigmoid(_dot(_rms(x, gmix_ref[...]).astype(BF16), wg_ref[...]))
    out_b = _dot(att_ref[...], wfo_ref[...])
    mix = gate[:, :d] * oa_ref[...] + gate[:, d:] * out_b
    h1 = x + _dot(mix.astype(BF16), wmix_ref[...])

    n = _rms(h1, gq_ref[...]).astype(BF16)
    qm = _dot(n, wq_ref[...])
    hd = qm.shape[-1] // MEM_HEADS
    qm = (qm * (hd ** -0.5)).astype(BF16)
    outs = []
    for hh in range(MEM_HEADS):
        hs = slice(hh * hd, (hh + 1) * hd)
        s = _dot_nt(qm[:, hs], km_ref[0, :, hs])
        s = s - jnp.max(s, axis=-1, keepdims=True)
        p = jnp.exp(s)
        p = p / jnp.sum(p, axis=-1, keepdims=True)
        outs.append(_dot(p.astype(BF16), vm_ref[0, :, hs]))
    o = jnp.concatenate(outs, axis=-1).astype(BF16)
    h_ref[...] = h1 + _dot(o, wo_ref[...])


def _ffn_kernel(h_ref, gf_ref, win_ref, wout_ref, gfin_ref, o_ref):
    h2 = h_ref[...]
    f = _rms(h2, gf_ref[...]).astype(BF16)
    hidden = wout_ref.shape[0]
    acc = jnp.zeros_like(h2)
    for lo, hi in _ffn_chunks(hidden):
        fa = _dot(f, win_ref[:, lo:hi])
        fb = _dot(f, win_ref[:, hidden + lo:hidden + hi])
        g = (fa * jax.nn.sigmoid(fa) * fb).astype(BF16)
        acc = acc + _dot(g, wout_ref[lo:hi, :])
    o_ref[...] = _rms(h2 + acc, gfin_ref[...])


def _ffn_chunks(hidden):
    tiles = hidden // MXU_TILE
    assert tiles * MXU_TILE == hidden
    cuts = [-(-tiles * c // FFN_CHUNKS) * MXU_TILE for c in range(FFN_CHUNKS + 1)]
    return list(zip(cuts[:-1], cuts[1:]))


def _resident(shape):
    n = len(shape)
    return pl.BlockSpec(shape, lambda *_: (0,) * n, pipeline_mode=pl.Buffered(1))


def _post(x2, out_a, att, gain_mix, w_gate, w_fox_o, w_mix, gain_q, w_q, k_m, v_m, w_o,
          gain_f, w_ffn_in, w_ffn_out, gain_fin, bsz, seq):
    t, d = x2.shape
    tm = TM_POST
    nb = seq // tm
    row = lambda w: pl.BlockSpec((tm, w), lambda b, i: (b * nb + i, 0))
    mem = pl.BlockSpec((1,) + k_m.shape[1:], lambda b, i: (b, 0, 0))
    gain = lambda g: g.reshape(1, d)
    weights = lambda *ws: [_resident(w.shape) for w in ws]
    h2 = pl.pallas_call(
        _merge_kernel,
        grid=(bsz, nb),
        in_specs=[row(d), row(d), row(att.shape[1]), _full((1, d))]
                 + weights(w_gate, w_fox_o, w_mix) + [_full((1, d))] + weights(w_q)
                 + [mem, mem] + weights(w_o),
        out_specs=row(d),
        out_shape=jax.ShapeDtypeStruct((t, d), F32),
        compiler_params=_cparams(("arbitrary", "arbitrary")),
        name="merge",
    )(x2, out_a, att, gain(gain_mix), w_gate, w_fox_o, w_mix, gain(gain_q), w_q, k_m, v_m, w_o)
    blk = pl.BlockSpec((tm, d), lambda i: (i, 0))
    return pl.pallas_call(
        _ffn_kernel,
        grid=(t // tm,),
        in_specs=[blk, _full((1, d))] + weights(w_ffn_in, w_ffn_out) + [_full((1, d))],
        out_specs=blk,
        out_shape=jax.ShapeDtypeStruct((t, d), F32),
        compiler_params=_cparams(("arbitrary",)),
        name="ffn",
    )(h2, gain(gain_f), w_ffn_in, w_ffn_out, gain(gain_fin))


def _seg_perm(q):
    seg = q // SUBLANES
    r = jnp.arange(q)
    src = (r % SUBLANES) * seg + r // SUBLANES
    return (src[:, None] == jnp.arange(q)[None, :]).astype(BF16)


def _layer(x2, mem, bsz, seq, norm_mix, w_in, b_forget, lam_re, lam_im, log_dt, b_re, b_im, c_re,
           c_im, d_skip, w_glu, w_fox_o, w_mix_out, norm_mem_q, norm_mem_kv, w_mem_q, w_mem_kv,
           w_mem_o, norm_ffn, w_ffn_in, w_ffn_out, norm_final):
    d = x2.shape[-1]
    ssm_w = lam_re.shape[0] * SSM_GROUP
    n_heads = b_forget.shape[0]
    fox_w = n_heads * FOX_HEAD_DIM

    assert b_re.shape[-1] == SSM_GROUP
    a_re, a_im, s_re, s_im, bd_bre, bd_bim, bd_cre, bd_cim = _ssm_prep(
        lam_re, lam_im, log_dt, b_re, b_im, c_re, c_im)
    perm = _seg_perm(SSM_CHUNK)

    bf = jnp.zeros((1, LANES), F32).at[0, :n_heads].set(b_forget)
    ssm_consts = [perm, perm.T, bd_bre, bd_bim, a_re, a_im, s_re, s_im, bd_cre, bd_cim,
                  d_skip.reshape(1, ssm_w), w_glu.astype(BF16)]
    post_weights = [w_fox_o, w_mix_out, w_mem_q, w_mem_o, w_ffn_in, w_ffn_out]
    q, k, v, cum_t, out_a, w_gate, *post_bf16 = _in_ssm(
        x2, norm_mix, w_in, bf, ssm_consts, post_weights, ssm_w, fox_w, n_heads, bsz, seq)
    wb_fox_o, wb_mix, wb_mem_q, wb_mem_o, wb_ffn_in, wb_ffn_out = post_bf16

    att = _fox(q, k, v, cum_t.reshape(bsz, n_heads, seq // FOX_TK, FOX_TK), bsz, seq)

    k_m, v_m = _mem_kv(mem, norm_mem_kv, w_mem_kv)
    return _post(x2, out_a, att, norm_mix, w_gate, wb_fox_o, wb_mix, norm_mem_q, wb_mem_q, k_m, v_m,
                 wb_mem_o, norm_ffn, wb_ffn_in, wb_ffn_out, norm_final, bsz, seq)


def kernel(x, mem, norm_mix, w_in, b_forget, lam_re, lam_im, log_dt, b_re, b_im, c_re, c_im, d_skip,
           w_glu, w_fox_o, w_mix_out, norm_mem_q, norm_mem_kv, w_mem_q, w_mem_kv, w_mem_o, norm_ffn,
           w_ffn_in, w_ffn_out, norm_final):
    bsz, seq, d = x.shape
    assert w_in.shape[0] == 1, "single-layer block"
    out = _layer(x.reshape(bsz * seq, d), mem, bsz, seq, norm_mix[0], w_in[0], b_forget[0],
                 lam_re[0], lam_im[0], log_dt[0], b_re[0], b_im[0], c_re[0], c_im[0], d_skip[0],
                 w_glu[0], w_fox_o[0], w_mix_out[0], norm_mem_q[0], norm_mem_kv[0], w_mem_q[0],
                 w_mem_kv[0], w_mem_o[0], norm_ffn[0], w_ffn_in[0], w_ffn_out[0], norm_final)
    return out.reshape(bsz, seq, d)
```

```python
import functools
import math

import jax
import jax.numpy as jnp
from jax import lax
from jax.experimental import pallas as pl
from jax.experimental.pallas import tpu as pltpu

F32 = jnp.float32
BF16 = jnp.bfloat16

RMS_EPS = 1e-6
LOG2E = math.log2(math.e)
SSM_GROUP = 16
FOX_HEAD_DIM = 64
MEM_HEADS = 4
LANES = 128
SUBLANES = 8
MXU_TILE = 256
VMEM_LIMIT = 56 * 1024 * 1024

TM_PROJ = 512
TM_POST = 1024
FFN_CHUNKS = 2
SSM_CHUNK = 256
SSM_SEG = SSM_CHUNK // SUBLANES
SSM_COLS = 512
FOX_TQ = 1024
FOX_TK = 512
FOX_HEADS_PER_STEP = 4
FOX_SKIP_LOG2 = 160.0
FOX_NORM_SLACK = 1.02


def _cparams(sem):
    return pltpu.CompilerParams(dimension_semantics=sem, vmem_limit_bytes=VMEM_LIMIT)


def _rms(x, gain):
    return x * lax.rsqrt(jnp.mean(x * x, axis=-1, keepdims=True) + RMS_EPS) * gain


def _dot(a, b):
    return jnp.dot(a, b, preferred_element_type=F32)


def _dot_nt(a, b):
    return lax.dot_general(a, b, (((1,), (1,)), ((), ())), preferred_element_type=F32)


def _full(shape):
    n = len(shape)
    return pl.BlockSpec(shape, lambda *_: (0,) * n)


def _ssm_prep_kernel(lre_ref, lim_ref, ldt_ref, bre_ref, bim_ref, cre_ref, cim_ref,
                     are_ref, aim_ref, sre_ref, sim_ref, bbre_ref, bbim_ref, ccre_ref, ccim_ref):
    lre = lre_ref[...]
    lim = lim_ref[...]
    dt = jnp.exp(ldt_ref[...])
    zr = lre * dt
    zi = lim * dt
    mag = jnp.exp(zr)
    are = mag * jnp.cos(zi)
    aim = mag * jnp.sin(zi)
    are_ref[...] = are
    aim_ref[...] = aim
    mag_s = jnp.exp(zr * SSM_SEG)
    sre_ref[...] = mag_s * jnp.cos(zi * SSM_SEG)
    sim_ref[...] = mag_s * jnp.sin(zi * SSM_SEG)
    nr = are - 1.0
    ni = aim
    den = lre * lre + lim * lim
    fr = (nr * lre + ni * lim) / den
    fi = (ni * lre - nr * lim) / den
    bre = bre_ref[...]
    bim = bim_ref[...]
    nblk, rows, cols = bbre_ref.shape
    n = bre.shape[0]
    per = rows // n
    same_group = (lax.broadcasted_iota(jnp.int32, (rows, cols), 0) // n
                  == lax.broadcasted_iota(jnp.int32, (rows, cols), 1) // (cols // per))

    def block_diag(x, kb):
        slab = x[:, kb * cols:(kb + 1) * cols]
        return jnp.where(same_group, jnp.concatenate([slab] * per, axis=0), 0.0)

    b_bar = (fr * bre - fi * bim, fr * bim + fi * bre)
    for kb in range(nblk):
        bbre_ref[kb] = block_diag(b_bar[0], kb).astype(BF16)
        bbim_ref[kb] = block_diag(b_bar[1], kb).astype(BF16)
        ccre_ref[kb] = block_diag(cre_ref[...], kb).T.astype(BF16)
        ccim_ref[kb] = block_diag(cim_ref[...], kb).T.astype(BF16)


def _ssm_prep(lam_re, lam_im, log_dt, b_re, b_im, c_re, c_im):
    g, p = lam_re.shape
    n = b_re.shape[-1]
    c = g * p
    per = LANES // n
    nblk = g // per
    row = lambda a: a.reshape(1, c)
    ldt = jnp.broadcast_to(log_dt[:, None], (g, p))
    blk = lambda r, cc: jax.ShapeDtypeStruct((nblk, r, cc), BF16)
    return pl.pallas_call(
        _ssm_prep_kernel,
        out_shape=[jax.ShapeDtypeStruct((1, c), F32)] * 4
                  + [blk(LANES, per * p)] * 2 + [blk(per * p, LANES)] * 2,
        name="ssm_prep",
    )(row(lam_re), row(lam_im), row(ldt),
      b_re.transpose(2, 0, 1).reshape(n, c), b_im.transpose(2, 0, 1).reshape(n, c),
      c_re.transpose(1, 0, 2).reshape(n, c), c_im.transpose(1, 0, 2).reshape(n, c))


def _mem_kv_kernel(m_ref, g_ref, w_ref, k_ref, v_ref):
    n = _rms(m_ref[0], g_ref[...]).astype(BF16)
    kv = _dot(n, w_ref[...].astype(BF16))
    half = kv.shape[-1] // 2
    k_ref[0] = kv[:, :half].astype(BF16)
    v_ref[0] = kv[:, half:].astype(BF16)


def _mem_kv(mem, gain, w_kv):
    b, m, d = mem.shape
    w2 = w_kv.shape[-1]
    return pl.pallas_call(
        _mem_kv_kernel,
        grid=(b,),
        in_specs=[pl.BlockSpec((1, m, d), lambda i: (i, 0, 0)), _full((1, d)), _full((d, w2))],
        out_specs=[pl.BlockSpec((1, m, w2 // 2), lambda i: (i, 0, 0))] * 2,
        out_shape=[jax.ShapeDtypeStruct((b, m, w2 // 2), BF16)] * 2,
        compiler_params=_cparams(("arbitrary",)),
        name="mem_kv",
    )(mem, gain.reshape(1, d), w_kv)


def _gelu_tanh(x):
    return 0.5 * x * (1.0 + jnp.tanh(math.sqrt(2.0 / math.pi) * (x + 0.044715 * (x * x * x))))


def _ssm_drive(u, perm_ref, bre_ref, bim_ref, h_scr):
    n_chunks = u.shape[0] // SSM_CHUNK
    ub = u.astype(BF16)
    up = jnp.concatenate(
        [_dot(perm_ref[...], ub[c * SSM_CHUNK:(c + 1) * SSM_CHUNK]).astype(BF16)
         for c in range(n_chunks)], axis=0)
    kw = bre_ref.shape[1]
    nw = bre_ref.shape[2]
    for kb in range(bre_ref.shape[0]):
        ukb = up[:, kb * kw:(kb + 1) * kw]
        for part, w_ref in enumerate((bre_ref, bim_ref)):
            bu = _dot(ukb, w_ref[kb])
            for c in range(n_chunks):
                h_scr[2 * c + part][:, kb * nw:(kb + 1) * nw] = bu[c * SSM_CHUNK:(c + 1) * SSM_CHUNK]


def _ssm_chunk(u, permt_ref, are_ref, aim_ref, sre_ref, sim_ref, cre_ref, cim_ref, d_ref,
               hre, him, car_re, car_im, anchors=()):
    if callable(anchors):
        anchors = anchors()

    nstate = hre.shape[1]
    for cb in range(nstate // SSM_COLS):
        cs = slice(cb * SSM_COLS, (cb + 1) * SSM_COLS)
        a_re = jnp.broadcast_to(are_ref[:, cs], (SUBLANES, SSM_COLS))
        a_im = jnp.broadcast_to(aim_ref[:, cs], (SUBLANES, SSM_COLS))

        e_re = hre[0:SUBLANES, cs]
        e_im = him[0:SUBLANES, cs]
        for k in range(1, SSM_SEG):
            r = slice(k * SUBLANES, (k + 1) * SUBLANES)
            e_re, e_im = (a_re * e_re - a_im * e_im + hre[r, cs],
                          a_re * e_im + a_im * e_re + him[r, cs])
            hre[r, cs] = e_re
            him[r, cs] = e_im

        s_re = sre_ref[:, cs]
        s_im = sim_ref[:, cs]
        c_re = car_re[:, cs]
        c_im = car_im[:, cs]
        rows_re, rows_im = [], []
        for j in range(SUBLANES):
            rows_re.append(c_re)
            rows_im.append(c_im)
            n_re = s_re * c_re - s_im * c_im + e_re[j:j + 1]
            n_im = s_re * c_im + s_im * c_re + e_im[j:j + 1]
            c_re, c_im = n_re, n_im
        car_re[:, cs] = c_re
        car_im[:, cs] = c_im

        d_re = jnp.concatenate(rows_re, axis=0)
        d_im = jnp.concatenate(rows_im, axis=0)
        for k in range(SSM_SEG):
            r = slice(k * SUBLANES, (k + 1) * SUBLANES)
            d_re, d_im = a_re * d_re - a_im * d_im, a_re * d_im + a_im * d_re
            if k == SSM_SEG // 2:
                for z in anchors[cb::nstate // SSM_COLS]:
                    d_re = d_re + jnp.concatenate([z] * (SSM_COLS // LANES), axis=1)
            hre[r, cs] = hre[r, cs] + d_re
            him[r, cs] = him[r, cs] + d_im

    ncb = cre_ref.shape[0]
    cw = cre_ref.shape[1]
    ys = []
    for kb in range(ncb):
        h_re = hre[:, kb * cw:(kb + 1) * cw].astype(BF16)
        h_im = him[:, kb * cw:(kb + 1) * cw].astype(BF16)
        ys.append(_dot(h_re, cre_ref[kb]) - _dot(h_im, cim_ref[kb]))
    yp = jnp.concatenate(ys, axis=-1)
    hi = yp.astype(BF16)
    lo = (yp - hi.astype(F32)).astype(BF16)
    y = _dot(permt_ref[...], hi) + _dot(permt_ref[...], lo)
    y = y + d_ref[...] * u
    return _gelu_tanh(y).astype(BF16)


def _glu(g, wglu_ref):
    z = _dot(g, wglu_ref[...])
    half = z.shape[-1] // 2
    return z[:, :half] * jax.nn.sigmoid(z[:, half:])


def _schedule_anchor(x):
    rows, cols = x.shape
    s = x[:, :LANES]
    for c in range(1, cols // LANES):
        s = s + x[:, c * LANES:(c + 1) * LANES]
    s = jnp.sum(s.reshape(rows // SUBLANES, SUBLANES, LANES), axis=0)
    bits = pltpu.bitcast(s, jnp.uint32)
    return ((bits >> 16) >> 16).astype(F32)


def _in_ssm_kernel(x_ref, g_ref, wa_ref, wb_ref, wc_ref, bf_ref, *rest, n_cast):
    nconst = 12
    (perm_ref, permt_ref, bre_ref, bim_ref, are_ref, aim_ref, sre_ref, sim_ref,
     cre_ref, cim_ref, d_ref, wglu_ref) = rest[:nconst]
    cast_in = rest[nconst:nconst + n_cast]
    outs = rest[nconst + n_cast:]
    q_ref, k_ref, v_ref, cum_ref, oa_ref, wg_ref = outs[:6]
    cast_out = outs[6:6 + n_cast]
    carry_ref, car_re, car_im, w_bf, wf_bf = outs[6 + n_cast:11 + n_cast]
    h_scr = outs[11 + n_cast:]
    n_forget = cum_ref.shape[1]
    for src, dst in zip(cast_in, cast_out):
        dst[...] = src[...].astype(BF16)

    @pl.when(pl.program_id(1) == 0)
    def _():
        carry_ref[...] = jnp.zeros_like(carry_ref)
        car_re[...] = jnp.zeros_like(car_re)
        car_im[...] = jnp.zeros_like(car_im)
        w_bf[...] = wa_ref[...].T.astype(BF16)
        wf_bf[...] = wb_ref[:LANES, :].T.astype(BF16)
        wg_ref[...] = jnp.concatenate([wb_ref[n_forget:, :], wc_ref[...]], axis=0).T.astype(BF16)

    un = _rms(x_ref[...], g_ref[...]).astype(BF16)
    wu = d_ref.shape[1]
    wh = q_ref.shape[1]
    col = lambda lo, hi: _dot(un, w_bf[:, lo:hi])
    u = col(0, wu)
    pieces = [(ref, lo + p * MXU_TILE, p * MXU_TILE, scale)
              for ref, lo, scale in ((q_ref, wu, FOX_HEAD_DIM ** -0.5 * LOG2E),
                                     (k_ref, wu + wh, None), (v_ref, wu + 2 * wh, None))
              for p in range(wh // MXU_TILE)]
    n_chunks = u.shape[0] // SSM_CHUNK
    per_chunk = -(-len(pieces) // n_chunks)

    def project(todo):
        anchors = []
        for ref, src, dst, scale in todo:
            piece = col(src, src + MXU_TILE)
            if scale is not None:
                piece = piece * scale
            ref[:, dst:dst + MXU_TILE] = piece.astype(BF16)
            anchors.append(_schedule_anchor(piece))
        return anchors

    _ssm_drive(u, perm_ref, bre_ref, bim_ref, h_scr)
    acts = []
    for c in range(n_chunks):
        rows = slice(c * SSM_CHUNK, (c + 1) * SSM_CHUNK)
        todo = pieces[c * per_chunk:(c + 1) * per_chunk]
        acts.append(_ssm_chunk(u[rows], permt_ref, are_ref, aim_ref, sre_ref, sim_ref, cre_ref,
                               cim_ref, d_ref, h_scr[2 * c], h_scr[2 * c + 1], car_re, car_im,
                               functools.partial(project, todo)))
    oa_ref[...] = _glu(jnp.concatenate(acts, axis=0), wglu_ref)
    c = jax.nn.log_sigmoid(_dot(un, wf_bf[...]) + bf_ref[...])
    rows = c.shape[0]
    row = lax.broadcasted_iota(jnp.int32, c.shape, 0)
    shift = 1
    while shift < rows:
        c = c + jnp.where(row >= shift, pltpu.roll(c, shift, 0), 0.0)
        shift *= 2
    c = c + carry_ref[...]
    carry_ref[...] = c[rows - 1:rows, :]
    cum_ref[0] = (c * LOG2E).T[:cum_ref.shape[1], :]


def _in_ssm(x2, gain, w_in, bf, ssm_consts, later_weights, ssm_w, fox_w, n_heads, bsz, seq):
    t, d = x2.shape
    tm = TM_PROJ
    nb = seq // tm
    steps = bsz * nb
    slabs = [w.reshape(steps, w.shape[0] // steps, w.shape[1]) for w in later_weights]
    slab = lambda s: pl.BlockSpec((1,) + s.shape[1:], lambda b, i: (b * nb + i, 0, 0))
    assert tm % SSM_CHUNK == 0 and len(ssm_consts) == 12
    nstate = ssm_consts[4].shape[1]
    dm = ssm_consts[11].shape[1] // 2
    main_w = ssm_w + 3 * fox_w
    gate_w = w_in.shape[1] - main_w - n_heads
    assert gate_w == main_w and main_w % LANES == 0 and n_heads == SUBLANES
    win = lambda rows, idx: pl.BlockSpec((rows, d), lambda *_: (idx, 0),
                                         pipeline_mode=pl.Buffered(1))
    w_t = w_in.T
    row = lambda w: pl.BlockSpec((tm, w), lambda b, i: (b * nb + i, 0))
    h_scr = [pltpu.VMEM((SSM_CHUNK, nstate), F32)] * (2 * (tm // SSM_CHUNK))
    outs = pl.pallas_call(
        functools.partial(_in_ssm_kernel, n_cast=len(slabs)),
        grid=(bsz, nb),
        in_specs=[row(d), _full((1, d)), win(main_w, 0), win(main_w, 1),
                  win(n_heads, 2 * main_w // n_heads), _full((1, LANES))]
                 + [_full(c.shape) for c in ssm_consts] + [slab(s) for s in slabs],
        out_specs=[row(fox_w), row(fox_w), row(fox_w),
                   pl.BlockSpec((1, n_heads, tm), lambda b, i: (b, 0, i)), row(dm),
                   _full((d, gate_w))] + [slab(s) for s in slabs],
        out_shape=[jax.ShapeDtypeStruct((t, fox_w), BF16),
                   jax.ShapeDtypeStruct((t, fox_w), BF16),
                   jax.ShapeDtypeStruct((t, fox_w), BF16),
                   jax.ShapeDtypeStruct((bsz, n_heads, seq), F32),
                   jax.ShapeDtypeStruct((t, dm), F32),
                   jax.ShapeDtypeStruct((d, gate_w), BF16)]
                  + [jax.ShapeDtypeStruct(s.shape, BF16) for s in slabs],
        scratch_shapes=[pltpu.VMEM((1, LANES), F32), pltpu.VMEM((1, nstate), F32),
                        pltpu.VMEM((1, nstate), F32), pltpu.VMEM((d, main_w), BF16),
                        pltpu.VMEM((d, LANES), BF16)] + h_scr,
        compiler_params=_cparams(("arbitrary", "arbitrary")),
        name="in_ssm",
    )(x2, gain.reshape(1, d), w_t, w_t, w_t, bf, *ssm_consts, *slabs)
    cast = [o.reshape(w.shape) for o, w in zip(outs[6:], later_weights)]
    return list(outs[:6]) + cast


def _fox_kernel(q_ref, k_ref, v_ref, ck_ref, o_ref, m_scr, acc_scr, first_blk):
    seq = q_ref.shape[0]
    tq = FOX_TQ
    tk = FOX_TK
    dh = FOX_HEAD_DIM
    n_heads = ck_ref.shape[1]
    lane = lax.broadcasted_iota(jnp.int32, (1, LANES), 1)
    keep = [jnp.where(lane < dh, 1.0, 0.0).astype(BF16), jnp.where(lane >= dh, 1.0, 0.0).astype(BF16)]

    def block(q, j, rows, h, masked, first=False):
        e = h % 2
        ps = slice((h // 2) * LANES, (h // 2 + 1) * LANES)
        hs = slice(h * LANES, (h + 1) * LANES)
        ks = pl.ds(pl.multiple_of(j * tk, tk), tk)
        s = _dot_nt(q, k_ref[ks, ps] * keep[e]) - ck_ref[0, h, pl.ds(j, 1), :]
        if masked:
            r = lax.broadcasted_iota(jnp.int32, (tk, tk), 0)
            c = lax.broadcasted_iota(jnp.int32, (tk, tk), 1)
            top = jnp.where(c <= r, s[:tk], -jnp.inf)
            s = top if s.shape[0] == tk else jnp.concatenate([top, s[tk:]], axis=0)
        m_new = jnp.broadcast_to(jnp.max(s, axis=1, keepdims=True), (s.shape[0], LANES))
        if not first:
            m_prev = m_scr[rows, hs]
            m_new = jnp.maximum(m_prev, m_new)
        p = jnp.exp2(s - jnp.concatenate([m_new] * (tk // LANES), axis=1))
        pv = _dot(p.astype(BF16), v_ref[ks, ps] * keep[e] + keep[1 - e])
        acc_scr[rows, hs] = pv if first else jnp.exp2(m_prev - m_new) * acc_scr[rows, hs] + pv
        m_scr[rows, hs] = m_new

    nsub = tq // tk
    nq = seq // tq
    nk = seq // tk
    all_rows = slice(0, tq)

    col = lax.broadcasted_iota(jnp.int32, (q_ref.shape[1], LANES), 0) // dh
    head_sel = jnp.where(col == lax.broadcasted_iota(jnp.int32, (q_ref.shape[1], LANES), 1),
                         1.0, 0.0).astype(BF16)

    def max_sq_norm(ref, lo, n):
        x = ref[lo:lo + n, :]
        sq = _dot(x * x, head_sel)
        return jnp.max(sq, axis=0, keepdims=True) * FOX_NORM_SLACK

    k_sq = jnp.concatenate([max_sq_norm(k_ref, j * tk, tk) for j in range(nk)], axis=0)
    q_sq = jnp.concatenate([max_sq_norm(q_ref, i * tq, tq) for i in range(nq)], axis=0)
    blk_id = lax.broadcasted_iota(jnp.int32, (nk, 1), 0).astype(F32)
    for h in range(n_heads):
        k_max = jnp.sqrt(k_sq[:, h:h + 1])
        c_end = ck_ref[0, h, :, tk - 1:tk]
        for i in range(nq):
            first_diag = float(i * nsub)
            q_max = jnp.sqrt(q_sq[i:i + 1, h:h + 1])
            k_own = jnp.max(k_max[i * nsub:(i + 1) * nsub], axis=0, keepdims=True)
            c_start = ck_ref[0, h, i * nsub:i * nsub + 1, 0:1]
            bound = q_max * (k_max + k_own) + (c_start - c_end)
            visit = jnp.logical_and(bound >= -FOX_SKIP_LOG2, blk_id < first_diag)
            first_blk[h * nq + i] = jnp.min(jnp.where(visit, blk_id, first_diag)).astype(jnp.int32)

    def tile_rows(qi):
        return pl.ds(pl.multiple_of(qi * tq, tq), tq)

    def start_tile(qi):
        qp = [q_ref[tile_rows(qi), p * LANES:(p + 1) * LANES] for p in range(n_heads // 2)]
        for r in range(nsub):
            rows = slice(r * tk, tq)
            for h in range(n_heads):
                block(qp[h // 2][rows], qi * nsub + r, rows, h, True, first=r == 0)
        return qp

    def finish_tile(qi):
        for p in range(n_heads // 2):
            a0 = acc_scr[:, (2 * p) * LANES:(2 * p + 1) * LANES]
            a1 = acc_scr[:, (2 * p + 1) * LANES:(2 * p + 2) * LANES]
            sums = pltpu.roll(jnp.where(lane < dh, a1, a0), dh, axis=1)
            out = jnp.where(lane < dh, a0, a1) / sums
            o_ref[tile_rows(qi), p * LANES:(p + 1) * LANES] = out.astype(o_ref.dtype)

    def qblock(qi, _):
        finish_tile(qi - 1)
        qp = start_tile(qi)

        def full(j, _):
            for h in range(n_heads):
                block(qp[h // 2], j, all_rows, h, False)
            return 0

        starts = [first_blk[h * nq + qi] for h in range(n_heads)]
        common = functools.reduce(jnp.maximum, starts)
        for h in range(n_heads):
            def one(j, _, h=h):
                block(qp[h // 2], j, all_rows, h, False)
                return 0

            def two(t, _, h=h):
                block(qp[h // 2], starts[h] + 2 * t, all_rows, h, False)
                block(qp[h // 2], starts[h] + 2 * t + 1, all_rows, h, False)
                return 0

            pairs = lax.shift_right_logical(common - starts[h], 1)
            lax.fori_loop(0, pairs, two, 0)
            lax.fori_loop(starts[h] + 2 * pairs, common, one, 0)
        lax.fori_loop(common, qi * nsub, full, 0)
        return 0

    start_tile(jnp.int32(0))
    lax.fori_loop(1, nq, qblock, 0)
    finish_tile(jnp.int32(nq - 1))


def _fox(q, k, v, cum_t, bsz, seq):
    t, w = q.shape
    assert FOX_TQ % FOX_TK == 0 and seq % FOX_TQ == 0 and FOX_HEADS_PER_STEP % 2 == 0
    hps = FOX_HEADS_PER_STEP
    wblk = hps * FOX_HEAD_DIM
    blk = pl.BlockSpec((seq, wblk), lambda bi, hi: (bi, hi))
    return pl.pallas_call(
        _fox_kernel,
        grid=(bsz, w // wblk),
        in_specs=[blk, blk, blk,
                  pl.BlockSpec((1, hps) + cum_t.shape[2:], lambda bi, hi: (bi, hi, 0, 0))],
        out_specs=blk,
        out_shape=jax.ShapeDtypeStruct((t, w), BF16),
        scratch_shapes=[pltpu.VMEM((FOX_TQ, hps * LANES), F32),
                        pltpu.VMEM((FOX_TQ, hps * LANES), F32),
                        pltpu.SMEM((hps * (seq // FOX_TQ),), jnp.int32)],
        compiler_params=_cparams(("arbitrary", "arbitrary")),
        name="fox",
    )(q, k, v, cum_t)


def _merge_kernel(x_ref, oa_ref, att_ref, gmix_ref, wg_ref, wfo_ref, wmix_ref, gq_ref, wq_ref,
                  km_ref, vm_ref, wo_ref, h_ref):
    d = x_ref.shape[-1]
    x = x_ref[...]
    gate = jax.nn.sigmoid(_dot(_rms(x, gmix_ref[...]).astype(BF16), wg_ref[...]))
    out_b = _dot(att_ref[...], wfo_ref[...])
    mix = gate[:, :d] * oa_ref[...] + gate[:, d:] * out_b
    h1 = x + _dot(mix.astype(BF16), wmix_ref[...])

    n = _rms(h1, gq_ref[...]).astype(BF16)
    qm = _dot(n, wq_ref[...])
    hd = qm.shape[-1] // MEM_HEADS
    qm = (qm * (hd ** -0.5)).astype(BF16)
    outs = []
    for hh in range(MEM_HEADS):
        hs = slice(hh * hd, (hh + 1) * hd)
        s = _dot_nt(qm[:, hs], km_ref[0, :, hs])
        s = s - jnp.max(s, axis=-1, keepdims=True)
        p = jnp.exp(s)
        p = p / jnp.sum(p, axis=-1, keepdims=True)
        outs.append(_dot(p.astype(BF16), vm_ref[0, :, hs]))
    o = jnp.concatenate(outs, axis=-1).astype(BF16)
    h_ref[...] = h1 + _dot(o, wo_ref[...])


def _ffn_kernel(h_ref, gf_ref, win_ref, wout_ref, gfin_ref, o_ref):
    h2 = h_ref[...]
    f = _rms(h2, gf_ref[...]).astype(BF16)
    hidden = wout_ref.shape[0]
    acc = jnp.zeros_like(h2)
    for lo, hi in _ffn_chunks(hidden):
        fa = _dot(f, win_ref[:, lo:hi])
        fb = _dot(f, win_ref[:, hidden + lo:hidden + hi])
        g = (fa * jax.nn.sigmoid(fa) * fb).astype(BF16)
        acc = acc + _dot(g, wout_ref[lo:hi, :])
    o_ref[...] = _rms(h2 + acc, gfin_ref[...])


def _ffn_chunks(hidden):
    tiles = hidden // MXU_TILE
    assert tiles * MXU_TILE == hidden
    cuts = [-(-tiles * c // FFN_CHUNKS) * MXU_TILE for c in range(FFN_CHUNKS + 1)]
    return list(zip(cuts[:-1], cuts[1:]))


def _resident(shape):
    n = len(shape)
    return pl.BlockSpec(shape, lambda *_: (0,) * n, pipeline_mode=pl.Buffered(1))


def _post(x2, out_a, att, gain_mix, w_gate, w_fox_o, w_mix, gain_q, w_q, k_m, v_m, w_o,
          gain_f, w_ffn_in, w_ffn_out, gain_fin, bsz, seq):
    t, d = x2.shape
    tm = TM_POST
    nb = seq // tm
    row = lambda w: pl.BlockSpec((tm, w), lambda b, i: (b * nb + i, 0))
    mem = pl.BlockSpec((1,) + k_m.shape[1:], lambda b, i: (b, 0, 0))
    gain = lambda g: g.reshape(1, d)
    weights = lambda *ws: [_resident(w.shape) for w in ws]
    h2 = pl.pallas_call(
        _merge_kernel,
        grid=(bsz, nb),
        in_specs=[row(d), row(d), row(att.shape[1]), _full((1, d))]
                 + weights(w_gate, w_fox_o, w_mix) + [_full((1, d))] + weights(w_q)
                 + [mem, mem] + weights(w_o),
        out_specs=row(d),
        out_shape=jax.ShapeDtypeStruct((t, d), F32),
        compiler_params=_cparams(("arbitrary", "arbitrary")),
        name="merge",
    )(x2, out_a, att, gain(gain_mix), w_gate, w_fox_o, w_mix, gain(gain_q), w_q, k_m, v_m, w_o)
    blk = pl.BlockSpec((tm, d), lambda i: (i, 0))
    return pl.pallas_call(
        _ffn_kernel,
        grid=(t // tm,),
        in_specs=[blk, _full((1, d))] + weights(w_ffn_in, w_ffn_out) + [_full((1, d))],
        out_specs=blk,
        out_shape=jax.ShapeDtypeStruct((t, d), F32),
        compiler_params=_cparams(("arbitrary",)),
        name="ffn",
    )(h2, gain(gain_f), w_ffn_in, w_ffn_out, gain(gain_fin))


def _seg_perm(q):
    seg = q // SUBLANES
    r = jnp.arange(q)
    src = (r % SUBLANES) * seg + r // SUBLANES
    return (src[:, None] == jnp.arange(q)[None, :]).astype(BF16)


def _layer(x2, mem, bsz, seq, norm_mix, w_in, b_forget, lam_re, lam_im, log_dt, b_re, b_im, c_re,
           c_im, d_skip, w_glu, w_fox_o, w_mix_out, norm_mem_q, norm_mem_kv, w_mem_q, w_mem_kv,
           w_mem_o, norm_ffn, w_ffn_in, w_ffn_out, norm_final):
    d = x2.shape[-1]
    ssm_w = lam_re.shape[0] * SSM_GROUP
    n_heads = b_forget.shape[0]
    fox_w = n_heads * FOX_HEAD_DIM

    assert b_re.shape[-1] == SSM_GROUP
    a_re, a_im, s_re, s_im, bd_bre, bd_bim, bd_cre, bd_cim = _ssm_prep(
        lam_re, lam_im, log_dt, b_re, b_im, c_re, c_im)
    perm = _seg_perm(SSM_CHUNK)

    bf = jnp.zeros((1, LANES), F32).at[0, :n_heads].set(b_forget)
    ssm_consts = [perm, perm.T, bd_bre, bd_bim, a_re, a_im, s_re, s_im, bd_cre, bd_cim,
                  d_skip.reshape(1, ssm_w), w_glu.astype(BF16)]
    post_weights = [w_fox_o, w_mix_out, w_mem_q, w_mem_o, w_ffn_in, w_ffn_out]
    q, k, v, cum_t, out_a, w_gate, *post_bf16 = _in_ssm(
        x2, norm_mix, w_in, bf, ssm_consts, post_weights, ssm_w, fox_w, n_heads, bsz, seq)
    wb_fox_o, wb_mix, wb_mem_q, wb_mem_o, wb_ffn_in, wb_ffn_out = post_bf16

    att = _fox(q, k, v, cum_t.reshape(bsz, n_heads, seq // FOX_TK, FOX_TK), bsz, seq)

    k_m, v_m = _mem_kv(mem, norm_mem_kv, w_mem_kv)
    return _post(x2, out_a, att, norm_mix, w_gate, wb_fox_o, wb_mix, norm_mem_q, wb_mem_q, k_m, v_m,
                 wb_mem_o, norm_ffn, wb_ffn_in, wb_ffn_out, norm_final, bsz, seq)


def kernel(x, mem, norm_mix, w_in, b_forget, lam_re, lam_im, log_dt, b_re, b_im, c_re, c_im, d_skip,
           w_glu, w_fox_o, w_mix_out, norm_mem_q, norm_mem_kv, w_mem_q, w_mem_kv, w_mem_o, norm_ffn,
           w_ffn_in, w_ffn_out, norm_final):
    bsz, seq, d = x.shape
    assert w_in.shape[0] == 1, "single-layer block"
    out = _layer(x.reshape(bsz * seq, d), mem, bsz, seq, norm_mix[0], w_in[0], b_forget[0],
                 lam_re[0], lam_im[0], log_dt[0], b_re[0], b_im[0], c_re[0], c_im[0], d_skip[0],
                 w_glu[0], w_fox_o[0], w_mix_out[0], norm_mem_q[0], norm_mem_kv[0], w_mem_q[0],
                 w_mem_kv[0], w_mem_o[0], norm_ffn[0], w_ffn_in[0], w_ffn_out[0], norm_final)
    return out.reshape(bsz, seq, d)
```

```python
import functools
import math

import jax
import jax.numpy as jnp
from jax import lax
from jax.experimental import pallas as pl
from jax.experimental.pallas import tpu as pltpu

F32 = jnp.float32
BF16 = jnp.bfloat16

RMS_EPS = 1e-6
LOG2E = math.log2(math.e)
SSM_GROUP = 16
FOX_HEAD_DIM = 64
MEM_HEADS = 4
LANES = 128
SUBLANES = 8
MXU_TILE = 256
VMEM_LIMIT = 56 * 1024 * 1024

TM_PROJ = 512
TM_POST = 1024
FFN_CHUNKS = 2
SSM_CHUNK = 256
SSM_SEG = SSM_CHUNK // SUBLANES
SSM_COLS = 512
FOX_TQ = 1024
FOX_TK = 512
FOX_HEADS_PER_STEP = 4
FOX_SKIP_LOG2 = 160.0
FOX_NORM_SLACK = 1.02


def _cparams(sem):
    return pltpu.CompilerParams(dimension_semantics=sem, vmem_limit_bytes=VMEM_LIMIT)


def _rms(x, gain):
    return x * lax.rsqrt(jnp.mean(x * x, axis=-1, keepdims=True) + RMS_EPS) * gain


def _dot(a, b):
    return jnp.dot(a, b, preferred_element_type=F32)


def _dot_nt(a, b):
    return lax.dot_general(a, b, (((1,), (1,)), ((), ())), preferred_element_type=F32)


def _full(shape):
    n = len(shape)
    return pl.BlockSpec(shape, lambda *_: (0,) * n)


def _ssm_prep_kernel(lre_ref, lim_ref, ldt_ref, bre_ref, bim_ref, cre_ref, cim_ref,
                     are_ref, aim_ref, sre_ref, sim_ref, bbre_ref, bbim_ref, ccre_ref, ccim_ref):
    lre = lre_ref[...]
    lim = lim_ref[...]
    dt = jnp.exp(ldt_ref[...])
    zr = lre * dt
    zi = lim * dt
    mag = jnp.exp(zr)
    are = mag * jnp.cos(zi)
    aim = mag * jnp.sin(zi)
    are_ref[...] = are
    aim_ref[...] = aim
    mag_s = jnp.exp(zr * SSM_SEG)
    sre_ref[...] = mag_s * jnp.cos(zi * SSM_SEG)
    sim_ref[...] = mag_s * jnp.sin(zi * SSM_SEG)
    nr = are - 1.0
    ni = aim
    den = lre * lre + lim * lim
    fr = (nr * lre + ni * lim) / den
    fi = (ni * lre - nr * lim) / den
    bre = bre_ref[...]
    bim = bim_ref[...]
    nblk, rows, cols = bbre_ref.shape
    n = bre.shape[0]
    per = rows // n
    same_group = (lax.broadcasted_iota(jnp.int32, (rows, cols), 0) // n
                  == lax.broadcasted_iota(jnp.int32, (rows, cols), 1) // (cols // per))

    def block_diag(x, kb):
        slab = x[:, kb * cols:(kb + 1) * cols]
        return jnp.where(same_group, jnp.concatenate([slab] * per, axis=0), 0.0)

    b_bar = (fr * bre - fi * bim, fr * bim + fi * bre)
    for kb in range(nblk):
        bbre_ref[kb] = block_diag(b_bar[0], kb).astype(BF16)
        bbim_ref[kb] = block_diag(b_bar[1], kb).astype(BF16)
        ccre_ref[kb] = block_diag(cre_ref[...], kb).T.astype(BF16)
        ccim_ref[kb] = block_diag(cim_ref[...], kb).T.astype(BF16)


def _ssm_prep(lam_re, lam_im, log_dt, b_re, b_im, c_re, c_im):
    g, p = lam_re.shape
    n = b_re.shape[-1]
    c = g * p
    per = LANES // n
    nblk = g // per
    row = lambda a: a.reshape(1, c)
    ldt = jnp.broadcast_to(log_dt[:, None], (g, p))
    blk = lambda r, cc: jax.ShapeDtypeStruct((nblk, r, cc), BF16)
    return pl.pallas_call(
        _ssm_prep_kernel,
        out_shape=[jax.ShapeDtypeStruct((1, c), F32)] * 4
                  + [blk(LANES, per * p)] * 2 + [blk(per * p, LANES)] * 2,
        name="ssm_prep",
    )(row(lam_re), row(lam_im), row(ldt),
      b_re.transpose(2, 0, 1).reshape(n, c), b_im.transpose(2, 0, 1).reshape(n, c),
      c_re.transpose(1, 0, 2).reshape(n, c), c_im.transpose(1, 0, 2).reshape(n, c))


def _mem_kv_kernel(m_ref, g_ref, w_ref, k_ref, v_ref):
    n = _rms(m_ref[0], g_ref[...]).astype(BF16)
    kv = _dot(n, w_ref[...].astype(BF16))
    half = kv.shape[-1] // 2
    k_ref[0] = kv[:, :half].astype(BF16)
    v_ref[0] = kv[:, half:].astype(BF16)


def _mem_kv(mem, gain, w_kv):
    b, m, d = mem.shape
    w2 = w_kv.shape[-1]
    return pl.pallas_call(
        _mem_kv_kernel,
        grid=(b,),
        in_specs=[pl.BlockSpec((1, m, d), lambda i: (i, 0, 0)), _full((1, d)), _full((d, w2))],
        out_specs=[pl.BlockSpec((1, m, w2 // 2), lambda i: (i, 0, 0))] * 2,
        out_shape=[jax.ShapeDtypeStruct((b, m, w2 // 2), BF16)] * 2,
        compiler_params=_cparams(("arbitrary",)),
        name="mem_kv",
    )(mem, gain.reshape(1, d), w_kv)


def _gelu_tanh(x):
    return 0.5 * x * (1.0 + jnp.tanh(math.sqrt(2.0 / math.pi) * (x + 0.044715 * (x * x * x))))


def _sigmoid(x):
    return 0.5 * jnp.tanh(0.5 * x) + 0.5


def _ssm_drive(u, perm_ref, bre_ref, bim_ref, h_scr):
    n_chunks = u.shape[0] // SSM_CHUNK
    ub = u.astype(BF16)
    up = jnp.concatenate(
        [_dot(perm_ref[...], ub[c * SSM_CHUNK:(c + 1) * SSM_CHUNK]).astype(BF16)
         for c in range(n_chunks)], axis=0)
    kw = bre_ref.shape[1]
    nw = bre_ref.shape[2]
    for kb in range(bre_ref.shape[0]):
        ukb = up[:, kb * kw:(kb + 1) * kw]
        for part, w_ref in enumerate((bre_ref, bim_ref)):
            bu = _dot(ukb, w_ref[kb])
            for c in range(n_chunks):
                h_scr[2 * c + part][:, kb * nw:(kb + 1) * nw] = bu[c * SSM_CHUNK:(c + 1) * SSM_CHUNK]


def _ssm_chunk(u, permt_ref, are_ref, aim_ref, sre_ref, sim_ref, cre_ref, cim_ref, d_ref,
               hre, him, car_re, car_im, anchors=()):
    if callable(anchors):
        anchors = anchors()

    nstate = hre.shape[1]
    for cb in range(nstate // SSM_COLS):
        cs = slice(cb * SSM_COLS, (cb + 1) * SSM_COLS)
        a_re = jnp.broadcast_to(are_ref[:, cs], (SUBLANES, SSM_COLS))
        a_im = jnp.broadcast_to(aim_ref[:, cs], (SUBLANES, SSM_COLS))

        e_re = hre[0:SUBLANES, cs]
        e_im = him[0:SUBLANES, cs]
        for k in range(1, SSM_SEG):
            r = slice(k * SUBLANES, (k + 1) * SUBLANES)
            e_re, e_im = (a_re * e_re - a_im * e_im + hre[r, cs],
                          a_re * e_im + a_im * e_re + him[r, cs])
            hre[r, cs] = e_re
            him[r, cs] = e_im

        s_re = sre_ref[:, cs]
        s_im = sim_ref[:, cs]
        c_re = car_re[:, cs]
        c_im = car_im[:, cs]
        rows_re, rows_im = [], []
        for j in range(SUBLANES):
            rows_re.append(c_re)
            rows_im.append(c_im)
            n_re = s_re * c_re - s_im * c_im + e_re[j:j + 1]
            n_im = s_re * c_im + s_im * c_re + e_im[j:j + 1]
            c_re, c_im = n_re, n_im
        car_re[:, cs] = c_re
        car_im[:, cs] = c_im

        d_re = jnp.concatenate(rows_re, axis=0)
        d_im = jnp.concatenate(rows_im, axis=0)
        for k in range(SSM_SEG):
            r = slice(k * SUBLANES, (k + 1) * SUBLANES)
            d_re, d_im = a_re * d_re - a_im * d_im, a_re * d_im + a_im * d_re
            if k == SSM_SEG // 2:
                for z in anchors[cb::nstate // SSM_COLS]:
                    d_re = d_re + jnp.concatenate([z] * (SSM_COLS // LANES), axis=1)
            hre[r, cs] = hre[r, cs] + d_re
            him[r, cs] = him[r, cs] + d_im

    ncb = cre_ref.shape[0]
    cw = cre_ref.shape[1]
    ys = []
    for kb in range(ncb):
        h_re = hre[:, kb * cw:(kb + 1) * cw].astype(BF16)
        h_im = him[:, kb * cw:(kb + 1) * cw].astype(BF16)
        ys.append(_dot(h_re, cre_ref[kb]) - _dot(h_im, cim_ref[kb]))
    yp = jnp.concatenate(ys, axis=-1)
    hi = yp.astype(BF16)
    lo = (yp - hi.astype(F32)).astype(BF16)
    y = _dot(permt_ref[...], hi) + _dot(permt_ref[...], lo)
    y = y + d_ref[...] * u
    return _gelu_tanh(y).astype(BF16)


def _glu(g, wglu_ref):
    z = _dot(g, wglu_ref[...])
    half = z.shape[-1] // 2
    return z[:, :half] * jax.nn.sigmoid(z[:, half:])


def _schedule_anchor(x):
    rows, cols = x.shape
    s = x[:, :LANES]
    for c in range(1, cols // LANES):
        s = s + x[:, c * LANES:(c + 1) * LANES]
    s = jnp.sum(s.reshape(rows // SUBLANES, SUBLANES, LANES), axis=0)
    bits = pltpu.bitcast(s, jnp.uint32)
    return ((bits >> 16) >> 16).astype(F32)


def _in_ssm_kernel(x_ref, g_ref, wa_ref, wb_ref, wc_ref, bf_ref, *rest, n_cast):
    nconst = 12
    (perm_ref, permt_ref, bre_ref, bim_ref, are_ref, aim_ref, sre_ref, sim_ref,
     cre_ref, cim_ref, d_ref, wglu_ref) = rest[:nconst]
    cast_in = rest[nconst:nconst + n_cast]
    outs = rest[nconst + n_cast:]
    q_ref, k_ref, v_ref, cum_ref, oa_ref, wg_ref = outs[:6]
    cast_out = outs[6:6 + n_cast]
    carry_ref, car_re, car_im, w_bf, wf_bf = outs[6 + n_cast:11 + n_cast]
    h_scr = outs[11 + n_cast:]
    n_forget = cum_ref.shape[1]
    for src, dst in zip(cast_in, cast_out):
        dst[...] = src[...].astype(BF16)

    @pl.when(pl.program_id(1) == 0)
    def _():
        carry_ref[...] = jnp.zeros_like(carry_ref)
        car_re[...] = jnp.zeros_like(car_re)
        car_im[...] = jnp.zeros_like(car_im)
        w_bf[...] = wa_ref[...].T.astype(BF16)
        wf_bf[...] = wb_ref[:LANES, :].T.astype(BF16)
        wg_ref[...] = jnp.concatenate([wb_ref[n_forget:, :], wc_ref[...]], axis=0).T.astype(BF16)

    un = _rms(x_ref[...], g_ref[...]).astype(BF16)
    wu = d_ref.shape[1]
    wh = q_ref.shape[1]
    col = lambda lo, hi: _dot(un, w_bf[:, lo:hi])
    u = col(0, wu)
    pieces = [(ref, lo + p * MXU_TILE, p * MXU_TILE, scale)
              for ref, lo, scale in ((q_ref, wu, FOX_HEAD_DIM ** -0.5 * LOG2E),
                                     (k_ref, wu + wh, None), (v_ref, wu + 2 * wh, None))
              for p in range(wh // MXU_TILE)]
    n_chunks = u.shape[0] // SSM_CHUNK
    per_chunk = -(-len(pieces) // n_chunks)

    def project(todo):
        anchors = []
        for ref, src, dst, scale in todo:
            piece = col(src, src + MXU_TILE)
            if scale is not None:
                piece = piece * scale
            ref[:, dst:dst + MXU_TILE] = piece.astype(BF16)
            anchors.append(_schedule_anchor(piece))
        return anchors

    _ssm_drive(u, perm_ref, bre_ref, bim_ref, h_scr)
    acts = []
    for c in range(n_chunks):
        rows = slice(c * SSM_CHUNK, (c + 1) * SSM_CHUNK)
        todo = pieces[c * per_chunk:(c + 1) * per_chunk]
        acts.append(_ssm_chunk(u[rows], permt_ref, are_ref, aim_ref, sre_ref, sim_ref, cre_ref,
                               cim_ref, d_ref, h_scr[2 * c], h_scr[2 * c + 1], car_re, car_im,
                               functools.partial(project, todo)))
    oa_ref[...] = _glu(jnp.concatenate(acts, axis=0), wglu_ref)
    c = jax.nn.log_sigmoid(_dot(un, wf_bf[...]) + bf_ref[...])
    rows = c.shape[0]
    row = lax.broadcasted_iota(jnp.int32, c.shape, 0)
    shift = 1
    while shift < rows:
        c = c + jnp.where(row >= shift, pltpu.roll(c, shift, 0), 0.0)
        shift *= 2
    c = c + carry_ref[...]
    carry_ref[...] = c[rows - 1:rows, :]
    cum_ref[0] = (c * LOG2E).T[:cum_ref.shape[1], :]


def _in_ssm(x2, gain, w_in, bf, ssm_consts, later_weights, ssm_w, fox_w, n_heads, bsz, seq):
    t, d = x2.shape
    tm = TM_PROJ
    nb = seq // tm
    steps = bsz * nb
    slabs = [w.reshape(steps, w.shape[0] // steps, w.shape[1]) for w in later_weights]
    slab = lambda s: pl.BlockSpec((1,) + s.shape[1:], lambda b, i: (b * nb + i, 0, 0))
    assert tm % SSM_CHUNK == 0 and len(ssm_consts) == 12
    nstate = ssm_consts[4].shape[1]
    dm = ssm_consts[11].shape[1] // 2
    main_w = ssm_w + 3 * fox_w
    gate_w = w_in.shape[1] - main_w - n_heads
    assert gate_w == main_w and main_w % LANES == 0 and n_heads == SUBLANES
    win = lambda rows, idx: pl.BlockSpec((rows, d), lambda *_: (idx, 0),
                                         pipeline_mode=pl.Buffered(1))
    w_t = w_in.T
    row = lambda w: pl.BlockSpec((tm, w), lambda b, i: (b * nb + i, 0))
    h_scr = [pltpu.VMEM((SSM_CHUNK, nstate), F32)] * (2 * (tm // SSM_CHUNK))
    outs = pl.pallas_call(
        functools.partial(_in_ssm_kernel, n_cast=len(slabs)),
        grid=(bsz, nb),
        in_specs=[row(d), _full((1, d)), win(main_w, 0), win(main_w, 1),
                  win(n_heads, 2 * main_w // n_heads), _full((1, LANES))]
                 + [_full(c.shape) for c in ssm_consts] + [slab(s) for s in slabs],
        out_specs=[row(fox_w), row(fox_w), row(fox_w),
                   pl.BlockSpec((1, n_heads, tm), lambda b, i: (b, 0, i)), row(dm),
                   _full((d, gate_w))] + [slab(s) for s in slabs],
        out_shape=[jax.ShapeDtypeStruct((t, fox_w), BF16),
                   jax.ShapeDtypeStruct((t, fox_w), BF16),
                   jax.ShapeDtypeStruct((t, fox_w), BF16),
                   jax.ShapeDtypeStruct((bsz, n_heads, seq), F32),
                   jax.ShapeDtypeStruct((t, dm), F32),
                   jax.ShapeDtypeStruct((d, gate_w), BF16)]
                  + [jax.ShapeDtypeStruct(s.shape, BF16) for s in slabs],
        scratch_shapes=[pltpu.VMEM((1, LANES), F32), pltpu.VMEM((1, nstate), F32),
                        pltpu.VMEM((1, nstate), F32), pltpu.VMEM((d, main_w), BF16),
                        pltpu.VMEM((d, LANES), BF16)] + h_scr,
        compiler_params=_cparams(("arbitrary", "arbitrary")),
        name="in_ssm",
    )(x2, gain.reshape(1, d), w_t, w_t, w_t, bf, *ssm_consts, *slabs)
    cast = [o.reshape(w.shape) for o, w in zip(outs[6:], later_weights)]
    return list(outs[:6]) + cast


def _fox_kernel(q_ref, k_ref, v_ref, ck_ref, o_ref, m_scr, acc_scr, first_blk):
    seq = q_ref.shape[0]
    tq = FOX_TQ
    tk = FOX_TK
    dh = FOX_HEAD_DIM
    n_heads = ck_ref.shape[1]
    lane = lax.broadcasted_iota(jnp.int32, (1, LANES), 1)
    keep = [jnp.where(lane < dh, 1.0, 0.0).astype(BF16), jnp.where(lane >= dh, 1.0, 0.0).astype(BF16)]

    def block(q, j, rows, h, masked, first=False):
        e = h % 2
        ps = slice((h // 2) * LANES, (h // 2 + 1) * LANES)
        hs = slice(h * LANES, (h + 1) * LANES)
        ks = pl.ds(pl.multiple_of(j * tk, tk), tk)
        s = _dot_nt(q, k_ref[ks, ps] * keep[e]) - ck_ref[0, h, pl.ds(j, 1), :]
        if masked:
            r = lax.broadcasted_iota(jnp.int32, (tk, tk), 0)
            c = lax.broadcasted_iota(jnp.int32, (tk, tk), 1)
            top = jnp.where(c <= r, s[:tk], -jnp.inf)
            s = top if s.shape[0] == tk else jnp.concatenate([top, s[tk:]], axis=0)
        m_new = jnp.broadcast_to(jnp.max(s, axis=1, keepdims=True), (s.shape[0], LANES))
        if not first:
            m_prev = m_scr[rows, hs]
            m_new = jnp.maximum(m_prev, m_new)
        p = jnp.exp2(s - jnp.concatenate([m_new] * (tk // LANES), axis=1))
        pv = _dot(p.astype(BF16), v_ref[ks, ps] * keep[e] + keep[1 - e])
        acc_scr[rows, hs] = pv if first else jnp.exp2(m_prev - m_new) * acc_scr[rows, hs] + pv
        m_scr[rows, hs] = m_new

    nsub = tq // tk
    nq = seq // tq
    nk = seq // tk
    all_rows = slice(0, tq)

    col = lax.broadcasted_iota(jnp.int32, (q_ref.shape[1], LANES), 0) // dh
    head_sel = jnp.where(col == lax.broadcasted_iota(jnp.int32, (q_ref.shape[1], LANES), 1),
                         1.0, 0.0).astype(BF16)

    def max_sq_norm(ref, lo, n):
        x = ref[lo:lo + n, :]
        sq = _dot(x * x, head_sel)
        return jnp.max(sq, axis=0, keepdims=True) * FOX_NORM_SLACK

    k_sq = jnp.concatenate([max_sq_norm(k_ref, j * tk, tk) for j in range(nk)], axis=0)
    q_sq = jnp.concatenate([max_sq_norm(q_ref, i * tq, tq) for i in range(nq)], axis=0)
    blk_id = lax.broadcasted_iota(jnp.int32, (nk, 1), 0).astype(F32)
    for h in range(n_heads):
        k_max = jnp.sqrt(k_sq[:, h:h + 1])
        c_end = ck_ref[0, h, :, tk - 1:tk]
        for i in range(nq):
            first_diag = float(i * nsub)
            q_max = jnp.sqrt(q_sq[i:i + 1, h:h + 1])
            k_own = jnp.max(k_max[i * nsub:(i + 1) * nsub], axis=0, keepdims=True)
            c_start = ck_ref[0, h, i * nsub:i * nsub + 1, 0:1]
            bound = q_max * (k_max + k_own) + (c_start - c_end)
            visit = jnp.logical_and(bound >= -FOX_SKIP_LOG2, blk_id < first_diag)
            first_blk[h * nq + i] = jnp.min(jnp.where(visit, blk_id, first_diag)).astype(jnp.int32)

    def tile_rows(qi):
        return pl.ds(pl.multiple_of(qi * tq, tq), tq)

    def start_tile(qi):
        qp = [q_ref[tile_rows(qi), p * LANES:(p + 1) * LANES] for p in range(n_heads // 2)]
        for r in range(nsub):
            rows = slice(r * tk, tq)
            for h in range(n_heads):
                block(qp[h // 2][rows], qi * nsub + r, rows, h, True, first=r == 0)
        return qp

    def finish_tile(qi):
        for p in range(n_heads // 2):
            a0 = acc_scr[:, (2 * p) * LANES:(2 * p + 1) * LANES]
            a1 = acc_scr[:, (2 * p + 1) * LANES:(2 * p + 2) * LANES]
            sums = pltpu.roll(jnp.where(lane < dh, a1, a0), dh, axis=1)
            out = jnp.where(lane < dh, a0, a1) / sums
            o_ref[tile_rows(qi), p * LANES:(p + 1) * LANES] = out.astype(o_ref.dtype)

    def qblock(qi, _):
        finish_tile(qi - 1)
        qp = start_tile(qi)

        def full(j, _):
            for h in range(n_heads):
                block(qp[h // 2], j, all_rows, h, False)
            return 0

        starts = [first_blk[h * nq + qi] for h in range(n_heads)]
        common = functools.reduce(jnp.maximum, starts)
        for h in range(n_heads):
            def one(j, _, h=h):
                block(qp[h // 2], j, all_rows, h, False)
                return 0

            def two(t, _, h=h):
                block(qp[h // 2], starts[h] + 2 * t, all_rows, h, False)
                block(qp[h // 2], starts[h] + 2 * t + 1, all_rows, h, False)
                return 0

            pairs = lax.shift_right_logical(common - starts[h], 1)
            lax.fori_loop(0, pairs, two, 0)
            lax.fori_loop(starts[h] + 2 * pairs, common, one, 0)
        lax.fori_loop(common, qi * nsub, full, 0)
        return 0

    start_tile(jnp.int32(0))
    lax.fori_loop(1, nq, qblock, 0)
    finish_tile(jnp.int32(nq - 1))


def _fox(q, k, v, cum_t, bsz, seq):
    t, w = q.shape
    assert FOX_TQ % FOX_TK == 0 and seq % FOX_TQ == 0 and FOX_HEADS_PER_STEP % 2 == 0
    hps = FOX_HEADS_PER_STEP
    wblk = hps * FOX_HEAD_DIM
    blk = pl.BlockSpec((seq, wblk), lambda bi, hi: (bi, hi))
    return pl.pallas_call(
        _fox_kernel,
        grid=(bsz, w // wblk),
        in_specs=[blk, blk, blk,
                  pl.BlockSpec((1, hps) + cum_t.shape[2:], lambda bi, hi: (bi, hi, 0, 0))],
        out_specs=blk,
        out_shape=jax.ShapeDtypeStruct((t, w), BF16),
        scratch_shapes=[pltpu.VMEM((FOX_TQ, hps * LANES), F32),
                        pltpu.VMEM((FOX_TQ, hps * LANES), F32),
                        pltpu.SMEM((hps * (seq // FOX_TQ),), jnp.int32)],
        compiler_params=_cparams(("arbitrary", "arbitrary")),
        name="fox",
    )(q, k, v, cum_t)


def _merge_kernel(x_ref, oa_ref, att_ref, gmix_ref, wg_ref, wfo_ref, wmix_ref, gq_ref, wq_ref,
                  km_ref, vm_ref, wo_ref, h_ref):
    d = x_ref.shape[-1]
    x = x_ref[...]
    gate = _sigmoid(_dot(_rms(x, gmix_ref[...]).astype(BF16), wg_ref[...]))
    out_b = _dot(att_ref[...], wfo_ref[...])
    mix = gate[:, :d] * oa_ref[...] + gate[:, d:] * out_b
    h1 = x + _dot(mix.astype(BF16), wmix_ref[...])

    n = _rms(h1, gq_ref[...]).astype(BF16)
    qm = _dot(n, wq_ref[...])
    hd = qm.shape[-1] // MEM_HEADS
    qm = (qm * (hd ** -0.5)).astype(BF16)
    outs = []
    for hh in range(MEM_HEADS):
        hs = slice(hh * hd, (hh + 1) * hd)
        s = _dot_nt(qm[:, hs], km_ref[0, :, hs])
        s = s - jnp.max(s, axis=-1, keepdims=True)
        p = jnp.exp(s)
        p = p / jnp.sum(p, axis=-1, keepdims=True)
        outs.append(_dot(p.astype(BF16), vm_ref[0, :, hs]))
    o = jnp.concatenate(outs, axis=-1).astype(BF16)
    h_ref[...] = h1 + _dot(o, wo_ref[...])


def _ffn_kernel(h_ref, gf_ref, win_ref, wout_ref, gfin_ref, o_ref):
    h2 = h_ref[...]
    f = _rms(h2, gf_ref[...]).astype(BF16)
    hidden = wout_ref.shape[0]
    acc = jnp.zeros_like(h2)
    for lo, hi in _ffn_chunks(hidden):
        fa = _dot(f, win_ref[:, lo:hi])
        fb = _dot(f, win_ref[:, hidden + lo:hidden + hi])
        g = (fa * _sigmoid(fa) * fb).astype(BF16)
        acc = acc + _dot(g, wout_ref[lo:hi, :])
    o_ref[...] = _rms(h2 + acc, gfin_ref[...])


def _ffn_chunks(hidden):
    tiles = hidden // MXU_TILE
    assert tiles * MXU_TILE == hidden
    cuts = [-(-tiles * c // FFN_CHUNKS) * MXU_TILE for c in range(FFN_CHUNKS + 1)]
    return list(zip(cuts[:-1], cuts[1:]))


def _resident(shape):
    n = len(shape)
    return pl.BlockSpec(shape, lambda *_: (0,) * n, pipeline_mode=pl.Buffered(1))


def _post(x2, out_a, att, gain_mix, w_gate, w_fox_o, w_mix, gain_q, w_q, k_m, v_m, w_o,
          gain_f, w_ffn_in, w_ffn_out, gain_fin, bsz, seq):
    t, d = x2.shape
    tm = TM_POST
    nb = seq // tm
    row = lambda w: pl.BlockSpec((tm, w), lambda b, i: (b * nb + i, 0))
    mem = pl.BlockSpec((1,) + k_m.shape[1:], lambda b, i: (b, 0, 0))
    gain = lambda g: g.reshape(1, d)
    weights = lambda *ws: [_resident(w.shape) for w in ws]
    h2 = pl.pallas_call(
        _merge_kernel,
        grid=(bsz, nb),
        in_specs=[row(d), row(d), row(att.shape[1]), _full((1, d))]
                 + weights(w_gate, w_fox_o, w_mix) + [_full((1, d))] + weights(w_q)
                 + [mem, mem] + weights(w_o),
        out_specs=row(d),
        out_shape=jax.ShapeDtypeStruct((t, d), F32),
        compiler_params=_cparams(("arbitrary", "arbitrary")),
        name="merge",
    )(x2, out_a, att, gain(gain_mix), w_gate, w_fox_o, w_mix, gain(gain_q), w_q, k_m, v_m, w_o)
    blk = pl.BlockSpec((tm, d), lambda i: (i, 0))
    return pl.pallas_call(
        _ffn_kernel,
        grid=(t // tm,),
        in_specs=[blk, _full((1, d))] + weights(w_ffn_in, w_ffn_out) + [_full((1, d))],
        out_specs=blk,
        out_shape=jax.ShapeDtypeStruct((t, d), F32),
        compiler_params=_cparams(("arbitrary",)),
        name="ffn",
    )(h2, gain(gain_f), w_ffn_in, w_ffn_out, gain(gain_fin))


def _seg_perm(q):
    seg = q // SUBLANES
    r = jnp.arange(q)
    src = (r % SUBLANES) * seg + r // SUBLANES
    return (src[:, None] == jnp.arange(q)[None, :]).astype(BF16)


def _layer(x2, mem, bsz, seq, norm_mix, w_in, b_forget, lam_re, lam_im, log_dt, b_re, b_im, c_re,
           c_im, d_skip, w_glu, w_fox_o, w_mix_out, norm_mem_q, norm_mem_kv, w_mem_q, w_mem_kv,
           w_mem_o, norm_ffn, w_ffn_in, w_ffn_out, norm_final):
    d = x2.shape[-1]
    ssm_w = lam_re.shape[0] * SSM_GROUP
    n_heads = b_forget.shape[0]
    fox_w = n_heads * FOX_HEAD_DIM

    assert b_re.shape[-1] == SSM_GROUP
    a_re, a_im, s_re, s_im, bd_bre, bd_bim, bd_cre, bd_cim = _ssm_prep(
        lam_re, lam_im, log_dt, b_re, b_im, c_re, c_im)
    perm = _seg_perm(SSM_CHUNK)

    bf = jnp.zeros((1, LANES), F32).at[0, :n_heads].set(b_forget)
    ssm_consts = [perm, perm.T, bd_bre, bd_bim, a_re, a_im, s_re, s_im, bd_cre, bd_cim,
                  d_skip.reshape(1, ssm_w), w_glu.astype(BF16)]
    post_weights = [w_fox_o, w_mix_out, w_mem_q, w_mem_o, w_ffn_in, w_ffn_out]
    q, k, v, cum_t, out_a, w_gate, *post_bf16 = _in_ssm(
        x2, norm_mix, w_in, bf, ssm_consts, post_weights, ssm_w, fox_w, n_heads, bsz, seq)
    wb_fox_o, wb_mix, wb_mem_q, wb_mem_o, wb_ffn_in, wb_ffn_out = post_bf16

    att = _fox(q, k, v, cum_t.reshape(bsz, n_heads, seq // FOX_TK, FOX_TK), bsz, seq)

    k_m, v_m = _mem_kv(mem, norm_mem_kv, w_mem_kv)
    return _post(x2, out_a, att, norm_mix, w_gate, wb_fox_o, wb_mix, norm_mem_q, wb_mem_q, k_m, v_m,
                 wb_mem_o, norm_ffn, wb_ffn_in, wb_ffn_out, norm_final, bsz, seq)


def kernel(x, mem, norm_mix, w_in, b_forget, lam_re, lam_im, log_dt, b_re, b_im, c_re, c_im, d_skip,
           w_glu, w_fox_o, w_mix_out, norm_mem_q, norm_mem_kv, w_mem_q, w_mem_kv, w_mem_o, norm_ffn,
           w_ffn_in, w_ffn_out, norm_final):
    bsz, seq, d = x.shape
    assert w_in.shape[0] == 1, "single-layer block"
    out = _layer(x.reshape(bsz * seq, d), mem, bsz, seq, norm_mix[0], w_in[0], b_forget[0],
                 lam_re[0], lam_im[0], log_dt[0], b_re[0], b_im[0], c_re[0], c_im[0], d_skip[0],
                 w_glu[0], w_fox_o[0], w_mix_out[0], norm_mem_q[0], norm_mem_kv[0], w_mem_q[0],
                 w_mem_kv[0], w_mem_o[0], norm_ffn[0], w_ffn_in[0], w_ffn_out[0], norm_final)
    return out.reshape(bsz, seq, d)
```

```python
import functools
import math

import jax
import jax.numpy as jnp
from jax import lax
from jax.experimental import pallas as pl
from jax.experimental.pallas import tpu as pltpu

F32 = jnp.float32
BF16 = jnp.bfloat16

RMS_EPS = 1e-6
LOG2E = math.log2(math.e)
SSM_GROUP = 16
FOX_HEAD_DIM = 64
MEM_HEADS = 4
LANES = 128
SUBLANES = 8
MXU_TILE = 256
VMEM_LIMIT = 56 * 1024 * 1024

TM_PROJ = 512
TM_POST = 1024
FFN_CHUNKS = 2
SSM_CHUNK = 256
SSM_SEG = SSM_CHUNK // SUBLANES
SSM_COLS = 512
FOX_TQ = 1024
FOX_TK = 512
FOX_HEADS_PER_STEP = 4
FOX_SKIP_LOG2 = 160.0
FOX_NORM_SLACK = 1.02


def _cparams(sem):
    return pltpu.CompilerParams(dimension_semantics=sem, vmem_limit_bytes=VMEM_LIMIT)


def _rms(x, gain):
    return x * lax.rsqrt(jnp.mean(x * x, axis=-1, keepdims=True) + RMS_EPS) * gain


def _dot(a, b):
    return jnp.dot(a, b, preferred_element_type=F32)


def _dot_nt(a, b):
    return lax.dot_general(a, b, (((1,), (1,)), ((), ())), preferred_element_type=F32)


def _full(shape):
    n = len(shape)
    return pl.BlockSpec(shape, lambda *_: (0,) * n)


def _ssm_prep_kernel(lre_ref, lim_ref, ldt_ref, bre_ref, bim_ref, cre_ref, cim_ref,
                     are_ref, aim_ref, sre_ref, sim_ref, bbre_ref, bbim_ref, ccre_ref, ccim_ref):
    lre = lre_ref[...]
    lim = lim_ref[...]
    dt = jnp.exp(ldt_ref[...])
    zr = lre * dt
    zi = lim * dt
    mag = jnp.exp(zr)
    are = mag * jnp.cos(zi)
    aim = mag * jnp.sin(zi)
    are_ref[...] = are
    aim_ref[...] = aim
    mag_s = jnp.exp(zr * SSM_SEG)
    sre_ref[...] = mag_s * jnp.cos(zi * SSM_SEG)
    sim_ref[...] = mag_s * jnp.sin(zi * SSM_SEG)
    nr = are - 1.0
    ni = aim
    den = lre * lre + lim * lim
    fr = (nr * lre + ni * lim) / den
    fi = (ni * lre - nr * lim) / den
    bre = bre_ref[...]
    bim = bim_ref[...]
    nblk, rows, cols = bbre_ref.shape
    n = bre.shape[0]
    per = rows // n
    same_group = (lax.broadcasted_iota(jnp.int32, (rows, cols), 0) // n
                  == lax.broadcasted_iota(jnp.int32, (rows, cols), 1) // (cols // per))

    def block_diag(x, kb):
        slab = x[:, kb * cols:(kb + 1) * cols]
        return jnp.where(same_group, jnp.concatenate([slab] * per, axis=0), 0.0)

    b_bar = (fr * bre - fi * bim, fr * bim + fi * bre)
    for kb in range(nblk):
        bbre_ref[kb] = block_diag(b_bar[0], kb).astype(BF16)
        bbim_ref[kb] = block_diag(b_bar[1], kb).astype(BF16)
        ccre_ref[kb] = block_diag(cre_ref[...], kb).T.astype(BF16)
        ccim_ref[kb] = block_diag(cim_ref[...], kb).T.astype(BF16)


def _ssm_prep(lam_re, lam_im, log_dt, b_re, b_im, c_re, c_im):
    g, p = lam_re.shape
    n = b_re.shape[-1]
    c = g * p
    per = LANES // n
    nblk = g // per
    row = lambda a: a.reshape(1, c)
    ldt = jnp.broadcast_to(log_dt[:, None], (g, p))
    blk = lambda r, cc: jax.ShapeDtypeStruct((nblk, r, cc), BF16)
    return pl.pallas_call(
        _ssm_prep_kernel,
        out_shape=[jax.ShapeDtypeStruct((1, c), F32)] * 4
                  + [blk(LANES, per * p)] * 2 + [blk(per * p, LANES)] * 2,
        name="ssm_prep",
    )(row(lam_re), row(lam_im), row(ldt),
      b_re.transpose(2, 0, 1).reshape(n, c), b_im.transpose(2, 0, 1).reshape(n, c),
      c_re.transpose(1, 0, 2).reshape(n, c), c_im.transpose(1, 0, 2).reshape(n, c))


def _mem_kv_kernel(m_ref, g_ref, w_ref, k_ref, v_ref):
    n = _rms(m_ref[0], g_ref[...]).astype(BF16)
    kv = _dot(n, w_ref[...].astype(BF16))
    half = kv.shape[-1] // 2
    k_ref[0] = kv[:, :half].astype(BF16)
    v_ref[0] = kv[:, half:].astype(BF16)


def _mem_kv(mem, gain, w_kv):
    b, m, d = mem.shape
    w2 = w_kv.shape[-1]
    return pl.pallas_call(
        _mem_kv_kernel,
        grid=(b,),
        in_specs=[pl.BlockSpec((1, m, d), lambda i: (i, 0, 0)), _full((1, d)), _full((d, w2))],
        out_specs=[pl.BlockSpec((1, m, w2 // 2), lambda i: (i, 0, 0))] * 2,
        out_shape=[jax.ShapeDtypeStruct((b, m, w2 // 2), BF16)] * 2,
        compiler_params=_cparams(("arbitrary",)),
        name="mem_kv",
    )(mem, gain.reshape(1, d), w_kv)


def _gelu_tanh(x):
    return 0.5 * x * (1.0 + jnp.tanh(math.sqrt(2.0 / math.pi) * (x + 0.044715 * (x * x * x))))


def _sigmoid(x):
    return 0.5 * jnp.tanh(0.5 * x) + 0.5


def _ssm_drive(u, perm_ref, bre_ref, bim_ref, h_scr):
    n_chunks = u.shape[0] // SSM_CHUNK
    ub = u.astype(BF16)
    up = jnp.concatenate(
        [_dot(perm_ref[...], ub[c * SSM_CHUNK:(c + 1) * SSM_CHUNK]).astype(BF16)
         for c in range(n_chunks)], axis=0)
    kw = bre_ref.shape[1]
    nw = bre_ref.shape[2]
    for kb in range(bre_ref.shape[0]):
        ukb = up[:, kb * kw:(kb + 1) * kw]
        for part, w_ref in enumerate((bre_ref, bim_ref)):
            bu = _dot(ukb, w_ref[kb])
            for c in range(n_chunks):
                h_scr[2 * c + part][:, kb * nw:(kb + 1) * nw] = bu[c * SSM_CHUNK:(c + 1) * SSM_CHUNK]


def _ssm_chunk(u, permt_ref, are_ref, aim_ref, sre_ref, sim_ref, cre_ref, cim_ref, d_ref,
               hre, him, car_re, car_im, anchors=()):
    if callable(anchors):
        anchors = anchors()

    nstate = hre.shape[1]
    for cb in range(nstate // SSM_COLS):
        cs = slice(cb * SSM_COLS, (cb + 1) * SSM_COLS)
        a_re = jnp.broadcast_to(are_ref[:, cs], (SUBLANES, SSM_COLS))
        a_im = jnp.broadcast_to(aim_ref[:, cs], (SUBLANES, SSM_COLS))

        e_re = hre[0:SUBLANES, cs]
        e_im = him[0:SUBLANES, cs]
        for k in range(1, SSM_SEG):
            r = slice(k * SUBLANES, (k + 1) * SUBLANES)
            e_re, e_im = (a_re * e_re - a_im * e_im + hre[r, cs],
                          a_re * e_im + a_im * e_re + him[r, cs])
            hre[r, cs] = e_re
            him[r, cs] = e_im

        s_re = sre_ref[:, cs]
        s_im = sim_ref[:, cs]
        c_re = car_re[:, cs]
        c_im = car_im[:, cs]
        rows_re, rows_im = [], []
        for j in range(SUBLANES):
            rows_re.append(c_re)
            rows_im.append(c_im)
            n_re = s_re * c_re - s_im * c_im + e_re[j:j + 1]
            n_im = s_re * c_im + s_im * c_re + e_im[j:j + 1]
            c_re, c_im = n_re, n_im
        car_re[:, cs] = c_re
        car_im[:, cs] = c_im

        d_re = jnp.concatenate(rows_re, axis=0)
        d_im = jnp.concatenate(rows_im, axis=0)
        for k in range(SSM_SEG):
            r = slice(k * SUBLANES, (k + 1) * SUBLANES)
            d_re, d_im = a_re * d_re - a_im * d_im, a_re * d_im + a_im * d_re
            if k == SSM_SEG // 2:
                for z in anchors[cb::nstate // SSM_COLS]:
                    d_re = d_re + jnp.concatenate([z] * (SSM_COLS // LANES), axis=1)
            hre[r, cs] = hre[r, cs] + d_re
            him[r, cs] = him[r, cs] + d_im

    ncb = cre_ref.shape[0]
    cw = cre_ref.shape[1]
    ys = []
    for kb in range(ncb):
        h_re = hre[:, kb * cw:(kb + 1) * cw].astype(BF16)
        h_im = him[:, kb * cw:(kb + 1) * cw].astype(BF16)
        ys.append(_dot(h_re, cre_ref[kb]) - _dot(h_im, cim_ref[kb]))
    yp = jnp.concatenate(ys, axis=-1)
    hi = yp.astype(BF16)
    lo = (yp - hi.astype(F32)).astype(BF16)
    y = _dot(permt_ref[...], hi) + _dot(permt_ref[...], lo)
    y = y + d_ref[...] * u
    return _gelu_tanh(y).astype(BF16)


def _glu(g, wglu_ref):
    z = _dot(g, wglu_ref[...])
    half = z.shape[-1] // 2
    return z[:, :half] * jax.nn.sigmoid(z[:, half:])


def _schedule_anchor(x):
    rows, cols = x.shape
    s = x[:, :LANES]
    for c in range(1, cols // LANES):
        s = s + x[:, c * LANES:(c + 1) * LANES]
    s = jnp.sum(s.reshape(rows // SUBLANES, SUBLANES, LANES), axis=0)
    bits = pltpu.bitcast(s, jnp.uint32)
    return ((bits >> 16) >> 16).astype(F32)


def _in_ssm_kernel(x_ref, g_ref, wa_ref, wb_ref, wc_ref, bf_ref, *rest, n_cast):
    nconst = 12
    (perm_ref, permt_ref, bre_ref, bim_ref, are_ref, aim_ref, sre_ref, sim_ref,
     cre_ref, cim_ref, d_ref, wglu_ref) = rest[:nconst]
    cast_in = rest[nconst:nconst + n_cast]
    outs = rest[nconst + n_cast:]
    q_ref, k_ref, v_ref, cum_ref, oa_ref, wg_ref = outs[:6]
    cast_out = outs[6:6 + n_cast]
    carry_ref, car_re, car_im, w_bf, wf_bf = outs[6 + n_cast:11 + n_cast]
    h_scr = outs[11 + n_cast:]
    n_forget = cum_ref.shape[1]
    for src, dst in zip(cast_in, cast_out):
        dst[...] = src[...].astype(BF16)

    @pl.when(pl.program_id(1) == 0)
    def _():
        carry_ref[...] = jnp.zeros_like(carry_ref)
        car_re[...] = jnp.zeros_like(car_re)
        car_im[...] = jnp.zeros_like(car_im)
        w_bf[...] = wa_ref[...].T.astype(BF16)
        wf_bf[...] = wb_ref[:LANES, :].T.astype(BF16)
        wg_ref[...] = jnp.concatenate([wb_ref[n_forget:, :], wc_ref[...]], axis=0).T.astype(BF16)

    un = _rms(x_ref[...], g_ref[...]).astype(BF16)
    wu = d_ref.shape[1]
    wh = q_ref.shape[1]
    col = lambda lo, hi: _dot(un, w_bf[:, lo:hi])
    u = col(0, wu)
    pieces = [(ref, lo + p * MXU_TILE, p * MXU_TILE, scale)
              for ref, lo, scale in ((q_ref, wu, FOX_HEAD_DIM ** -0.5 * LOG2E),
                                     (k_ref, wu + wh, None), (v_ref, wu + 2 * wh, None))
              for p in range(wh // MXU_TILE)]
    n_chunks = u.shape[0] // SSM_CHUNK
    per_chunk = -(-len(pieces) // n_chunks)

    def project(todo):
        anchors = []
        for ref, src, dst, scale in todo:
            piece = col(src, src + MXU_TILE)
            if scale is not None:
                piece = piece * scale
            ref[:, dst:dst + MXU_TILE] = piece.astype(BF16)
            anchors.append(_schedule_anchor(piece))
        return anchors

    _ssm_drive(u, perm_ref, bre_ref, bim_ref, h_scr)
    acts = []
    for c in range(n_chunks):
        rows = slice(c * SSM_CHUNK, (c + 1) * SSM_CHUNK)
        todo = pieces[c * per_chunk:(c + 1) * per_chunk]
        acts.append(_ssm_chunk(u[rows], permt_ref, are_ref, aim_ref, sre_ref, sim_ref, cre_ref,
                               cim_ref, d_ref, h_scr[2 * c], h_scr[2 * c + 1], car_re, car_im,
                               functools.partial(project, todo)))
    oa_ref[...] = _glu(jnp.concatenate(acts, axis=0), wglu_ref)
    c = jax.nn.log_sigmoid(_dot(un, wf_bf[...]) + bf_ref[...])
    rows = c.shape[0]
    row = lax.broadcasted_iota(jnp.int32, c.shape, 0)
    shift = 1
    while shift < rows:
        c = c + jnp.where(row >= shift, pltpu.roll(c, shift, 0), 0.0)
        shift *= 2
    c = c + carry_ref[...]
    carry_ref[...] = c[rows - 1:rows, :]
    cum_ref[0] = (c * LOG2E).T[:cum_ref.shape[1], :]


def _in_ssm(x2, gain, w_in, bf, ssm_consts, later_weights, ssm_w, fox_w, n_heads, bsz, seq):
    t, d = x2.shape
    tm = TM_PROJ
    nb = seq // tm
    steps = bsz * nb
    slabs = [w.reshape(steps, w.shape[0] // steps, w.shape[1]) for w in later_weights]
    slab = lambda s: pl.BlockSpec((1,) + s.shape[1:], lambda b, i: (b * nb + i, 0, 0))
    assert tm % SSM_CHUNK == 0 and len(ssm_consts) == 12
    nstate = ssm_consts[4].shape[1]
    dm = ssm_consts[11].shape[1] // 2
    main_w = ssm_w + 3 * fox_w
    gate_w = w_in.shape[1] - main_w - n_heads
    assert gate_w == main_w and main_w % LANES == 0 and n_heads == SUBLANES
    win = lambda rows, idx: pl.BlockSpec((rows, d), lambda *_: (idx, 0),
                                         pipeline_mode=pl.Buffered(1))
    w_t = w_in.T
    row = lambda w: pl.BlockSpec((tm, w), lambda b, i: (b * nb + i, 0))
    h_scr = [pltpu.VMEM((SSM_CHUNK, nstate), F32)] * (2 * (tm // SSM_CHUNK))
    outs = pl.pallas_call(
        functools.partial(_in_ssm_kernel, n_cast=len(slabs)),
        grid=(bsz, nb),
        in_specs=[row(d), _full((1, d)), win(main_w, 0), win(main_w, 1),
                  win(n_heads, 2 * main_w // n_heads), _full((1, LANES))]
                 + [_full(c.shape) for c in ssm_consts] + [slab(s) for s in slabs],
        out_specs=[row(fox_w), row(fox_w), row(fox_w),
                   pl.BlockSpec((1, n_heads, tm), lambda b, i: (b, 0, i)), row(dm),
                   _full((d, gate_w))] + [slab(s) for s in slabs],
        out_shape=[jax.ShapeDtypeStruct((t, fox_w), BF16),
                   jax.ShapeDtypeStruct((t, fox_w), BF16),
                   jax.ShapeDtypeStruct((t, fox_w), BF16),
                   jax.ShapeDtypeStruct((bsz, n_heads, seq), F32),
                   jax.ShapeDtypeStruct((t, dm), F32),
                   jax.ShapeDtypeStruct((d, gate_w), BF16)]
                  + [jax.ShapeDtypeStruct(s.shape, BF16) for s in slabs],
        scratch_shapes=[pltpu.VMEM((1, LANES), F32), pltpu.VMEM((1, nstate), F32),
                        pltpu.VMEM((1, nstate), F32), pltpu.VMEM((d, main_w), BF16),
                        pltpu.VMEM((d, LANES), BF16)] + h_scr,
        compiler_params=_cparams(("arbitrary", "arbitrary")),
        name="in_ssm",
    )(x2, gain.reshape(1, d), w_t, w_t, w_t, bf, *ssm_consts, *slabs)
    cast = [o.reshape(w.shape) for o, w in zip(outs[6:], later_weights)]
    return list(outs[:6]) + cast


def _fox_kernel(q_ref, k_ref, v_ref, ck_ref, o_ref, m_scr, acc_scr, first_blk):
    seq = q_ref.shape[0]
    tq = FOX_TQ
    tk = FOX_TK
    dh = FOX_HEAD_DIM
    n_heads = ck_ref.shape[1]
    lane = lax.broadcasted_iota(jnp.int32, (1, LANES), 1)
    keep = [jnp.where(lane < dh, 1.0, 0.0).astype(BF16), jnp.where(lane >= dh, 1.0, 0.0).astype(BF16)]

    def block(q, j, rows, h, masked, first=False):
        e = h % 2
        ps = slice((h // 2) * LANES, (h // 2 + 1) * LANES)
        hs = slice(h * LANES, (h + 1) * LANES)
        ks = pl.ds(pl.multiple_of(j * tk, tk), tk)
        s = _dot_nt(q, k_ref[ks, ps] * keep[e]) - ck_ref[0, h, pl.ds(j, 1), :]
        if masked:
            r = lax.broadcasted_iota(jnp.int32, (tk, tk), 0)
            c = lax.broadcasted_iota(jnp.int32, (tk, tk), 1)
            top = jnp.where(c <= r, s[:tk], -jnp.inf)
            s = top if s.shape[0] == tk else jnp.concatenate([top, s[tk:]], axis=0)
        m_new = jnp.broadcast_to(jnp.max(s, axis=1, keepdims=True), (s.shape[0], LANES))
        if not first:
            m_prev = m_scr[rows, hs]
            m_new = jnp.maximum(m_prev, m_new)
        p = jnp.exp2(s - jnp.concatenate([m_new] * (tk // LANES), axis=1))
        pv = _dot(p.astype(BF16), v_ref[ks, ps] * keep[e] + keep[1 - e])
        acc_scr[rows, hs] = pv if first else jnp.exp2(m_prev - m_new) * acc_scr[rows, hs] + pv
        m_scr[rows, hs] = m_new

    nsub = tq // tk
    nq = seq // tq
    nk = seq // tk
    all_rows = slice(0, tq)

    col = lax.broadcasted_iota(jnp.int32, (q_ref.shape[1], LANES), 0) // dh
    head_sel = jnp.where(col == lax.broadcasted_iota(jnp.int32, (q_ref.shape[1], LANES), 1),
                         1.0, 0.0).astype(BF16)

    def max_sq_norm(ref, lo, n):
        x = ref[lo:lo + n, :]
        sq = _dot(x * x, head_sel)
        return jnp.max(sq, axis=0, keepdims=True) * FOX_NORM_SLACK

    k_sq = jnp.concatenate([max_sq_norm(k_ref, j * tk, tk) for j in range(nk)], axis=0)
    q_sq = jnp.concatenate([max_sq_norm(q_ref, i * tq, tq) for i in range(nq)], axis=0)
    blk_id = lax.broadcasted_iota(jnp.int32, (nk, 1), 0).astype(F32)
    for h in range(n_heads):
        k_max = jnp.sqrt(k_sq[:, h:h + 1])
        c_end = ck_ref[0, h, :, tk - 1:tk]
        for i in range(nq):
            first_diag = float(i * nsub)
            q_max = jnp.sqrt(q_sq[i:i + 1, h:h + 1])
            k_own = jnp.max(k_max[i * nsub:(i + 1) * nsub], axis=0, keepdims=True)
            c_start = ck_ref[0, h, i * nsub:i * nsub + 1, 0:1]
            bound = q_max * (k_max + k_own) + (c_start - c_end)
            visit = jnp.logical_and(bound >= -FOX_SKIP_LOG2, blk_id < first_diag)
            first_blk[h * nq + i] = jnp.min(jnp.where(visit, blk_id, first_diag)).astype(jnp.int32)

    def tile_rows(qi):
        return pl.ds(pl.multiple_of(qi * tq, tq), tq)

    def start_tile(qi):
        qp = [q_ref[tile_rows(qi), p * LANES:(p + 1) * LANES] for p in range(n_heads // 2)]
        for r in range(nsub):
            rows = slice(r * tk, tq)
            for h in range(n_heads):
                block(qp[h // 2][rows], qi * nsub + r, rows, h, True, first=r == 0)
        return qp

    def finish_tile(qi):
        for p in range(n_heads // 2):
            a0 = acc_scr[:, (2 * p) * LANES:(2 * p + 1) * LANES]
            a1 = acc_scr[:, (2 * p + 1) * LANES:(2 * p + 2) * LANES]
            sums = pltpu.roll(jnp.where(lane < dh, a1, a0), dh, axis=1)
            out = jnp.where(lane < dh, a0, a1) / sums
            o_ref[tile_rows(qi), p * LANES:(p + 1) * LANES] = out.astype(o_ref.dtype)

    def qblock(qi, _):
        finish_tile(qi - 1)
        qp = start_tile(qi)

        def full(j, _):
            for h in range(n_heads):
                block(qp[h // 2], j, all_rows, h, False)
            return 0

        starts = [first_blk[h * nq + qi] for h in range(n_heads)]
        common = functools.reduce(jnp.maximum, starts)
        for h in range(n_heads):
            def one(j, _, h=h):
                block(qp[h // 2], j, all_rows, h, False)
                return 0

            def two(t, _, h=h):
                block(qp[h // 2], starts[h] + 2 * t, all_rows, h, False)
                block(qp[h // 2], starts[h] + 2 * t + 1, all_rows, h, False)
                return 0

            pairs = lax.shift_right_logical(common - starts[h], 1)
            lax.fori_loop(0, pairs, two, 0)
            lax.fori_loop(starts[h] + 2 * pairs, common, one, 0)
        lax.fori_loop(common, qi * nsub, full, 0)
        return 0

    start_tile(jnp.int32(0))
    lax.fori_loop(1, nq, qblock, 0)
    finish_tile(jnp.int32(nq - 1))


def _fox(q, k, v, cum_t, bsz, seq):
    t, w = q.shape
    assert FOX_TQ % FOX_TK == 0 and seq % FOX_TQ == 0 and FOX_HEADS_PER_STEP % 2 == 0
    hps = FOX_HEADS_PER_STEP
    wblk = hps * FOX_HEAD_DIM
    blk = pl.BlockSpec((seq, wblk), lambda bi, hi: (bi, hi))
    return pl.pallas_call(
        _fox_kernel,
        grid=(bsz, w // wblk),
        in_specs=[blk, blk, blk,
                  pl.BlockSpec((1, hps) + cum_t.shape[2:], lambda bi, hi: (bi, hi, 0, 0))],
        out_specs=blk,
        out_shape=jax.ShapeDtypeStruct((t, w), BF16),
        scratch_shapes=[pltpu.VMEM((FOX_TQ, hps * LANES), F32),
                        pltpu.VMEM((FOX_TQ, hps * LANES), F32),
                        pltpu.SMEM((hps * (seq // FOX_TQ),), jnp.int32)],
        compiler_params=_cparams(("arbitrary", "arbitrary")),
        name="fox",
    )(q, k, v, cum_t)


def _merge_kernel(x_ref, oa_ref, att_ref, gmix_ref, wg_ref, wfo_ref, wmix_ref, gq_ref, wq_ref,
                  km_ref, vm_ref, wo_ref, h_ref):
    d = x_ref.shape[-1]
    x = x_ref[...]
    gate = _sigmoid(_dot(_rms(x, gmix_ref[...]).astype(BF16), wg_ref[...]))
    out_b = _dot(att_ref[...], wfo_ref[...])
    mix = gate[:, :d] * oa_ref[...] + gate[:, d:] * out_b
    h1 = x + _dot(mix.astype(BF16), wmix_ref[...])

    n = _rms(h1, gq_ref[...]).astype(BF16)
    qm = _dot(n, wq_ref[...])
    hd = qm.shape[-1] // MEM_HEADS
    qm = (qm * (hd ** -0.5)).astype(BF16)
    outs = []
    for hh in range(MEM_HEADS):
        hs = slice(hh * hd, (hh + 1) * hd)
        s = _dot_nt(qm[:, hs], km_ref[0, :, hs])
        s = s - jnp.max(s, axis=-1, keepdims=True)
        p = jnp.exp(s)
        p = p / jnp.sum(p, axis=-1, keepdims=True)
        outs.append(_dot(p.astype(BF16), vm_ref[0, :, hs]))
    o = jnp.concatenate(outs, axis=-1).astype(BF16)
    h_ref[...] = h1 + _dot(o, wo_ref[...])


def _ffn_kernel(h_ref, gf_ref, win_ref, wout_ref, gfin_ref, o_ref):
    h2 = h_ref[...]
    f = _rms(h2, gf_ref[...]).astype(BF16)
    hidden = wout_ref.shape[0]
    acc = jnp.zeros_like(h2)
    for lo, hi in _ffn_chunks(hidden):
        fa = _dot(f, win_ref[:, lo:hi])
        fb = _dot(f, win_ref[:, hidden + lo:hidden + hi])
        g = (fa * jax.nn.sigmoid(fa) * fb).astype(BF16)
        acc = acc + _dot(g, wout_ref[lo:hi, :])
    o_ref[...] = _rms(h2 + acc, gfin_ref[...])


def _ffn_chunks(hidden):
    tiles = hidden // MXU_TILE
    assert tiles * MXU_TILE == hidden
    cuts = [-(-tiles * c // FFN_CHUNKS) * MXU_TILE for c in range(FFN_CHUNKS + 1)]
    return list(zip(cuts[:-1], cuts[1:]))


def _resident(shape):
    n = len(shape)
    return pl.BlockSpec(shape, lambda *_: (0,) * n, pipeline_mode=pl.Buffered(1))


def _post(x2, out_a, att, gain_mix, w_gate, w_fox_o, w_mix, gain_q, w_q, k_m, v_m, w_o,
          gain_f, w_ffn_in, w_ffn_out, gain_fin, bsz, seq):
    t, d = x2.shape
    tm = TM_POST
    nb = seq // tm
    row = lambda w: pl.BlockSpec((tm, w), lambda b, i: (b * nb + i, 0))
    mem = pl.BlockSpec((1,) + k_m.shape[1:], lambda b, i: (b, 0, 0))
    gain = lambda g: g.reshape(1, d)
    weights = lambda *ws: [_resident(w.shape) for w in ws]
    h2 = pl.pallas_call(
        _merge_kernel,
        grid=(bsz, nb),
        in_specs=[row(d), row(d), row(att.shape[1]), _full((1, d))]
                 + weights(w_gate, w_fox_o, w_mix) + [_full((1, d))] + weights(w_q)
                 + [mem, mem] + weights(w_o),
        out_specs=row(d),
        out_shape=jax.ShapeDtypeStruct((t, d), F32),
        compiler_params=_cparams(("arbitrary", "arbitrary")),
        name="merge",
    )(x2, out_a, att, gain(gain_mix), w_gate, w_fox_o, w_mix, gain(gain_q), w_q, k_m, v_m, w_o)
    blk = pl.BlockSpec((tm, d), lambda i: (i, 0))
    return pl.pallas_call(
        _ffn_kernel,
        grid=(t // tm,),
        in_specs=[blk, _full((1, d))] + weights(w_ffn_in, w_ffn_out) + [_full((1, d))],
        out_specs=blk,
        out_shape=jax.ShapeDtypeStruct((t, d), F32),
        compiler_params=_cparams(("arbitrary",)),
        name="ffn",
    )(h2, gain(gain_f), w_ffn_in, w_ffn_out, gain(gain_fin))


def _seg_perm(q):
    seg = q // SUBLANES
    r = jnp.arange(q)
    src = (r % SUBLANES) * seg + r // SUBLANES
    return (src[:, None] == jnp.arange(q)[None, :]).astype(BF16)


def _layer(x2, mem, bsz, seq, norm_mix, w_in, b_forget, lam_re, lam_im, log_dt, b_re, b_im, c_re,
           c_im, d_skip, w_glu, w_fox_o, w_mix_out, norm_mem_q, norm_mem_kv, w_mem_q, w_mem_kv,
           w_mem_o, norm_ffn, w_ffn_in, w_ffn_out, norm_final):
    d = x2.shape[-1]
    ssm_w = lam_re.shape[0] * SSM_GROUP
    n_heads = b_forget.shape[0]
    fox_w = n_heads * FOX_HEAD_DIM

    assert b_re.shape[-1] == SSM_GROUP
    a_re, a_im, s_re, s_im, bd_bre, bd_bim, bd_cre, bd_cim = _ssm_prep(
        lam_re, lam_im, log_dt, b_re, b_im, c_re, c_im)
    perm = _seg_perm(SSM_CHUNK)

    bf = jnp.zeros((1, LANES), F32).at[0, :n_heads].set(b_forget)
    ssm_consts = [perm, perm.T, bd_bre, bd_bim, a_re, a_im, s_re, s_im, bd_cre, bd_cim,
                  d_skip.reshape(1, ssm_w), w_glu.astype(BF16)]
    post_weights = [w_fox_o, w_mix_out, w_mem_q, w_mem_o, w_ffn_in, w_ffn_out]
    q, k, v, cum_t, out_a, w_gate, *post_bf16 = _in_ssm(
        x2, norm_mix, w_in, bf, ssm_consts, post_weights, ssm_w, fox_w, n_heads, bsz, seq)
    wb_fox_o, wb_mix, wb_mem_q, wb_mem_o, wb_ffn_in, wb_ffn_out = post_bf16

    att = _fox(q, k, v, cum_t.reshape(bsz, n_heads, seq // FOX_TK, FOX_TK), bsz, seq)

    k_m, v_m = _mem_kv(mem, norm_mem_kv, w_mem_kv)
    return _post(x2, out_a, att, norm_mix, w_gate, wb_fox_o, wb_mix, norm_mem_q, wb_mem_q, k_m, v_m,
                 wb_mem_o, norm_ffn, wb_ffn_in, wb_ffn_out, norm_final, bsz, seq)


def kernel(x, mem, norm_mix, w_in, b_forget, lam_re, lam_im, log_dt, b_re, b_im, c_re, c_im, d_skip,
           w_glu, w_fox_o, w_mix_out, norm_mem_q, norm_mem_kv, w_mem_q, w_mem_kv, w_mem_o, norm_ffn,
           w_ffn_in, w_ffn_out, norm_final):
    bsz, seq, d = x.shape
    assert w_in.shape[0] == 1, "single-layer block"
    out = _layer(x.reshape(bsz * seq, d), mem, bsz, seq, norm_mix[0], w_in[0], b_forget[0],
                 lam_re[0], lam_im[0], log_dt[0], b_re[0], b_im[0], c_re[0], c_im[0], d_skip[0],
                 w_glu[0], w_fox_o[0], w_mix_out[0], norm_mem_q[0], norm_mem_kv[0], w_mem_q[0],
                 w_mem_kv[0], w_mem_o[0], norm_ffn[0], w_ffn_in[0], w_ffn_out[0], norm_final)
    return out.reshape(bsz, seq, d)
```
